```python
import jax, jax.numpy as jnp
from jax import lax
import numpy as np

D_MODEL = 2048
BATCH = 4
SEQ = 2048
DEPTH = 1

CHUNK = 64
H_A = 8
DK_A = 128
DV_A = 128
D_A = H_A * DK_A
H_B = 8
DH_B = 128
D_B = H_B * DH_B
N_PAST_CHUNKS = 8
BAND = N_PAST_CHUNKS + 1
REL_FUTURE = CHUNK - 1
REL_PAST = 2 * CHUNK - 1
N_REL = REL_FUTURE + REL_PAST + 1
D_FF = -(-8 * D_MODEL // (3 * 256)) * 256
N_IN = 4 * D_A + 3 * D_B + 2 * D_MODEL
EPS = 1e-6

kernel_name = "hybrid_hgrn2_chunkattn_gated_block"


def rms_norm(x, gain):
    xf = x.astype(jnp.float32)
    y = xf * lax.rsqrt(jnp.mean(xf * xf, axis=-1, keepdims=True) + EPS)
    return (y * gain.astype(jnp.float32)).astype(x.dtype)


def hgrn_lower_bounds(lb_logits):
    p = jax.nn.softmax(lb_logits.astype(jnp.float32), axis=0)
    return jnp.cumsum(p, axis=0)[:DEPTH]


def hgrn2_mixer(q, f_logit, i, g, lb, out_gain):
    B, T, _ = q.shape
    n_chunks = T // CHUNK
    f32 = jnp.float32
    lbf = lb.astype(f32)
    f = lbf + (1.0 - lbf) * jax.nn.sigmoid(f_logit.astype(f32))
    log_f = jnp.log(f)
    k = 1.0 - f
    qs = jax.nn.silu(q.astype(f32))

    def to_chunks(t, dh):
        return t.reshape(B, n_chunks, CHUNK, H_A, dh).transpose(1, 0, 3, 2, 4)

    qc, kc, lfc = to_chunks(qs, DK_A), to_chunks(k, DK_A), to_chunks(log_f, DK_A)
    vc = to_chunks(i.astype(f32), DV_A)
    causal = jnp.tril(jnp.ones((CHUNK, CHUNK), dtype=bool))[:, :, None]

    def step(S, inp):
        qj, kj, lfj, vj = inp
        b = jnp.cumsum(lfj, axis=2)
        o_inter = jnp.einsum('bhtk,bhkv->bhtv', qj * jnp.exp(b), S)
        rel = jnp.where(causal, b[:, :, :, None, :] - b[:, :, None, :, :], -jnp.inf)
        decay = jnp.exp(rel)
        scores = jnp.einsum('bhtk,bhsk,bhtsk->bhts', qj, kj, decay)
        o_intra = jnp.einsum('bhts,bhsv->bhtv', scores, vj)
        b_last = b[:, :, -1:, :]
        S_new = jnp.exp(b_last[:, :, 0, :, None]) * S + jnp.einsum(
            'bhsk,bhsv->bhkv', kj * jnp.exp(b_last - b), vj)
        return S_new, o_inter + o_intra

    S0 = jnp.zeros((B, H_A, DK_A, DV_A), f32)
    _, o = lax.scan(step, S0, (qc, kc, lfc, vc))
    o = o.transpose(1, 0, 3, 2, 4).reshape(B, T, H_A, DV_A)
    o = o * lax.rsqrt(jnp.mean(o * o, axis=-1, keepdims=True) + EPS)
    o = o.reshape(B, T, D_A) * out_gain.astype(f32)
    o = o * jax.nn.silu(g.astype(f32))
    return o.astype(q.dtype)


def head_rms_norm(t, gain):
    tf = t.astype(jnp.float32)
    y = tf * lax.rsqrt(jnp.mean(tf * tf, axis=-1, keepdims=True) + EPS)
    return y * gain.astype(jnp.float32)


def chunked_relpos_attention(q, k, v, q_gain, k_gain, rel_bias):
    B, T, _ = q.shape
    n_chunks = T // CHUNK

    def heads(t):
        return t.reshape(B, n_chunks, CHUNK, H_B, DH_B).transpose(0, 3, 1, 2, 4)

    qh = head_rms_norm(heads(q), q_gain)
    kh = head_rms_norm(heads(k), k_gain)
    vh = heads(v).astype(jnp.float32)

    pad = ((0, 0), (0, 0), (N_PAST_CHUNKS, 0), (0, 0), (0, 0))
    band_idx = jnp.arange(n_chunks)[:, None] + jnp.arange(BAND)[None, :]
    k_band = jnp.pad(kh, pad)[:, :, band_idx].reshape(B, H_B, n_chunks, BAND * CHUNK, DH_B)
    v_band = jnp.pad(vh, pad)[:, :, band_idx].reshape(B, H_B, n_chunks, BAND * CHUNK, DH_B)

    q_pos = jnp.arange(n_chunks)[:, None] * CHUNK + jnp.arange(CHUNK)[None, :]
    k_chunk = band_idx - N_PAST_CHUNKS
    k_pos = (k_chunk[:, :, None] * CHUNK + jnp.arange(CHUNK)[None, None, :]).reshape(
        n_chunks, BAND * CHUNK)
    valid = k_pos >= 0
    dist = q_pos[:, :, None] - k_pos[:, None, :]
    rel_idx = jnp.clip(dist, -REL_FUTURE, REL_PAST) + REL_FUTURE
    bias = rel_bias.astype(jnp.float32)[:, rel_idx]

    scale = DH_B ** -0.5
    scores = jnp.einsum('bhnqd,bhnkd->bhnqk', qh, k_band) * scale + bias[None]
    scores = jnp.where(valid[None, None, :, None, :], scores, -jnp.inf)
    p = jax.nn.softmax(scores, axis=-1)
    o = jnp.einsum('bhnqk,bhnkd->bhnqd', p, v_band)
    return o.transpose(0, 2, 3, 1, 4).reshape(B, T, D_B).astype(q.dtype)


def setup_inputs(seed: int = 0) -> dict:
    key = jax.random.key(seed)
    ks = jax.random.split(key, 16)
    f32 = jnp.float32

    def nrm(k, shape, scale):
        return jax.random.normal(k, shape, f32) * scale

    return {
        "x": nrm(ks[0], (BATCH, SEQ, D_MODEL), 1.0),
        "w_in": nrm(ks[1], (DEPTH, D_MODEL, N_IN), D_MODEL ** -0.5),
        "b_gate": nrm(ks[2], (DEPTH, 2 * D_MODEL), 0.02),
        "norm_mix": 1.0 + nrm(ks[3], (DEPTH, D_MODEL), 0.02),
        "norm_ffn": 1.0 + nrm(ks[4], (DEPTH, D_MODEL), 0.02),
        "hgrn_lb_logits": nrm(ks[5], (DEPTH + 1, D_A), 0.5),
        "hgrn_out_gain": 1.0 + nrm(ks[6], (DEPTH, D_A), 0.02),
        "q_gain": 1.0 + nrm(ks[7], (DEPTH, DH_B), 0.02),
        "k_gain": 1.0 + nrm(ks[8], (DEPTH, DH_B), 0.02),
        "rel_bias": nrm(ks[9], (DEPTH, H_B, N_REL), 0.1),
        "w_proj_a": nrm(ks[10], (DEPTH, D_A, D_MODEL), D_A ** -0.5),
        "w_proj_b": nrm(ks[11], (DEPTH, D_B, D_MODEL), D_B ** -0.5),
        "w_out": nrm(ks[12], (DEPTH, D_MODEL, D_MODEL), D_MODEL ** -0.5),
        "w_ffn_in": nrm(ks[13], (DEPTH, D_MODEL, 2 * D_FF), D_MODEL ** -0.5),
        "w_ffn_out": nrm(ks[14], (DEPTH, D_FF, D_MODEL), D_FF ** -0.5),
    }


def reference(x, w_in, b_gate, norm_mix, norm_ffn, hgrn_lb_logits, hgrn_out_gain,
              q_gain, k_gain, rel_bias, w_proj_a, w_proj_b, w_out, w_ffn_in, w_ffn_out):
    lower_bounds = hgrn_lower_bounds(hgrn_lb_logits)
    split_pts = [D_A, 2 * D_A, 3 * D_A, 4 * D_A,
                 4 * D_A + D_B, 4 * D_A + 2 * D_B, 4 * D_A + 3 * D_B]
    for l in range(DEPTH):
        h = rms_norm(x, norm_mix[l])
        proj = jnp.einsum('btd,dn->btn', h, w_in[l])
        q_a, f_a, i_a, g_a, q_b, k_b, v_b, gate_logits = jnp.split(proj, split_pts, axis=-1)
        gates = jax.nn.sigmoid((gate_logits + b_gate[l]).astype(jnp.float32)).astype(x.dtype)
        gate_a, gate_b = jnp.split(gates, 2, axis=-1)

        y_a = hgrn2_mixer(q_a, f_a, i_a, g_a, lower_bounds[l], hgrn_out_gain[l])
        y_b = chunked_relpos_attention(q_b, k_b, v_b, q_gain[l], k_gain[l], rel_bias[l])

        merged = (gate_a * jnp.einsum('btc,cd->btd', y_a, w_proj_a[l])
                  + gate_b * jnp.einsum('btc,cd->btd', y_b, w_proj_b[l]))
        x = x + jnp.einsum('btd,de->bte', merged, w_out[l])

        h = rms_norm(x, norm_ffn[l])
        gate_up = jnp.einsum('btd,df->btf', h, w_ffn_in[l])
        ff_gate, ff_up = jnp.split(gate_up, 2, axis=-1)
        x = x + jnp.einsum('btf,fd->btd', jax.nn.silu(ff_gate) * ff_up, w_ffn_out[l])
    return x
```

```python
import functools

import jax
import jax.numpy as jnp
from jax import lax
from jax.experimental import pallas as pl
from jax.experimental.pallas import tpu as pltpu

F32 = jnp.float32
BF16 = jnp.bfloat16

EPS = 1e-6
LANES = 128
CHUNK = 64
N_PAST_CHUNKS = 8
REL_FUTURE = CHUNK - 1
REL_PAST = 2 * CHUNK - 1
N_REL = REL_FUTURE + REL_PAST + 1
MASK_VALUE = -1e30

IN_TM = 512
IN_TN = 1024
HG_L = 64
HG_SUB = 16
HG_G = 2
AT_Q = 4 * CHUNK
MG_TM = 256
FF_TM = 512
FF_TF = 512

NT_DIMS = (((1,), (1,)), ((), ()))


def _sigmoid(x):
    return 1.0 / (1.0 + jnp.exp(-x))


def _in_proj_kernel(x_ref, gain_ref, w_ref, bias_ref, lbl_ref, qg_ref, kg_ref,
                    out_ref, logf_ref, h_ref, *, n_heads, scale):
    j = pl.program_id(1)

    @pl.when(j == 0)
    def _norm():
        x = x_ref[...]
        ms = jnp.mean(x * x, axis=-1, keepdims=True)
        h_ref[...] = (x * lax.rsqrt(ms + EPS) * gain_ref[...]).astype(BF16)

    acc = jnp.dot(h_ref[...], w_ref[...], preferred_element_type=F32)

    @pl.when((j == 0) | (j == 3))
    def _silu():
        out_ref[...] = (acc * _sigmoid(acc)).astype(BF16)

    @pl.when(j == 1)
    def _forget():
        l = lbl_ref[...]
        e = jnp.exp(l - jnp.max(l, axis=0, keepdims=True))
        lb = e[0:1] / jnp.sum(e, axis=0, keepdims=True)
        f = lb + (1.0 - lb) * _sigmoid(acc)
        logf_ref[...] = jnp.log(f)
        out_ref[...] = (1.0 - f).astype(BF16)

    @pl.when((j == 2) | (j == 6))
    def _plain():
        out_ref[...] = acc.astype(BF16)

    def head_norm(gain_row, mult):
        for h in range(n_heads):
            sl = slice(h * LANES, (h + 1) * LANES)
            t = acc[:, sl]
            ms = jnp.mean(t * t, axis=-1, keepdims=True)
            out_ref[:, sl] = (t * lax.rsqrt(ms + EPS) * (gain_row[:, sl] * mult)).astype(BF16)

    @pl.when(j == 4)
    def _qnorm():
        head_norm(qg_ref[...], scale)

    @pl.when(j == 5)
    def _knorm():
        head_norm(kg_ref[...], 1.0)

    @pl.when(j >= 7)
    def _gates():
        out_ref[...] = _sigmoid(acc + bias_ref[...]).astype(BF16)


def _in_proj(x2, gain, w, bias_full, lb_logits, qg, kg, *, n_heads, scale):
    m, d = x2.shape
    n = w.shape[1]
    tm, tn = IN_TM, IN_TN
    grid = (m // tm, n // tn)
    return pl.pallas_call(
        functools.partial(_in_proj_kernel, n_heads=n_heads, scale=scale),
        out_shape=(jax.ShapeDtypeStruct((m, n), BF16),
                   jax.ShapeDtypeStruct((m, tn), F32)),
        grid=grid,
        in_specs=[
            pl.BlockSpec((tm, d), lambda i, j: (i, 0)),
            pl.BlockSpec((1, d), lambda i, j: (0, 0)),
            pl.BlockSpec((d, tn), lambda i, j: (0, j)),
            pl.BlockSpec((1, tn), lambda i, j: (0, j)),
            pl.BlockSpec(lb_logits.shape, lambda i, j: (0, 0)),
            pl.BlockSpec((1, tn), lambda i, j: (0, 0)),
            pl.BlockSpec((1, tn), lambda i, j: (0, 0)),
        ],
        out_specs=(pl.BlockSpec((tm, tn), lambda i, j: (i, j)),
                   pl.BlockSpec((tm, tn), lambda i, j: (i, 0))),
        scratch_shapes=[pltpu.VMEM((tm, d), BF16)],
        compiler_params=pltpu.CompilerParams(
            dimension_semantics=("arbitrary", "arbitrary"),
            vmem_limit_bytes=48 * 1024 * 1024),
        name="in_proj",
    )(x2, gain, w, bias_full, lb_logits, qg, kg)


def _hgrn_kernel(q_ref, k_ref, v_ref, g_ref, lf_ref, gain_ref, out_ref,
                 st_ref, kb_ref, bb_ref, vb_ref):
    c = pl.program_id(2)
    n_g = st_ref.shape[0]
    L, SUB = HG_L, HG_SUB

    @pl.when(c == 0)
    def _init():
        st_ref[...] = jnp.zeros_like(st_ref)
        kb_ref[...] = jnp.zeros_like(kb_ref)
        bb_ref[...] = jnp.zeros_like(bb_ref)
        vb_ref[...] = jnp.zeros_like(vb_ref)

    row = lax.broadcasted_iota(jnp.int32, (L, L), 0)
    col = lax.broadcasted_iota(jnp.int32, (L, L), 1)
    tri = (row >= col).astype(F32)
    rmod = lax.broadcasted_iota(jnp.int32, (L, 1), 0) % SUB

    for g in range(n_g):
        sl = slice(g * LANES, (g + 1) * LANES)
        lf = lf_ref[0, :, sl]
        b = jnp.dot(tri, lf, precision=lax.Precision.HIGHEST, preferred_element_type=F32)
        q = q_ref[0, :, sl].astype(F32)
        k = k_ref[0, :, sl].astype(F32)
        v_bf = v_ref[0, :, sl]
        v = v_bf.astype(F32)
        st = st_ref[g]

        qe = (q * jnp.exp(b)).astype(BF16)
        o = lax.dot_general(qe, st.astype(BF16), NT_DIMS, preferred_element_type=F32)

        o_rows = [o[0:SUB]]
        for i in range(1, L // SUB):
            r0 = i * SUB
            bref = b[r0:r0 + 1]
            qt = (q[r0:r0 + SUB] * jnp.exp(b[r0:r0 + SUB] - bref)).astype(BF16)
            kt = (k[0:r0] * jnp.exp(bref - b[0:r0])).astype(BF16)
            s = lax.dot_general(qt, kt, NT_DIMS, preferred_element_type=F32)
            oi = jnp.dot(s.astype(BF16), v_bf[0:r0], preferred_element_type=F32)
            o_rows.append(o[r0:r0 + SUB] + oi)
        o = jnp.concatenate(o_rows, axis=0)

        kb_ref[g, SUB:SUB + L, :] = k
        bb_ref[g, SUB:SUB + L, :] = b
        vb_ref[g, SUB:SUB + L, :] = v
        for d in range(SUB):
            if d == 0:
                kd, bd, vd = k, b, v
            else:
                kd = kb_ref[g, SUB - d:SUB - d + L, :]
                bd = bb_ref[g, SUB - d:SUB - d + L, :]
                vd = vb_ref[g, SUB - d:SUB - d + L, :]
            r = jnp.sum(q * kd * jnp.exp(b - bd), axis=-1, keepdims=True)
            r = jnp.where(rmod >= d, r, 0.0)
            o = o + r * vd

        b_last = b[L - 1:L]
        kdec = (k * jnp.exp(b_last - b)).astype(BF16)
        upd = jnp.dot(jnp.transpose(v).astype(BF16), kdec, preferred_element_type=F32)
        st_ref[g] = st * jnp.exp(b_last) + upd

        ms = jnp.mean(o * o, axis=-1, keepdims=True)
        y = o * lax.rsqrt(ms + EPS) * gain_ref[:, sl] * g_ref[0, :, sl].astype(F32)
        out_ref[0, :, sl] = y.astype(BF16)


def _hgrn(proj3, logf3, gain, *, d_a):
    bsz, t, _ = proj3.shape
    gw = HG_G * LANES
    nsec = d_a // gw
    grid = (bsz, nsec, t // HG_L)

    def sec(s):
        return pl.BlockSpec((1, HG_L, gw), lambda b, h, c, s=s: (b, c, s * nsec + h))

    return pl.pallas_call(
        _hgrn_kernel,
        out_shape=jax.ShapeDtypeStruct((bsz, t, d_a), BF16),
        grid=grid,
        in_specs=[sec(0), sec(1), sec(2), sec(3),
                  pl.BlockSpec((1, HG_L, gw), lambda b, h, c: (b, c, h)),
                  pl.BlockSpec((1, gw), lambda b, h, c: (0, h))],
        out_specs=pl.BlockSpec((1, HG_L, gw), lambda b, h, c: (b, c, h)),
        scratch_shapes=[pltpu.VMEM((HG_G, LANES, LANES), F32),
                        pltpu.VMEM((HG_G, HG_L + HG_SUB, LANES), F32),
                        pltpu.VMEM((HG_G, HG_L + HG_SUB, LANES), F32),
                        pltpu.VMEM((HG_G, HG_L + HG_SUB, LANES), F32)],
        compiler_params=pltpu.CompilerParams(
            dimension_semantics=("arbitrary", "arbitrary", "arbitrary")),
        name="hgrn",
    )(proj3, proj3, proj3, proj3, logf3, gain)


def _attn_kernel(q_ref, k_ref, v_ref, gvec_ref, out_ref, band_ref, bias_ref):
    t = q_ref.shape[1]
    nq = AT_Q // CHUNK
    nk = nq + N_PAST_CHUNKS
    n_tb = 3
    wb = (nk + nq - 1) * CHUNK

    grow = gvec_ref[0]
    xb = jnp.broadcast_to(grow, (CHUNK, 2 * LANES))
    r = lax.broadcasted_iota(jnp.int32, (CHUNK, 2 * LANES), 0)
    for bit in range(6):
        xb = jnp.where(((r >> bit) & 1) == 1, pltpu.roll(xb, 1 << bit, axis=1), xb)
    const = grow[:, 0:1]

    n_const = nk - nq + 1 - n_tb
    lo = (nq - 1) * CHUNK
    band_ref[:, 0:lo] = jnp.full((CHUNK, lo), MASK_VALUE, F32)
    band_ref[:, lo:lo + n_const * CHUNK] = jnp.broadcast_to(const, (CHUNK, n_const * CHUNK))
    band_ref[:, lo + n_const * CHUNK:lo + (n_const + n_tb) * CHUNK] = xb[:, CHUNK:]
    band_ref[:, wb - lo:wb] = jnp.full((CHUNK, lo), MASK_VALUE, F32)
    for qi in range(nq):
        off = (nq - 1 - qi) * CHUNK
        bias_ref[qi * CHUNK:(qi + 1) * CHUNK, :] = band_ref[:, off:off + nk * CHUNK]

    for g in range(t // AT_Q):
        q0 = g * AT_Q
        ks = max(0, q0 - N_PAST_CHUNKS * CHUNK)
        kw = q0 + AT_Q - ks
        q = q_ref[0, q0:q0 + AT_Q, :]
        k = k_ref[0, ks:ks + kw, :]
        v = v_ref[0, ks:ks + kw, :]
        s = lax.dot_general(q, k, NT_DIMS, preferred_element_type=F32)
        s = s + bias_ref[:, nk * CHUNK - kw:nk * CHUNK]
        m = jnp.max(s, axis=-1, keepdims=True)
        p = jnp.exp(s - m)
        l = jnp.sum(p, axis=-1, keepdims=True)
        o = jnp.dot(p.astype(BF16), v, preferred_element_type=F32)
        out_ref[0, q0:q0 + AT_Q, :] = (o / l).astype(BF16)


def _attn(proj3, gvec, *, col0, n_heads):
    bsz, t, _ = proj3.shape
    d_b = n_heads * LANES
    nq = AT_Q // CHUNK
    nk = nq + N_PAST_CHUNKS
    blk0 = col0 // LANES

    def sec(s):
        return pl.BlockSpec((1, t, LANES), lambda b, h, s=s: (b, 0, blk0 + s * n_heads + h))

    return pl.pallas_call(
        _attn_kernel,
        out_shape=jax.ShapeDtypeStruct((bsz, t, d_b), BF16),
        grid=(bsz, n_heads),
        in_specs=[sec(0), sec(1), sec(2),
                  pl.BlockSpec((1, 1, 2 * LANES), lambda b, h: (h, 0, 0))],
        out_specs=pl.BlockSpec((1, t, LANES), lambda b, h: (b, 0, h)),
        scratch_shapes=[pltpu.VMEM((CHUNK, (nk + nq - 1) * CHUNK), F32),
                        pltpu.VMEM((AT_Q, nk * CHUNK), F32)],
        compiler_params=pltpu.CompilerParams(
            dimension_semantics=("arbitrary", "arbitrary")),
        name="attn",
    )(proj3, proj3, proj3, gvec)


def _merge_kernel(ya_ref, yb_ref, ga0_ref, ga1_ref, gb0_ref, gb1_ref, x_ref,
                  wa_ref, wb_ref, wo_ref, gain_ref, x1_ref, h2_ref):
    pa = jnp.dot(ya_ref[...], wa_ref[...], preferred_element_type=F32)
    pb = jnp.dot(yb_ref[...], wb_ref[...], preferred_element_type=F32)
    ga = jnp.concatenate([ga0_ref[...], ga1_ref[...]], axis=1).astype(F32)
    gb = jnp.concatenate([gb0_ref[...], gb1_ref[...]], axis=1).astype(F32)
    merged = (ga * pa + gb * pb).astype(BF16)
    x1 = x_ref[...] + jnp.dot(merged, wo_ref[...], preferred_element_type=F32)
    x1_ref[...] = x1
    ms = jnp.mean(x1 * x1, axis=-1, keepdims=True)
    h2_ref[...] = (x1 * lax.rsqrt(ms + EPS) * gain_ref[...]).astype(BF16)


def _merge(ya, yb, proj, x2, wa, wb, wo, gain, *, gate_col0):
    m, d = x2.shape
    da, db = ya.shape[1], yb.shape[1]
    tm = MG_TM
    gblk = gate_col0 // IN_TN

    def gate(s):
        return pl.BlockSpec((tm, IN_TN), lambda i, s=s: (i, gblk + s))

    def whole(a):
        return pl.BlockSpec(a.shape, lambda i: (0, 0), pipeline_mode=pl.Buffered(1))

    return pl.pallas_call(
        _merge_kernel,
        out_shape=(jax.ShapeDtypeStruct((m, d), F32),
                   jax.ShapeDtypeStruct((m, d), BF16)),
        grid=(m // tm,),
        in_specs=[pl.BlockSpec((tm, da), lambda i: (i, 0)),
                  pl.BlockSpec((tm, db), lambda i: (i, 0)),
                  gate(0), gate(1), gate(2), gate(3),
                  pl.BlockSpec((tm, d), lambda i: (i, 0)),
                  whole(wa), whole(wb), whole(wo), whole(gain)],
        out_specs=(pl.BlockSpec((tm, d), lambda i: (i, 0)),
                   pl.BlockSpec((tm, d), lambda i: (i, 0))),
        compiler_params=pltpu.CompilerParams(
            dimension_semantics=("arbitrary",),
            vmem_limit_bytes=48 * 1024 * 1024),
        name="merge",
    )(ya, yb, proj, proj, proj, proj, x2, wa, wb, wo, gain)


def _ffn_kernel(h_ref, wg_ref, wu_ref, wd_ref, x1_ref, out_ref, acc_ref):
    f = pl.program_id(1)

    @pl.when(f == 0)
    def _zero():
        acc_ref[...] = jnp.zeros_like(acc_ref)

    h = h_ref[...]
    gate = jnp.dot(h, wg_ref[...], preferred_element_type=F32)
    up = jnp.dot(h, wu_ref[...], preferred_element_type=F32)
    act = (gate * _sigmoid(gate) * up).astype(BF16)
    acc_ref[...] += jnp.dot(act, wd_ref[...], preferred_element_type=F32)

    @pl.when(f == pl.num_programs(1) - 1)
    def _finish():
        out_ref[...] = x1_ref[...] + acc_ref[...]


def _ffn(h2, x1, w_in, w_out):
    m, d = h2.shape
    dff = w_out.shape[0]
    tm, tf = FF_TM, FF_TF
    nf = dff // tf
    return pl.pallas_call(
        _ffn_kernel,
        out_shape=jax.ShapeDtypeStruct((m, d), F32),
        grid=(m // tm, nf),
        in_specs=[pl.BlockSpec((tm, d), lambda i, f: (i, 0)),
                  pl.BlockSpec((d, tf), lambda i, f: (0, f)),
                  pl.BlockSpec((d, tf), lambda i, f: (0, nf + f)),
                  pl.BlockSpec((tf, d), lambda i, f: (f, 0)),
                  pl.BlockSpec((tm, d), lambda i, f: (i, 0))],
        out_specs=pl.BlockSpec((tm, d), lambda i, f: (i, 0)),
        scratch_shapes=[pltpu.VMEM((tm, d), F32)],
        compiler_params=pltpu.CompilerParams(
            dimension_semantics=("arbitrary", "arbitrary"),
            vmem_limit_bytes=48 * 1024 * 1024),
        name="ffn",
    )(h2, w_in, w_in, w_out, x1)


def _bias_vector(rel_bias):
    rev = rel_bias[:, ::-1]
    pad = 2 * LANES - N_REL
    return jnp.pad(rev, ((0, 0), (pad, 0)), mode="edge")[:, None, :]


def kernel(x, w_in, b_gate, norm_mix, norm_ffn, hgrn_lb_logits, hgrn_out_gain,
           q_gain, k_gain, rel_bias, w_proj_a, w_proj_b, w_out, w_ffn_in, w_ffn_out):
    bsz, t, d = x.shape
    depth = w_in.shape[0]
    d_a = hgrn_out_gain.shape[1]
    dh = q_gain.shape[1]
    n_heads_b = rel_bias.shape[1]
    d_b = n_heads_b * dh
    n_in = w_in.shape[2]
    gate_col0 = 4 * d_a + 3 * d_b
    assert dh == LANES and d_a == IN_TN and d_b == IN_TN and depth == 1
    assert hgrn_lb_logits.shape[0] == depth + 1
    assert n_in == gate_col0 + 2 * d and t % AT_Q == 0

    m = bsz * t
    x2 = x.reshape(m, d)
    for l in range(depth):
        bias_full = jnp.concatenate([jnp.zeros((gate_col0,), F32), b_gate[l]])[None, :]
        proj, logf = _in_proj(
            x2, norm_mix[l][None, :], w_in[l].astype(BF16), bias_full, hgrn_lb_logits,
            jnp.tile(q_gain[l], n_heads_b)[None, :], jnp.tile(k_gain[l], n_heads_b)[None, :],
            n_heads=n_heads_b, scale=dh ** -0.5)
        proj3 = proj.reshape(bsz, t, n_in)
        y_a = _hgrn(proj3, logf.reshape(bsz, t, d_a), hgrn_out_gain[l][None, :], d_a=d_a)
        y_b = _attn(proj3, _bias_vector(rel_bias[l]), col0=4 * d_a, n_heads=n_heads_b)
        x1, h2 = _merge(y_a.reshape(m, d_a), y_b.reshape(m, d_b), proj, x2,
                        w_proj_a[l].astype(BF16), w_proj_b[l].astype(BF16),
                        w_out[l].astype(BF16), norm_ffn[l][None, :], gate_col0=gate_col0)
        x2 = _ffn(h2, x1, w_ffn_in[l].astype(BF16), w_ffn_out[l].astype(BF16))
    return x2.reshape(bsz, t, d)
```

```python
import functools

import jax
import jax.numpy as jnp
from jax import lax
from jax.experimental import pallas as pl
from jax.experimental.pallas import tpu as pltpu

F32 = jnp.float32
BF16 = jnp.bfloat16

EPS = 1e-6
LANES = 128
CHUNK = 64
N_PAST_CHUNKS = 8
REL_FUTURE = CHUNK - 1
REL_PAST = 2 * CHUNK - 1
N_REL = REL_FUTURE + REL_PAST + 1
MASK_VALUE = -1e30
BIG_EXPONENT = 1e30
LOG2E = 1.4426950408889634

IN_TM = 512
IN_TN = 1024
HG_L = 64
HG_SUB = 16
HG_G = 8
AT_Q = 4 * CHUNK
MG_TM = 256
FF_TM = 512
FF_TF = 512

NT_DIMS = (((1,), (1,)), ((), ()))


def _sigmoid(x):
    return 1.0 / (1.0 + jnp.exp(-x))


def _in_proj_kernel(x_ref, gain_ref, w_ref, bias_ref, lbl_ref, qg_ref, kg_ref,
                    out_ref, logf_ref, h_ref, *, n_heads, scale):
    j = pl.program_id(1)

    @pl.when(j == 0)
    def _norm():
        x = x_ref[...]
        ms = jnp.mean(x * x, axis=-1, keepdims=True)
        h_ref[...] = (x * lax.rsqrt(ms + EPS) * gain_ref[...]).astype(BF16)

    acc = jnp.dot(h_ref[...], w_ref[...], preferred_element_type=F32)

    @pl.when((j == 0) | (j == 3))
    def _silu():
        out_ref[...] = (acc * _sigmoid(acc)).astype(BF16)

    @pl.when(j == 1)
    def _forget():
        l = lbl_ref[...]
        e = jnp.exp(l - jnp.max(l, axis=0, keepdims=True))
        lb = e[0:1] / jnp.sum(e, axis=0, keepdims=True)
        sig = _sigmoid(acc)
        logf_ref[...] = jnp.log(lb + (1.0 - lb) * sig)
        out_ref[...] = ((1.0 - lb) * (1.0 - sig)).astype(BF16)

    @pl.when((j == 2) | (j == 6))
    def _plain():
        out_ref[...] = acc.astype(BF16)

    def head_norm(gain_row, mult):
        for h in range(n_heads):
            sl = slice(h * LANES, (h + 1) * LANES)
            t = acc[:, sl]
            ms = jnp.mean(t * t, axis=-1, keepdims=True)
            out_ref[:, sl] = (t * lax.rsqrt(ms + EPS) * (gain_row[:, sl] * mult)).astype(BF16)

    @pl.when(j == 4)
    def _qnorm():
        head_norm(qg_ref[...], scale)

    @pl.when(j == 5)
    def _knorm():
        head_norm(kg_ref[...], 1.0)

    @pl.when(j >= 7)
    def _gates():
        out_ref[...] = _sigmoid(acc + bias_ref[...]).astype(BF16)


def _in_proj(x2, gain, w, bias_full, lb_logits, qg, kg, *, n_heads, scale):
    m, d = x2.shape
    n = w.shape[1]
    tm, tn = IN_TM, IN_TN
    grid = (m // tm, n // tn)
    return pl.pallas_call(
        functools.partial(_in_proj_kernel, n_heads=n_heads, scale=scale),
        out_shape=(jax.ShapeDtypeStruct((m, n), BF16),
                   jax.ShapeDtypeStruct((m, tn), F32)),
        grid=grid,
        in_specs=[
            pl.BlockSpec((tm, d), lambda i, j: (i, 0)),
            pl.BlockSpec((1, d), lambda i, j: (0, 0)),
            pl.BlockSpec((d, tn), lambda i, j: (0, j)),
            pl.BlockSpec((1, tn), lambda i, j: (0, j)),
            pl.BlockSpec(lb_logits.shape, lambda i, j: (0, 0)),
            pl.BlockSpec((1, tn), lambda i, j: (0, 0)),
            pl.BlockSpec((1, tn), lambda i, j: (0, 0)),
        ],
        out_specs=(pl.BlockSpec((tm, tn), lambda i, j: (i, j)),
                   pl.BlockSpec((tm, tn), lambda i, j: (i, 0))),
        scratch_shapes=[pltpu.VMEM((tm, d), BF16)],
        compiler_params=pltpu.CompilerParams(
            dimension_semantics=("arbitrary", "arbitrary"),
            vmem_limit_bytes=48 * 1024 * 1024),
        name="in_proj",
    )(x2, gain, w, bias_full, lb_logits, qg, kg)


def _hgrn_kernel(q_ref, k_ref, v_ref, g_ref, lf_ref, gain_ref, out_ref,
                 st_ref, cb_ref, vb_ref):
    n_g = st_ref.shape[0]
    L, SUB = HG_L, HG_SUB

    @pl.when(pl.program_id(2) == 0)
    def _init():
        st_ref[...] = jnp.zeros_like(st_ref)
        cb_ref[...] = jnp.full(cb_ref.shape, BIG_EXPONENT, F32)
        vb_ref[...] = jnp.zeros_like(vb_ref)

    row = lax.broadcasted_iota(jnp.int32, (L, L), 0)
    col = lax.broadcasted_iota(jnp.int32, (L, L), 1)
    tri = (row >= col).astype(BF16)
    ones = jnp.ones((LANES, LANES), BF16)

    def cumsum_rows(a):
        hi = a.astype(BF16)
        r1 = a - hi.astype(F32)
        mid = r1.astype(BF16)
        lo = (r1 - mid.astype(F32)).astype(BF16)
        return (jnp.dot(tri, hi, preferred_element_type=F32)
                + jnp.dot(tri, mid, preferred_element_type=F32)
                + jnp.dot(tri, lo, preferred_element_type=F32))

    heads = [slice(g * LANES, (g + 1) * LANES) for g in range(n_g)]

    b = cumsum_rows(lf_ref[0] * LOG2E)
    q = q_ref[0].astype(F32)
    k = k_ref[0].astype(F32)
    v_bf = v_ref[0]
    v = v_bf.astype(F32)
    c = b - jnp.log2(k)
    for g, sl in enumerate(heads):
        cb_ref[g, SUB:SUB + L, :] = c[:, sl]
        vb_ref[g, SUB:SUB + L, :] = v[:, sl]

    qe = (q * jnp.exp2(b)).astype(BF16)
    o_blocks = [[None] * (L // SUB) for _ in heads]
    st_old = []
    for g, sl in enumerate(heads):
        st = st_ref[g]
        st_old.append(st)
        o = lax.dot_general(qe[:, sl], st.astype(BF16), NT_DIMS, preferred_element_type=F32)
        for i in range(L // SUB):
            o_blocks[g][i] = o[i * SUB:(i + 1) * SUB]

    s_far = {}
    for i in range(1, L // SUB):
        r0 = i * SUB
        bref = b[r0:r0 + 1]
        qt = (q[r0:r0 + SUB] * jnp.exp2(b[r0:r0 + SUB] - bref)).astype(BF16)
        kt = (k[0:r0] * jnp.exp2(bref - b[0:r0])).astype(BF16)
        for g, sl in enumerate(heads):
            s_far[i, g] = lax.dot_general(qt[:, sl], kt[:, sl], NT_DIMS,
                                          preferred_element_type=F32)

    b_last = b[L - 1:L]
    kdec = (k * jnp.exp2(b_last - b)).astype(BF16)
    eb_last = jnp.exp2(b_last)
    for g, sl in enumerate(heads):
        upd = jnp.dot(jnp.transpose(v[:, sl]).astype(BF16), kdec[:, sl],
                      preferred_element_type=F32)
        st_ref[g] = st_old[g] * eb_last[:, sl] + upd

    for i in range(1, L // SUB):
        r0 = i * SUB
        far = (lax.broadcasted_iota(jnp.int32, (SUB, r0), 1)
               <= lax.broadcasted_iota(jnp.int32, (SUB, r0), 0) + (r0 - SUB))
        for g, sl in enumerate(heads):
            s = jnp.where(far, s_far[i, g], 0.0).astype(BF16)
            oi = jnp.dot(s, v_bf[0:r0, sl], preferred_element_type=F32)
            o_blocks[g][i] = o_blocks[g][i] + oi

    n_acc = 4
    for g, sl in enumerate(heads):
        qg, bg = q[:, sl], b[:, sl]
        e = [(qg * jnp.exp2(bg - cb_ref[g, SUB - d:SUB - d + L, :])).astype(BF16)
             for d in range(SUB)]
        r = jnp.dot(jnp.concatenate(e, axis=0), ones, preferred_element_type=F32)
        acc = [jnp.concatenate(o_blocks[g], axis=0)] + [None] * (n_acc - 1)
        for d in range(SUB):
            term = r[d * L:(d + 1) * L] * vb_ref[g, SUB - d:SUB - d + L, :]
            a = d % n_acc
            acc[a] = term if acc[a] is None else acc[a] + term
        o = (acc[0] + acc[1]) + (acc[2] + acc[3])
        ms = jnp.mean(o * o, axis=-1, keepdims=True)
        y = o * lax.rsqrt(ms + EPS) * gain_ref[:, sl] * g_ref[0, :, sl].astype(F32)
        out_ref[0, :, sl] = y.astype(BF16)


def _hgrn(proj3, logf3, gain, *, d_a):
    bsz, t, _ = proj3.shape
    gw = HG_G * LANES
    nsec = d_a // gw
    grid = (bsz, nsec, t // HG_L)

    def sec(s):
        return pl.BlockSpec((1, HG_L, gw), lambda b, h, c, s=s: (b, c, s * nsec + h))

    return pl.pallas_call(
        _hgrn_kernel,
        out_shape=jax.ShapeDtypeStruct((bsz, t, d_a), BF16),
        grid=grid,
        in_specs=[sec(0), sec(1), sec(2), sec(3),
                  pl.BlockSpec((1, HG_L, gw), lambda b, h, c: (b, c, h)),
                  pl.BlockSpec((1, gw), lambda b, h, c: (0, h))],
        out_specs=pl.BlockSpec((1, HG_L, gw), lambda b, h, c: (b, c, h)),
        scratch_shapes=[pltpu.VMEM((HG_G, LANES, LANES), F32),
                        pltpu.VMEM((HG_G, HG_L + HG_SUB, LANES), F32),
                        pltpu.VMEM((HG_G, HG_L + HG_SUB, LANES), F32)],
        compiler_params=pltpu.CompilerParams(
            dimension_semantics=("arbitrary", "arbitrary", "arbitrary")),
        name="hgrn",
    )(proj3, proj3, proj3, proj3, logf3, gain)


def _attn_kernel(q_ref, k_ref, v_ref, gvec_ref, out_ref, band_ref, bias_ref):
    t = q_ref.shape[1]
    nq = AT_Q // CHUNK
    nk = nq + N_PAST_CHUNKS
    n_tb = 3
    wb = (nk + nq - 1) * CHUNK

    grow = gvec_ref[0]
    xb = jnp.broadcast_to(grow, (CHUNK, 2 * LANES))
    r = lax.broadcasted_iota(jnp.int32, (CHUNK, 2 * LANES), 0)
    for bit in range(6):
        xb = jnp.where(((r >> bit) & 1) == 1, pltpu.roll(xb, 1 << bit, axis=1), xb)
    const = grow[:, 0:1]

    n_const = nk - nq + 1 - n_tb
    lo = (nq - 1) * CHUNK
    band_ref[:, 0:lo] = jnp.full((CHUNK, lo), MASK_VALUE, F32)
    band_ref[:, lo:lo + n_const * CHUNK] = jnp.broadcast_to(const, (CHUNK, n_const * CHUNK))
    band_ref[:, lo + n_const * CHUNK:lo + (n_const + n_tb) * CHUNK] = xb[:, CHUNK:]
    band_ref[:, wb - lo:wb] = jnp.full((CHUNK, lo), MASK_VALUE, F32)
    for qi in range(nq):
        off = (nq - 1 - qi) * CHUNK
        bias_ref[qi * CHUNK:(qi + 1) * CHUNK, :] = band_ref[:, off:off + nk * CHUNK]

    for g in range(t // AT_Q):
        q0 = g * AT_Q
        ks = max(0, q0 - N_PAST_CHUNKS * CHUNK)
        kw = q0 + AT_Q - ks
        q = q_ref[0, q0:q0 + AT_Q, :]
        k = k_ref[0, ks:ks + kw, :]
        v = v_ref[0, ks:ks + kw, :]
        s = lax.dot_general(q, k, NT_DIMS, preferred_element_type=F32)
        s = s + bias_ref[:, nk * CHUNK - kw:nk * CHUNK]
        m = jnp.max(s, axis=-1, keepdims=True)
        p = jnp.exp(s - m)
        l = jnp.sum(p, axis=-1, keepdims=True)
        o = jnp.dot(p.astype(BF16), v, preferred_element_type=F32)
        out_ref[0, q0:q0 + AT_Q, :] = (o / l).astype(BF16)


def _attn(proj3, gvec, *, col0, n_heads):
    bsz, t, _ = proj3.shape
    d_b = n_heads * LANES
    nq = AT_Q // CHUNK
    nk = nq + N_PAST_CHUNKS
    blk0 = col0 // LANES

    def sec(s):
        return pl.BlockSpec((1, t, LANES), lambda b, h, s=s: (b, 0, blk0 + s * n_heads + h))

    return pl.pallas_call(
        _attn_kernel,
        out_shape=jax.ShapeDtypeStruct((bsz, t, d_b), BF16),
        grid=(bsz, n_heads),
        in_specs=[sec(0), sec(1), sec(2),
                  pl.BlockSpec((1, 1, 2 * LANES), lambda b, h: (h, 0, 0))],
        out_specs=pl.BlockSpec((1, t, LANES), lambda b, h: (b, 0, h)),
        scratch_shapes=[pltpu.VMEM((CHUNK, (nk + nq - 1) * CHUNK), F32),
                        pltpu.VMEM((AT_Q, nk * CHUNK), F32)],
        compiler_params=pltpu.CompilerParams(
            dimension_semantics=("arbitrary", "arbitrary")),
        name="attn",
    )(proj3, proj3, proj3, gvec)


def _merge_kernel(ya_ref, yb_ref, ga0_ref, ga1_ref, gb0_ref, gb1_ref, x_ref,
                  wa_ref, wb_ref, wo_ref, gain_ref, x1_ref, h2_ref):
    pa = jnp.dot(ya_ref[...], wa_ref[...], preferred_element_type=F32)
    pb = jnp.dot(yb_ref[...], wb_ref[...], preferred_element_type=F32)
    ga = jnp.concatenate([ga0_ref[...], ga1_ref[...]], axis=1).astype(F32)
    gb = jnp.concatenate([gb0_ref[...], gb1_ref[...]], axis=1).astype(F32)
    merged = (ga * pa + gb * pb).astype(BF16)
    x1 = x_ref[...] + jnp.dot(merged, wo_ref[...], preferred_element_type=F32)
    x1_ref[...] = x1
    ms = jnp.mean(x1 * x1, axis=-1, keepdims=True)
    h2_ref[...] = (x1 * lax.rsqrt(ms + EPS) * gain_ref[...]).astype(BF16)


def _merge(ya, yb, proj, x2, wa, wb, wo, gain, *, gate_col0):
    m, d = x2.shape
    da, db = ya.shape[1], yb.shape[1]
    tm = MG_TM
    gblk = gate_col0 // IN_TN

    def gate(s):
        return pl.BlockSpec((tm, IN_TN), lambda i, s=s: (i, gblk + s))

    def whole(a):
        return pl.BlockSpec(a.shape, lambda i: (0, 0), pipeline_mode=pl.Buffered(1))

    return pl.pallas_call(
        _merge_kernel,
        out_shape=(jax.ShapeDtypeStruct((m, d), F32),
                   jax.ShapeDtypeStruct((m, d), BF16)),
        grid=(m // tm,),
        in_specs=[pl.BlockSpec((tm, da), lambda i: (i, 0)),
                  pl.BlockSpec((tm, db), lambda i: (i, 0)),
                  gate(0), gate(1), gate(2), gate(3),
                  pl.BlockSpec((tm, d), lambda i: (i, 0)),
                  whole(wa), whole(wb), whole(wo), whole(gain)],
        out_specs=(pl.BlockSpec((tm, d), lambda i: (i, 0)),
                   pl.BlockSpec((tm, d), lambda i: (i, 0))),
        compiler_params=pltpu.CompilerParams(
            dimension_semantics=("arbitrary",),
            vmem_limit_bytes=48 * 1024 * 1024),
        name="merge",
    )(ya, yb, proj, proj, proj, proj, x2, wa, wb, wo, gain)


def _ffn_kernel(h_ref, wg_ref, wu_ref, wd_ref, x1_ref, out_ref, acc_ref):
    f = pl.program_id(1)

    @pl.when(f == 0)
    def _zero():
        acc_ref[...] = jnp.zeros_like(acc_ref)

    h = h_ref[...]
    gate = jnp.dot(h, wg_ref[...], preferred_element_type=F32)
    up = jnp.dot(h, wu_ref[...], preferred_element_type=F32)
    act = (gate * _sigmoid(gate) * up).astype(BF16)
    acc_ref[...] += jnp.dot(act, wd_ref[...], preferred_element_type=F32)

    @pl.when(f == pl.num_programs(1) - 1)
    def _finish():
        out_ref[...] = x1_ref[...] + acc_ref[...]


def _ffn(h2, x1, w_in, w_out):
    m, d = h2.shape
    dff = w_out.shape[0]
    tm, tf = FF_TM, FF_TF
    nf = dff // tf
    return pl.pallas_call(
        _ffn_kernel,
        out_shape=jax.ShapeDtypeStruct((m, d), F32),
        grid=(m // tm, nf),
        in_specs=[pl.BlockSpec((tm, d), lambda i, f: (i, 0)),
                  pl.BlockSpec((d, tf), lambda i, f: (0, f)),
                  pl.BlockSpec((d, tf), lambda i, f: (0, nf + f)),
                  pl.BlockSpec((tf, d), lambda i, f: (f, 0)),
                  pl.BlockSpec((tm, d), lambda i, f: (i, 0))],
        out_specs=pl.BlockSpec((tm, d), lambda i, f: (i, 0)),
        scratch_shapes=[pltpu.VMEM((tm, d), F32)],
        compiler_params=pltpu.CompilerParams(
            dimension_semantics=("arbitrary", "arbitrary"),
            vmem_limit_bytes=48 * 1024 * 1024),
        name="ffn",
    )(h2, w_in, w_in, w_out, x1)


def _bias_vector(rel_bias):
    rev = rel_bias[:, ::-1]
    pad = 2 * LANES - N_REL
    return jnp.pad(rev, ((0, 0), (pad, 0)), mode="edge")[:, None, :]


def kernel(x, w_in, b_gate, norm_mix, norm_ffn, hgrn_lb_logits, hgrn_out_gain,
           q_gain, k_gain, rel_bias, w_proj_a, w_proj_b, w_out, w_ffn_in, w_ffn_out):
    bsz, t, d = x.shape
    depth = w_in.shape[0]
    d_a = hgrn_out_gain.shape[1]
    dh = q_gain.shape[1]
    n_heads_b = rel_bias.shape[1]
    d_b = n_heads_b * dh
    n_in = w_in.shape[2]
    gate_col0 = 4 * d_a + 3 * d_b
    assert dh == LANES and d_a == IN_TN and d_b == IN_TN and depth == 1
    assert hgrn_lb_logits.shape[0] == depth + 1
    assert n_in == gate_col0 + 2 * d and t % AT_Q == 0

    m = bsz * t
    x2 = x.reshape(m, d)
    for l in range(depth):
        bias_full = jnp.concatenate([jnp.zeros((gate_col0,), F32), b_gate[l]])[None, :]
        proj, logf = _in_proj(
            x2, norm_mix[l][None, :], w_in[l].astype(BF16), bias_full, hgrn_lb_logits,
            jnp.tile(q_gain[l], n_heads_b)[None, :], jnp.tile(k_gain[l], n_heads_b)[None, :],
            n_heads=n_heads_b, scale=dh ** -0.5)
        proj3 = proj.reshape(bsz, t, n_in)
        y_a = _hgrn(proj3, logf.reshape(bsz, t, d_a), hgrn_out_gain[l][None, :], d_a=d_a)
        y_b = _attn(proj3, _bias_vector(rel_bias[l]), col0=4 * d_a, n_heads=n_heads_b)
        x1, h2 = _merge(y_a.reshape(m, d_a), y_b.reshape(m, d_b), proj, x2,
                        w_proj_a[l].astype(BF16), w_proj_b[l].astype(BF16),
                        w_out[l].astype(BF16), norm_ffn[l][None, :], gate_col0=gate_col0)
        x2 = _ffn(h2, x1, w_ffn_in[l].astype(BF16), w_ffn_out[l].astype(BF16))
    return x2.reshape(bsz, t, d)
```

```python
import functools

import jax
import jax.numpy as jnp
from jax import lax
from jax.experimental import pallas as pl
from jax.experimental.pallas import tpu as pltpu

F32 = jnp.float32
BF16 = jnp.bfloat16

EPS = 1e-6
LANES = 128
CHUNK = 64
N_PAST_CHUNKS = 8
REL_FUTURE = CHUNK - 1
REL_PAST = 2 * CHUNK - 1
N_REL = REL_FUTURE + REL_PAST + 1
MASK_VALUE = -1e30
BIG_EXPONENT = 1e30
LOG2E = 1.4426950408889634

IN_TM = 512
IN_TN = 1024
IN_SUB = 256
HG_L = 64
HG_SUB = 16
HG_G = 8
AT_Q = 4 * CHUNK
MG_TM = 256
FF_TM = 512
FF_TF = 512

NT_DIMS = (((1,), (1,)), ((), ()))


def _sigmoid(x):
    return 0.5 * jnp.tanh(0.5 * x) + 0.5


def _in_proj_kernel(x_ref, gain_ref, w_ref, bias_ref, lbl_ref, qg_ref, kg_ref,
                    out_ref, logf_ref, h_ref, *, scale):
    j = pl.program_id(1)

    @pl.when(j == 0)
    def _norm():
        x = x_ref[...]
        ms = jnp.mean(x * x, axis=-1, keepdims=True)
        h_ref[...] = (x * lax.rsqrt(ms + EPS) * gain_ref[...]).astype(BF16)

    def section(epilogue):
        for c in range(w_ref.shape[1] // IN_SUB):
            cs = slice(c * IN_SUB, (c + 1) * IN_SUB)
            epilogue(jnp.dot(h_ref[...], w_ref[:, cs], preferred_element_type=F32), cs)

    @pl.when((j == 0) | (j == 3))
    def _silu():
        def epilogue(acc, cs):
            out_ref[:, cs] = (acc * _sigmoid(acc)).astype(BF16)
        section(epilogue)

    @pl.when(j == 1)
    def _forget():
        l = lbl_ref[...]
        e = jnp.exp(l - jnp.max(l, axis=0, keepdims=True))
        lb_row = e[0:1] / jnp.sum(e, axis=0, keepdims=True)

        def epilogue(acc, cs):
            lb = lb_row[:, cs]
            sig = _sigmoid(acc)
            logf_ref[:, cs] = jnp.log(lb + (1.0 - lb) * sig)
            out_ref[:, cs] = ((1.0 - lb) * (1.0 - sig)).astype(BF16)
        section(epilogue)

    @pl.when((j == 2) | (j == 6))
    def _plain():
        def epilogue(acc, cs):
            out_ref[:, cs] = acc.astype(BF16)
        section(epilogue)

    def head_norm(gain_ref_, mult):
        def epilogue(acc, cs):
            for h in range(IN_SUB // LANES):
                t = acc[:, h * LANES:(h + 1) * LANES]
                sl = slice(cs.start + h * LANES, cs.start + (h + 1) * LANES)
                ms = jnp.mean(t * t, axis=-1, keepdims=True)
                out_ref[:, sl] = (t * lax.rsqrt(ms + EPS) * (gain_ref_[:, sl] * mult)).astype(BF16)
        section(epilogue)

    @pl.when(j == 4)
    def _qnorm():
        head_norm(qg_ref, scale)

    @pl.when(j == 5)
    def _knorm():
        head_norm(kg_ref, 1.0)

    @pl.when(j >= 7)
    def _gates():
        def epilogue(acc, cs):
            out_ref[:, cs] = _sigmoid(acc + bias_ref[:, cs]).astype(BF16)
        section(epilogue)


def _in_proj(x2, gain, w, bias_full, lb_logits, qg, kg, *, scale):
    m, d = x2.shape
    n = w.shape[1]
    tm, tn = IN_TM, IN_TN
    grid = (m // tm, n // tn)
    return pl.pallas_call(
        functools.partial(_in_proj_kernel, scale=scale),
        out_shape=(jax.ShapeDtypeStruct((m, n), BF16),
                   jax.ShapeDtypeStruct((m, tn), F32)),
        grid=grid,
        in_specs=[
            pl.BlockSpec((tm, d), lambda i, j: (i, 0)),
            pl.BlockSpec((1, d), lambda i, j: (0, 0)),
            pl.BlockSpec((d, tn), lambda i, j: (0, j)),
            pl.BlockSpec((1, tn), lambda i, j: (0, j)),
            pl.BlockSpec(lb_logits.shape, lambda i, j: (0, 0)),
            pl.BlockSpec((1, tn), lambda i, j: (0, 0)),
            pl.BlockSpec((1, tn), lambda i, j: (0, 0)),
        ],
        out_specs=(pl.BlockSpec((tm, tn), lambda i, j: (i, j)),
                   pl.BlockSpec((tm, tn), lambda i, j: (i, 0))),
        scratch_shapes=[pltpu.VMEM((tm, d), BF16)],
        compiler_params=pltpu.CompilerParams(
            dimension_semantics=("arbitrary", "arbitrary"),
            vmem_limit_bytes=48 * 1024 * 1024),
        name="in_proj",
    )(x2, gain, w, bias_full, lb_logits, qg, kg)


def _hgrn_kernel(q_ref, k_ref, v_ref, g_ref, lf_ref, gain_ref, out_ref,
                 st_ref, cb_ref, vb_ref):
    n_g = st_ref.shape[0]
    L, SUB = HG_L, HG_SUB

    @pl.when(pl.program_id(2) == 0)
    def _init():
        st_ref[...] = jnp.zeros_like(st_ref)
        cb_ref[...] = jnp.full(cb_ref.shape, BIG_EXPONENT, F32)
        vb_ref[...] = jnp.zeros_like(vb_ref)

    row = lax.broadcasted_iota(jnp.int32, (L, L), 0)
    col = lax.broadcasted_iota(jnp.int32, (L, L), 1)
    tri = (row >= col).astype(BF16)
    ones = jnp.ones((LANES, LANES), BF16)

    def cumsum_rows(a):
        hi = a.astype(BF16)
        r1 = a - hi.astype(F32)
        mid = r1.astype(BF16)
        lo = (r1 - mid.astype(F32)).astype(BF16)
        return (jnp.dot(tri, hi, preferred_element_type=F32)
                + jnp.dot(tri, mid, preferred_element_type=F32)
                + jnp.dot(tri, lo, preferred_element_type=F32))

    heads = [slice(g * LANES, (g + 1) * LANES) for g in range(n_g)]

    b = cumsum_rows(lf_ref[0] * LOG2E)
    q = q_ref[0].astype(F32)
    k = k_ref[0].astype(F32)
    v_bf = v_ref[0]
    v = v_bf.astype(F32)
    c = b - jnp.log2(k)
    for g, sl in enumerate(heads):
        cb_ref[g, SUB:SUB + L, :] = c[:, sl]
        vb_ref[g, SUB:SUB + L, :] = v[:, sl]

    qe = (q * jnp.exp2(b)).astype(BF16)
    o_blocks = [[None] * (L // SUB) for _ in heads]
    st_old = []
    for g, sl in enumerate(heads):
        st = st_ref[g]
        st_old.append(st)
        o = lax.dot_general(qe[:, sl], st.astype(BF16), NT_DIMS, preferred_element_type=F32)
        for i in range(L // SUB):
            o_blocks[g][i] = o[i * SUB:(i + 1) * SUB]

    s_far = {}
    for i in range(1, L // SUB):
        r0 = i * SUB
        bref = b[r0:r0 + 1]
        qt = (q[r0:r0 + SUB] * jnp.exp2(b[r0:r0 + SUB] - bref)).astype(BF16)
        kt = (k[0:r0] * jnp.exp2(bref - b[0:r0])).astype(BF16)
        for g, sl in enumerate(heads):
            s_far[i, g] = lax.dot_general(qt[:, sl], kt[:, sl], NT_DIMS,
                                          preferred_element_type=F32)

    b_last = b[L - 1:L]
    kdec = (k * jnp.exp2(b_last - b)).astype(BF16)
    eb_last = jnp.exp2(b_last)
    for g, sl in enumerate(heads):
        upd = jnp.dot(jnp.transpose(v[:, sl]).astype(BF16), kdec[:, sl],
                      preferred_element_type=F32)
        st_ref[g] = st_old[g] * eb_last[:, sl] + upd

    for i in range(1, L // SUB):
        r0 = i * SUB
        far = (lax.broadcasted_iota(jnp.int32, (SUB, r0), 1)
               <= lax.broadcasted_iota(jnp.int32, (SUB, r0), 0) + (r0 - SUB))
        for g, sl in enumerate(heads):
            s = jnp.where(far, s_far[i, g], 0.0).astype(BF16)
            oi = jnp.dot(s, v_bf[0:r0, sl], preferred_element_type=F32)
            o_blocks[g][i] = o_blocks[g][i] + oi

    n_acc = 4
    for g, sl in enumerate(heads):
        qg, bg = q[:, sl], b[:, sl]
        e = [(qg * jnp.exp2(bg - cb_ref[g, SUB - d:SUB - d + L, :])).astype(BF16)
             for d in range(SUB)]
        r = jnp.dot(jnp.concatenate(e, axis=0), ones, preferred_element_type=F32)
        acc = [jnp.concatenate(o_blocks[g], axis=0)] + [None] * (n_acc - 1)
        for d in range(SUB):
            term = r[d * L:(d + 1) * L] * vb_ref[g, SUB - d:SUB - d + L, :]
            a = d % n_acc
            acc[a] = term if acc[a] is None else acc[a] + term
        o = (acc[0] + acc[1]) + (acc[2] + acc[3])
        ms = jnp.mean(o * o, axis=-1, keepdims=True)
        y = o * lax.rsqrt(ms + EPS) * gain_ref[:, sl] * g_ref[0, :, sl].astype(F32)
        out_ref[0, :, sl] = y.astype(BF16)


def _hgrn(proj3, logf3, gain, *, d_a):
    bsz, t, _ = proj3.shape
    gw = HG_G * LANES
    nsec = d_a // gw
    grid = (bsz, nsec, t // HG_L)

    def sec(s):
        return pl.BlockSpec((1, HG_L, gw), lambda b, h, c, s=s: (b, c, s * nsec + h))

    return pl.pallas_call(
        _hgrn_kernel,
        out_shape=jax.ShapeDtypeStruct((bsz, t, d_a), BF16),
        grid=grid,
        in_specs=[sec(0), sec(1), sec(2), sec(3),
                  pl.BlockSpec((1, HG_L, gw), lambda b, h, c: (b, c, h)),
                  pl.BlockSpec((1, gw), lambda b, h, c: (0, h))],
        out_specs=pl.BlockSpec((1, HG_L, gw), lambda b, h, c: (b, c, h)),
        scratch_shapes=[pltpu.VMEM((HG_G, LANES, LANES), F32),
                        pltpu.VMEM((HG_G, HG_L + HG_SUB, LANES), F32),
                        pltpu.VMEM((HG_G, HG_L + HG_SUB, LANES), F32)],
        compiler_params=pltpu.CompilerParams(
            dimension_semantics=("arbitrary", "arbitrary", "arbitrary")),
        name="hgrn",
    )(proj3, proj3, proj3, proj3, logf3, gain)


def _attn_kernel(q_ref, k_ref, v_ref, gvec_ref, out_ref, band_ref, bias_ref):
    t = q_ref.shape[1]
    nq = AT_Q // CHUNK
    nk = nq + N_PAST_CHUNKS
    n_tb = 3
    wb = (nk + nq - 1) * CHUNK

    @pl.when(pl.program_id(1) == 0)
    def _build_bias():
        grow = gvec_ref[0]
        xb = jnp.broadcast_to(grow, (CHUNK, 2 * LANES))
        r = lax.broadcasted_iota(jnp.int32, (CHUNK, 2 * LANES), 0)
        for bit in range(6):
            xb = jnp.where(((r >> bit) & 1) == 1, pltpu.roll(xb, 1 << bit, axis=1), xb)
        const = grow[:, 0:1]

        n_const = nk - nq + 1 - n_tb
        lo = (nq - 1) * CHUNK
        band_ref[:, 0:lo] = jnp.full((CHUNK, lo), MASK_VALUE, F32)
        band_ref[:, lo:lo + n_const * CHUNK] = jnp.broadcast_to(const, (CHUNK, n_const * CHUNK))
        band_ref[:, lo + n_const * CHUNK:lo + (n_const + n_tb) * CHUNK] = xb[:, CHUNK:]
        band_ref[:, wb - lo:wb] = jnp.full((CHUNK, lo), MASK_VALUE, F32)
        for qi in range(nq):
            off = (nq - 1 - qi) * CHUNK
            bias_ref[qi * CHUNK:(qi + 1) * CHUNK, :] = band_ref[:, off:off + nk * CHUNK]

    for g in range(t // AT_Q):
        q0 = g * AT_Q
        ks = max(0, q0 - N_PAST_CHUNKS * CHUNK)
        kw = q0 + AT_Q - ks
        q = q_ref[0, q0:q0 + AT_Q, :]
        k = k_ref[0, ks:ks + kw, :]
        v = v_ref[0, ks:ks + kw, :]
        s = lax.dot_general(q, k, NT_DIMS, preferred_element_type=F32)
        s = s + bias_ref[:, nk * CHUNK - kw:nk * CHUNK]
        m = jnp.max(s, axis=-1, keepdims=True)
        p = jnp.exp(s - m)
        l = jnp.sum(p, axis=-1, keepdims=True)
        o = jnp.dot(p.astype(BF16), v, preferred_element_type=F32)
        out_ref[0, q0:q0 + AT_Q, :] = (o / l).astype(BF16)


def _attn(proj3, gvec, *, col0, n_heads):
    bsz, t, _ = proj3.shape
    d_b = n_heads * LANES
    nq = AT_Q // CHUNK
    nk = nq + N_PAST_CHUNKS
    blk0 = col0 // LANES

    def sec(s):
        return pl.BlockSpec((1, t, LANES), lambda h, b, s=s: (b, 0, blk0 + s * n_heads + h))

    return pl.pallas_call(
        _attn_kernel,
        out_shape=jax.ShapeDtypeStruct((bsz, t, d_b), BF16),
        grid=(n_heads, bsz),
        in_specs=[sec(0), sec(1), sec(2),
                  pl.BlockSpec((1, 1, 2 * LANES), lambda h, b: (h, 0, 0))],
        out_specs=pl.BlockSpec((1, t, LANES), lambda h, b: (b, 0, h)),
        scratch_shapes=[pltpu.VMEM((CHUNK, (nk + nq - 1) * CHUNK), F32),
                        pltpu.VMEM((AT_Q, nk * CHUNK), F32)],
        compiler_params=pltpu.CompilerParams(
            dimension_semantics=("arbitrary", "arbitrary")),
        name="attn",
    )(proj3, proj3, proj3, gvec)


def _merge_kernel(ya_ref, yb_ref, ga0_ref, ga1_ref, gb0_ref, gb1_ref, x_ref,
                  wa_ref, wb_ref, wo_ref, gain_ref, x1_ref, h2_ref):
    pa = jnp.dot(ya_ref[...], wa_ref[...], preferred_element_type=F32)
    pb = jnp.dot(yb_ref[...], wb_ref[...], preferred_element_type=F32)
    ga = jnp.concatenate([ga0_ref[...], ga1_ref[...]], axis=1).astype(F32)
    gb = jnp.concatenate([gb0_ref[...], gb1_ref[...]], axis=1).astype(F32)
    merged = (ga * pa + gb * pb).astype(BF16)
    x1 = x_ref[...] + jnp.dot(merged, wo_ref[...], preferred_element_type=F32)
    x1_ref[...] = x1
    ms = jnp.mean(x1 * x1, axis=-1, keepdims=True)
    h2_ref[...] = (x1 * lax.rsqrt(ms + EPS) * gain_ref[...]).astype(BF16)


def _merge(ya, yb, proj, x2, wa, wb, wo, gain, *, gate_col0):
    m, d = x2.shape
    da, db = ya.shape[1], yb.shape[1]
    tm = MG_TM
    gblk = gate_col0 // IN_TN

    def gate(s):
        return pl.BlockSpec((tm, IN_TN), lambda i, s=s: (i, gblk + s))

    def whole(a):
        return pl.BlockSpec(a.shape, lambda i: (0, 0), pipeline_mode=pl.Buffered(1))

    return pl.pallas_call(
        _merge_kernel,
        out_shape=(jax.ShapeDtypeStruct((m, d), F32),
                   jax.ShapeDtypeStruct((m, d), BF16)),
        grid=(m // tm,),
        in_specs=[pl.BlockSpec((tm, da), lambda i: (i, 0)),
                  pl.BlockSpec((tm, db), lambda i: (i, 0)),
                  gate(0), gate(1), gate(2), gate(3),
                  pl.BlockSpec((tm, d), lambda i: (i, 0)),
                  whole(wa), whole(wb), whole(wo), whole(gain)],
        out_specs=(pl.BlockSpec((tm, d), lambda i: (i, 0)),
                   pl.BlockSpec((tm, d), lambda i: (i, 0))),
        compiler_params=pltpu.CompilerParams(
            dimension_semantics=("arbitrary",),
            vmem_limit_bytes=48 * 1024 * 1024),
        name="merge",
    )(ya, yb, proj, proj, proj, proj, x2, wa, wb, wo, gain)


def _ffn_kernel(h_ref, wg_ref, wu_ref, wd_ref, x1_ref, out_ref):
    @pl.when(pl.program_id(1) == 0)
    def _residual():
        out_ref[...] = x1_ref[...]

    h = h_ref[...]
    gate = jnp.dot(h, wg_ref[...], preferred_element_type=F32)
    up = jnp.dot(h, wu_ref[...], preferred_element_type=F32)
    act = (gate * _sigmoid(gate) * up).astype(BF16)
    out_ref[...] += jnp.dot(act, wd_ref[...], preferred_element_type=F32)


def _ffn(h2, x1, w_in, w_out):
    m, d = h2.shape
    dff = w_out.shape[0]
    tm, tf = FF_TM, FF_TF
    nf = dff // tf
    return pl.pallas_call(
        _ffn_kernel,
        out_shape=jax.ShapeDtypeStruct((m, d), F32),
        grid=(m // tm, nf),
        in_specs=[pl.BlockSpec((tm, d), lambda i, f: (i, 0)),
                  pl.BlockSpec((d, tf), lambda i, f: (0, f)),
                  pl.BlockSpec((d, tf), lambda i, f: (0, nf + f)),
                  pl.BlockSpec((tf, d), lambda i, f: (f, 0)),
                  pl.BlockSpec((tm, d), lambda i, f: (i, 0))],
        out_specs=pl.BlockSpec((tm, d), lambda i, f: (i, 0)),
        compiler_params=pltpu.CompilerParams(
            dimension_semantics=("arbitrary", "arbitrary"),
            vmem_limit_bytes=48 * 1024 * 1024),
        name="ffn",
    )(h2, w_in, w_in, w_out, x1)


def _bias_vector(rel_bias):
    rev = rel_bias[:, ::-1]
    pad = 2 * LANES - N_REL
    return jnp.pad(rev, ((0, 0), (pad, 0)), mode="edge")[:, None, :]


def kernel(x, w_in, b_gate, norm_mix, norm_ffn, hgrn_lb_logits, hgrn_out_gain,
           q_gain, k_gain, rel_bias, w_proj_a, w_proj_b, w_out, w_ffn_in, w_ffn_out):
    bsz, t, d = x.shape
    depth = w_in.shape[0]
    d_a = hgrn_out_gain.shape[1]
    dh = q_gain.shape[1]
    n_heads_b = rel_bias.shape[1]
    d_b = n_heads_b * dh
    n_in = w_in.shape[2]
    gate_col0 = 4 * d_a + 3 * d_b
    assert dh == LANES and d_a == IN_TN and d_b == IN_TN and depth == 1
    assert hgrn_lb_logits.shape[0] == depth + 1
    assert n_in == gate_col0 + 2 * d and t % AT_Q == 0

    m = bsz * t
    x2 = x.reshape(m, d)
    for l in range(depth):
        bias_full = jnp.concatenate([jnp.zeros((gate_col0,), F32), b_gate[l]])[None, :]
        proj, logf = _in_proj(
            x2, norm_mix[l][None, :], w_in[l].astype(BF16), bias_full, hgrn_lb_logits,
            jnp.tile(q_gain[l], n_heads_b)[None, :], jnp.tile(k_gain[l], n_heads_b)[None, :],
            scale=dh ** -0.5)
        proj3 = proj.reshape(bsz, t, n_in)
        y_a = _hgrn(proj3, logf.reshape(bsz, t, d_a), hgrn_out_gain[l][None, :], d_a=d_a)
        y_b = _attn(proj3, _bias_vector(rel_bias[l]), col0=4 * d_a, n_heads=n_heads_b)
        x1, h2 = _merge(y_a.reshape(m, d_a), y_b.reshape(m, d_b), proj, x2,
                        w_proj_a[l].astype(BF16), w_proj_b[l].astype(BF16),
                        w_out[l].astype(BF16), norm_ffn[l][None, :], gate_col0=gate_col0)
        x2 = _ffn(h2, x1, w_ffn_in[l].astype(BF16), w_ffn_out[l].astype(BF16))
    return x2.reshape(bsz, t, d)
```

```python
import functools

import jax
import jax.numpy as jnp
from jax import lax
from jax.experimental import pallas as pl
from jax.experimental.pallas import tpu as pltpu

F32 = jnp.float32
BF16 = jnp.bfloat16

EPS = 1e-6
LANES = 128
BF16_SUBLANES = 16
VMEM_LIMIT_BYTES = 56 * 1024 * 1024
CHUNK = 64
N_PAST_CHUNKS = 8
REL_FUTURE = CHUNK - 1
REL_PAST = 2 * CHUNK - 1
N_REL = REL_FUTURE + REL_PAST + 1
MASK_VALUE = -1e30
BIG_EXPONENT = 1e30
LOG2E = 1.4426950408889634

IN_TM = 1024
IN_TN = 1024
IN_SUB = 256
HG_L = 64
HG_SUB = 16
HG_G = 8
AT_Q = 4 * CHUNK
MG_TM = 256
FF_TM = 512
FF_TF = 512

NT_DIMS = (((1,), (1,)), ((), ()))


def _sigmoid(x):
    return 0.5 * jnp.tanh(0.5 * x) + 0.5


def _cast_spec(shape, grid):
    rows, cols = shape
    steps = 1
    for g in grid:
        steps *= g
    if rows % (steps * BF16_SUBLANES) == 0:
        strides = [steps // g0 for g0 in _running_products(grid)]

        def index_map(*ids):
            return (sum(i * s for i, s in zip(ids, strides)), 0)
        return pl.BlockSpec((rows // steps, cols), index_map)
    g0, g1 = grid
    assert rows % (g0 * BF16_SUBLANES) == 0 and cols % (g1 * LANES) == 0, (shape, grid)
    return pl.BlockSpec((rows // g0, cols // g1), lambda i, j: (i, j))


def _running_products(grid):
    out, p = [], 1
    for g in grid:
        p *= g
        out.append(p)
    return out


def _cast_blocks(src_refs, dst_refs):
    for src, dst in zip(src_refs, dst_refs):
        dst[...] = src[...].astype(BF16)


def _in_proj_kernel(x_ref, gain_ref, w_ref, bias_ref, lbl_ref, qg_ref, kg_ref, *rest,
                    scale, n_cast):
    cast_in = rest[:n_cast]
    out_ref, logf_ref = rest[n_cast:n_cast + 2]
    cast_out = rest[n_cast + 2:2 * n_cast + 2]
    h_ref = rest[-1]
    j = pl.program_id(1)

    @pl.when(j == 0)
    def _norm():
        x = x_ref[...]
        ms = jnp.mean(x * x, axis=-1, keepdims=True)
        h_ref[...] = (x * lax.rsqrt(ms + EPS) * gain_ref[...]).astype(BF16)

    def section(epilogue):
        _cast_blocks(cast_in, cast_out)
        for c in range(w_ref.shape[1] // IN_SUB):
            cs = slice(c * IN_SUB, (c + 1) * IN_SUB)
            epilogue(jnp.dot(h_ref[...], w_ref[:, cs], preferred_element_type=F32), cs)

    @pl.when((j == 0) | (j == 3))
    def _silu():
        def epilogue(acc, cs):
            out_ref[:, cs] = (acc * _sigmoid(acc)).astype(BF16)
        section(epilogue)

    @pl.when(j == 1)
    def _forget():
        l = lbl_ref[...]
        e = jnp.exp(l - jnp.max(l, axis=0, keepdims=True))
        lb_row = e[0:1] / jnp.sum(e, axis=0, keepdims=True)

        def epilogue(acc, cs):
            lb = lb_row[:, cs]
            sig = _sigmoid(acc)
            logf_ref[:, cs] = jnp.log(lb + (1.0 - lb) * sig)
            out_ref[:, cs] = ((1.0 - lb) * (1.0 - sig)).astype(BF16)
        section(epilogue)

    @pl.when((j == 2) | (j == 6))
    def _plain():
        def epilogue(acc, cs):
            out_ref[:, cs] = acc.astype(BF16)
        section(epilogue)

    def head_norm(gain_ref_, mult):
        def epilogue(acc, cs):
            for h in range(IN_SUB // LANES):
                t = acc[:, h * LANES:(h + 1) * LANES]
                sl = slice(cs.start + h * LANES, cs.start + (h + 1) * LANES)
                ms = jnp.mean(t * t, axis=-1, keepdims=True)
                out_ref[:, sl] = (t * lax.rsqrt(ms + EPS) * (gain_ref_[:, sl] * mult)).astype(BF16)
        section(epilogue)

    @pl.when(j == 4)
    def _qnorm():
        head_norm(qg_ref, scale)

    @pl.when(j == 5)
    def _knorm():
        head_norm(kg_ref, 1.0)

    @pl.when(j >= 7)
    def _gates():
        def epilogue(acc, cs):
            out_ref[:, cs] = _sigmoid(acc + bias_ref[:, cs]).astype(BF16)
        section(epilogue)


def _in_proj(x2, gain, w, bias_full, lb_logits, qg, kg, cast_srcs, *, scale):
    m, d = x2.shape
    n = w.shape[1]
    tm, tn = IN_TM, IN_TN
    grid = (m // tm, n // tn)
    cast_specs = [_cast_spec(a.shape, grid) for a in cast_srcs]
    return pl.pallas_call(
        functools.partial(_in_proj_kernel, scale=scale, n_cast=len(cast_srcs)),
        out_shape=(jax.ShapeDtypeStruct((m, n), BF16),
                   jax.ShapeDtypeStruct((m, tn), F32),
                   *[jax.ShapeDtypeStruct(a.shape, BF16) for a in cast_srcs]),
        grid=grid,
        in_specs=[
            pl.BlockSpec((tm, d), lambda i, j: (i, 0)),
            pl.BlockSpec((1, d), lambda i, j: (0, 0)),
            pl.BlockSpec((d, tn), lambda i, j: (0, j)),
            pl.BlockSpec((1, tn), lambda i, j: (0, j)),
            pl.BlockSpec(lb_logits.shape, lambda i, j: (0, 0)),
            pl.BlockSpec((1, tn), lambda i, j: (0, 0)),
            pl.BlockSpec((1, tn), lambda i, j: (0, 0)),
            *cast_specs,
        ],
        out_specs=(pl.BlockSpec((tm, tn), lambda i, j: (i, j)),
                   pl.BlockSpec((tm, tn), lambda i, j: (i, 0)),
                   *cast_specs),
        scratch_shapes=[pltpu.VMEM((tm, d), BF16)],
        compiler_params=pltpu.CompilerParams(
            dimension_semantics=("arbitrary", "arbitrary"),
            vmem_limit_bytes=VMEM_LIMIT_BYTES),
        name="in_proj",
    )(x2, gain, w, bias_full, lb_logits, qg, kg, *cast_srcs)


def _hgrn_kernel(q_ref, k_ref, v_ref, g_ref, lf_ref, gain_ref, out_ref,
                 st_ref, cb_ref, vb_ref):
    n_g = st_ref.shape[0]
    L, SUB = HG_L, HG_SUB

    @pl.when(pl.program_id(2) == 0)
    def _init():
        st_ref[...] = jnp.zeros_like(st_ref)
        cb_ref[...] = jnp.full(cb_ref.shape, BIG_EXPONENT, F32)
        vb_ref[...] = jnp.zeros_like(vb_ref)

    row = lax.broadcasted_iota(jnp.int32, (L, L), 0)
    col = lax.broadcasted_iota(jnp.int32, (L, L), 1)
    tri = (row >= col).astype(BF16)
    ones = jnp.ones((LANES, LANES), BF16)

    def cumsum_rows(a):
        hi = a.astype(BF16)
        r1 = a - hi.astype(F32)
        mid = r1.astype(BF16)
        lo = (r1 - mid.astype(F32)).astype(BF16)
        return (jnp.dot(tri, hi, preferred_element_type=F32)
                + jnp.dot(tri, mid, preferred_element_type=F32)
                + jnp.dot(tri, lo, preferred_element_type=F32))

    heads = [slice(g * LANES, (g + 1) * LANES) for g in range(n_g)]

    b = cumsum_rows(lf_ref[0] * LOG2E)
    q = q_ref[0].astype(F32)
    k = k_ref[0].astype(F32)
    v_bf = v_ref[0]
    v = v_bf.astype(F32)
    c = b - jnp.log2(k)
    for g, sl in enumerate(heads):
        cb_ref[g, SUB:SUB + L, :] = c[:, sl]
        vb_ref[g, SUB:SUB + L, :] = v[:, sl]

    qe = (q * jnp.exp2(b)).astype(BF16)
    o_blocks = [[None] * (L // SUB) for _ in heads]
    st_old = []
    for g, sl in enumerate(heads):
        st = st_ref[g]
        st_old.append(st)
        o = lax.dot_general(qe[:, sl], st.astype(BF16), NT_DIMS, preferred_element_type=F32)
        for i in range(L // SUB):
            o_blocks[g][i] = o[i * SUB:(i + 1) * SUB]

    s_far = {}
    for i in range(1, L // SUB):
        r0 = i * SUB
        bref = b[r0:r0 + 1]
        qt = (q[r0:r0 + SUB] * jnp.exp2(b[r0:r0 + SUB] - bref)).astype(BF16)
        kt = (k[0:r0] * jnp.exp2(bref - b[0:r0])).astype(BF16)
        for g, sl in enumerate(heads):
            s_far[i, g] = lax.dot_general(qt[:, sl], kt[:, sl], NT_DIMS,
                                          preferred_element_type=F32)

    b_last = b[L - 1:L]
    kdec = (k * jnp.exp2(b_last - b)).astype(BF16)
    eb_last = jnp.exp2(b_last)
    for g, sl in enumerate(heads):
        upd = jnp.dot(jnp.transpose(v[:, sl]).astype(BF16), kdec[:, sl],
                      preferred_element_type=F32)
        st_ref[g] = st_old[g] * eb_last[:, sl] + upd

    for i in range(1, L // SUB):
        r0 = i * SUB
        far = (lax.broadcasted_iota(jnp.int32, (SUB, r0), 1)
               <= lax.broadcasted_iota(jnp.int32, (SUB, r0), 0) + (r0 - SUB))
        for g, sl in enumerate(heads):
            s = jnp.where(far, s_far[i, g], 0.0).astype(BF16)
            oi = jnp.dot(s, v_bf[0:r0, sl], preferred_element_type=F32)
            o_blocks[g][i] = o_blocks[g][i] + oi

    n_acc = 4
    for g, sl in enumerate(heads):
        qg, bg = q[:, sl], b[:, sl]
        e = [(qg * jnp.exp2(bg - cb_ref[g, SUB - d:SUB - d + L, :])).astype(BF16)
             for d in range(SUB)]
        r = jnp.dot(jnp.concatenate(e, axis=0), ones, preferred_element_type=F32)
        acc = [jnp.concatenate(o_blocks[g], axis=0)] + [None] * (n_acc - 1)
        for d in range(SUB):
            term = r[d * L:(d + 1) * L] * vb_ref[g, SUB - d:SUB - d + L, :]
            a = d % n_acc
            acc[a] = term if acc[a] is None else acc[a] + term
        o = (acc[0] + acc[1]) + (acc[2] + acc[3])
        ms = jnp.mean(o * o, axis=-1, keepdims=True)
        y = o * lax.rsqrt(ms + EPS) * gain_ref[:, sl] * g_ref[0, :, sl].astype(F32)
        out_ref[0, :, sl] = y.astype(BF16)


def _hgrn(proj3, logf3, gain, *, d_a):
    bsz, t, _ = proj3.shape
    gw = HG_G * LANES
    nsec = d_a // gw
    grid = (bsz, nsec, t // HG_L)

    def sec(s):
        return pl.BlockSpec((1, HG_L, gw), lambda b, h, c, s=s: (b, c, s * nsec + h))

    return pl.pallas_call(
        _hgrn_kernel,
        out_shape=jax.ShapeDtypeStruct((bsz, t, d_a), BF16),
        grid=grid,
        in_specs=[sec(0), sec(1), sec(2), sec(3),
                  pl.BlockSpec((1, HG_L, gw), lambda b, h, c: (b, c, h)),
                  pl.BlockSpec((1, gw), lambda b, h, c: (0, h))],
        out_specs=pl.BlockSpec((1, HG_L, gw), lambda b, h, c: (b, c, h)),
        scratch_shapes=[pltpu.VMEM((HG_G, LANES, LANES), F32),
                        pltpu.VMEM((HG_G, HG_L + HG_SUB, LANES), F32),
                        pltpu.VMEM((HG_G, HG_L + HG_SUB, LANES), F32)],
        compiler_params=pltpu.CompilerParams(
            dimension_semantics=("arbitrary", "arbitrary", "arbitrary")),
        name="hgrn",
    )(proj3, proj3, proj3, proj3, logf3, gain)


def _attn_kernel(q_ref, k_ref, v_ref, gvec_ref, *rest, n_cast):
    cast_in = rest[:n_cast]
    out_ref = rest[n_cast]
    cast_out = rest[n_cast + 1:2 * n_cast + 1]
    band_ref, bias_ref = rest[-2:]
    _cast_blocks(cast_in, cast_out)
    t = q_ref.shape[1]
    nq = AT_Q // CHUNK
    nk = nq + N_PAST_CHUNKS
    n_tb = 3
    wb = (nk + nq - 1) * CHUNK

    @pl.when(pl.program_id(1) == 0)
    def _build_bias():
        grow = gvec_ref[0]
        xb = jnp.broadcast_to(grow, (CHUNK, 2 * LANES))
        r = lax.broadcasted_iota(jnp.int32, (CHUNK, 2 * LANES), 0)
        for bit in range(6):
            xb = jnp.where(((r >> bit) & 1) == 1, pltpu.roll(xb, 1 << bit, axis=1), xb)
        const = grow[:, 0:1]

        n_const = nk - nq + 1 - n_tb
        lo = (nq - 1) * CHUNK
        band_ref[:, 0:lo] = jnp.full((CHUNK, lo), MASK_VALUE, F32)
        band_ref[:, lo:lo + n_const * CHUNK] = jnp.broadcast_to(const, (CHUNK, n_const * CHUNK))
        band_ref[:, lo + n_const * CHUNK:lo + (n_const + n_tb) * CHUNK] = xb[:, CHUNK:]
        band_ref[:, wb - lo:wb] = jnp.full((CHUNK, lo), MASK_VALUE, F32)
        for qi in range(nq):
            off = (nq - 1 - qi) * CHUNK
            bias_ref[qi * CHUNK:(qi + 1) * CHUNK, :] = band_ref[:, off:off + nk * CHUNK]

    for g in range(t // AT_Q):
        q0 = g * AT_Q
        ks = max(0, q0 - N_PAST_CHUNKS * CHUNK)
        kw = q0 + AT_Q - ks
        q = q_ref[0, q0:q0 + AT_Q, :]
        k = k_ref[0, ks:ks + kw, :]
        v = v_ref[0, ks:ks + kw, :]
        s = lax.dot_general(q, k, NT_DIMS, preferred_element_type=F32)
        s = s + bias_ref[:, nk * CHUNK - kw:nk * CHUNK]
        m = jnp.max(s, axis=-1, keepdims=True)
        p = jnp.exp(s - m)
        l = jnp.sum(p, axis=-1, keepdims=True)
        o = jnp.dot(p.astype(BF16), v, preferred_element_type=F32)
        out_ref[0, q0:q0 + AT_Q, :] = (o / l).astype(BF16)


def _attn(proj3, gvec, cast_srcs, *, col0, n_heads):
    bsz, t, _ = proj3.shape
    d_b = n_heads * LANES
    nq = AT_Q // CHUNK
    nk = nq + N_PAST_CHUNKS
    blk0 = col0 // LANES
    grid = (n_heads, bsz)
    cast_specs = [_cast_spec(a.shape, grid) for a in cast_srcs]

    def sec(s):
        return pl.BlockSpec((1, t, LANES), lambda h, b, s=s: (b, 0, blk0 + s * n_heads + h))

    return pl.pallas_call(
        functools.partial(_attn_kernel, n_cast=len(cast_srcs)),
        out_shape=(jax.ShapeDtypeStruct((bsz, t, d_b), BF16),
                   *[jax.ShapeDtypeStruct(a.shape, BF16) for a in cast_srcs]),
        grid=grid,
        in_specs=[sec(0), sec(1), sec(2),
                  pl.BlockSpec((1, 1, 2 * LANES), lambda h, b: (h, 0, 0)),
                  *cast_specs],
        out_specs=(pl.BlockSpec((1, t, LANES), lambda h, b: (b, 0, h)), *cast_specs),
        scratch_shapes=[pltpu.VMEM((CHUNK, (nk + nq - 1) * CHUNK), F32),
                        pltpu.VMEM((AT_Q, nk * CHUNK), F32)],
        compiler_params=pltpu.CompilerParams(
            dimension_semantics=("arbitrary", "arbitrary")),
        name="attn",
    )(proj3, proj3, proj3, gvec, *cast_srcs)


def _merge_kernel(ya_ref, yb_ref, ga0_ref, ga1_ref, gb0_ref, gb1_ref, x_ref,
                  wa_ref, wb_ref, wo_ref, gain_ref, x1_ref, h2_ref):
    pa = jnp.dot(ya_ref[...], wa_ref[...], preferred_element_type=F32)
    pb = jnp.dot(yb_ref[...], wb_ref[...], preferred_element_type=F32)
    ga = jnp.concatenate([ga0_ref[...], ga1_ref[...]], axis=1).astype(F32)
    gb = jnp.concatenate([gb0_ref[...], gb1_ref[...]], axis=1).astype(F32)
    merged = (ga * pa + gb * pb).astype(BF16)
    x1 = x_ref[...] + jnp.dot(merged, wo_ref[...], preferred_element_type=F32)
    x1_ref[...] = x1
    ms = jnp.mean(x1 * x1, axis=-1, keepdims=True)
    h2_ref[...] = (x1 * lax.rsqrt(ms + EPS) * gain_ref[...]).astype(BF16)


def _merge(ya, yb, proj, x2, wa, wb, wo, gain, *, gate_col0):
    m, d = x2.shape
    da, db = ya.shape[1], yb.shape[1]
    tm = MG_TM
    gblk = gate_col0 // IN_TN

    def gate(s):
        return pl.BlockSpec((tm, IN_TN), lambda i, s=s: (i, gblk + s))

    def whole(a):
        return pl.BlockSpec(a.shape, lambda i: (0, 0), pipeline_mode=pl.Buffered(1))

    return pl.pallas_call(
        _merge_kernel,
        out_shape=(jax.ShapeDtypeStruct((m, d), F32),
                   jax.ShapeDtypeStruct((m, d), BF16)),
        grid=(m // tm,),
        in_specs=[pl.BlockSpec((tm, da), lambda i: (i, 0)),
                  pl.BlockSpec((tm, db), lambda i: (i, 0)),
                  gate(0), gate(1), gate(2), gate(3),
                  pl.BlockSpec((tm, d), lambda i: (i, 0)),
                  whole(wa), whole(wb), whole(wo), whole(gain)],
        out_specs=(pl.BlockSpec((tm, d), lambda i: (i, 0)),
                   pl.BlockSpec((tm, d), lambda i: (i, 0))),
        compiler_params=pltpu.CompilerParams(
            dimension_semantics=("arbitrary",),
            vmem_limit_bytes=VMEM_LIMIT_BYTES),
        name="merge",
    )(ya, yb, proj, proj, proj, proj, x2, wa, wb, wo, gain)


def _ffn_kernel(h_ref, wg_ref, wu_ref, wd_ref, x1_ref, out_ref):
    @pl.when(pl.program_id(1) == 0)
    def _residual():
        out_ref[...] = x1_ref[...]

    h = h_ref[...]
    gate = jnp.dot(h, wg_ref[...], preferred_element_type=F32)
    up = jnp.dot(h, wu_ref[...], preferred_element_type=F32)
    act = (gate * _sigmoid(gate) * up).astype(BF16)
    out_ref[...] += jnp.dot(act, wd_ref[...], preferred_element_type=F32)


def _ffn(h2, x1, w_in, w_out):
    m, d = h2.shape
    dff = w_out.shape[0]
    tm, tf = FF_TM, FF_TF
    nf = dff // tf
    return pl.pallas_call(
        _ffn_kernel,
        out_shape=jax.ShapeDtypeStruct((m, d), F32),
        grid=(m // tm, nf),
        in_specs=[pl.BlockSpec((tm, d), lambda i, f: (i, 0)),
                  pl.BlockSpec((d, tf), lambda i, f: (0, f)),
                  pl.BlockSpec((d, tf), lambda i, f: (0, nf + f)),
                  pl.BlockSpec((tf, d), lambda i, f: (f, 0)),
                  pl.BlockSpec((tm, d), lambda i, f: (i, 0))],
        out_specs=pl.BlockSpec((tm, d), lambda i, f: (i, 0)),
        compiler_params=pltpu.CompilerParams(
            dimension_semantics=("arbitrary", "arbitrary"),
            vmem_limit_bytes=VMEM_LIMIT_BYTES),
        name="ffn",
    )(h2, w_in, w_in, w_out, x1)


def _bias_vector(rel_bias):
    rev = rel_bias[:, ::-1]
    pad = 2 * LANES - N_REL
    return jnp.pad(rev, ((0, 0), (pad, 0)), mode="edge")[:, None, :]


def kernel(x, w_in, b_gate, norm_mix, norm_ffn, hgrn_lb_logits, hgrn_out_gain,
           q_gain, k_gain, rel_bias, w_proj_a, w_proj_b, w_out, w_ffn_in, w_ffn_out):
    bsz, t, d = x.shape
    depth = w_in.shape[0]
    d_a = hgrn_out_gain.shape[1]
    dh = q_gain.shape[1]
    n_heads_b = rel_bias.shape[1]
    d_b = n_heads_b * dh
    n_in = w_in.shape[2]
    gate_col0 = 4 * d_a + 3 * d_b
    assert dh == LANES and d_a == IN_TN and d_b == IN_TN and depth == 1
    assert hgrn_lb_logits.shape[0] == depth + 1
    assert n_in == gate_col0 + 2 * d and t % AT_Q == 0

    m = bsz * t
    x2 = x.reshape(m, d)
    for l in range(depth):
        bias_full = jnp.concatenate([jnp.zeros((gate_col0,), F32), b_gate[l]])[None, :]
        proj, logf, w_ffn_in_bf, w_ffn_out_bf = _in_proj(
            x2, norm_mix[l][None, :], w_in[l].astype(BF16), bias_full, hgrn_lb_logits,
            jnp.tile(q_gain[l], n_heads_b)[None, :], jnp.tile(k_gain[l], n_heads_b)[None, :],
            (w_ffn_in[l], w_ffn_out[l]), scale=dh ** -0.5)
        proj3 = proj.reshape(bsz, t, n_in)
        y_a = _hgrn(proj3, logf.reshape(bsz, t, d_a), hgrn_out_gain[l][None, :], d_a=d_a)
        y_b, wa_bf, wb_bf, wo_bf = _attn(
            proj3, _bias_vector(rel_bias[l]), (w_proj_a[l], w_proj_b[l], w_out[l]),
            col0=4 * d_a, n_heads=n_heads_b)
        x1, h2 = _merge(y_a.reshape(m, d_a), y_b.reshape(m, d_b), proj, x2,
                        wa_bf, wb_bf, wo_bf, norm_ffn[l][None, :], gate_col0=gate_col0)
        x2 = _ffn(h2, x1, w_ffn_in_bf, w_ffn_out_bf)
    return x2.reshape(bsz, t, d)
```

```python
import functools

import jax
import jax.numpy as jnp
from jax import lax
from jax.experimental import pallas as pl
from jax.experimental.pallas import tpu as pltpu

F32 = jnp.float32
BF16 = jnp.bfloat16

EPS = 1e-6
LANES = 128
BF16_SUBLANES = 16
VMEM_LIMIT_BYTES = 56 * 1024 * 1024
CHUNK = 64
N_PAST_CHUNKS = 8
REL_FUTURE = CHUNK - 1
REL_PAST = 2 * CHUNK - 1
N_REL = REL_FUTURE + REL_PAST + 1
MASK_VALUE = -1e30
BIG_EXPONENT = 1e30
LOGF_PIECES = 3

IN_TM = 1024
IN_TN = 1024
IN_SUB = 256
HG_L = 128
HG_SUB = 16
AT_Q = 4 * CHUNK
MG_TM = 256
FF_TM = 512
FF_TF = 512

NT_DIMS = (((1,), (1,)), ((), ()))


def _sigmoid(x):
    return 0.5 * jnp.tanh(0.5 * x) + 0.5


def _cast_spec(shape, grid):
    rows, cols = shape
    steps = 1
    for g in grid:
        steps *= g
    if rows % (steps * BF16_SUBLANES) == 0:
        strides = [steps // g0 for g0 in _running_products(grid)]

        def index_map(*ids):
            return (sum(i * s for i, s in zip(ids, strides)), 0)
        return pl.BlockSpec((rows // steps, cols), index_map)
    g0, g1 = grid
    assert rows % (g0 * BF16_SUBLANES) == 0 and cols % (g1 * LANES) == 0, (shape, grid)
    return pl.BlockSpec((rows // g0, cols // g1), lambda i, j: (i, j))


def _running_products(grid):
    out, p = [], 1
    for g in grid:
        p *= g
        out.append(p)
    return out


def _cast_blocks(src_refs, dst_refs):
    for src, dst in zip(src_refs, dst_refs):
        dst[...] = src[...].astype(BF16)


def _in_proj_kernel(x_ref, gain_ref, w_ref, bias_ref, lbl_ref, qg_ref, kg_ref, *rest,
                    scale, n_cast):
    cast_in = rest[:n_cast]
    out_ref, logf_ref = rest[n_cast:n_cast + 2]
    cast_out = rest[n_cast + 2:2 * n_cast + 2]
    h_ref = rest[-1]
    j = pl.program_id(1)

    @pl.when(j == 0)
    def _norm():
        x = x_ref[...]
        ms = jnp.mean(x * x, axis=-1, keepdims=True)
        h_ref[...] = (x * lax.rsqrt(ms + EPS) * gain_ref[...]).astype(BF16)

    def section(epilogue):
        _cast_blocks(cast_in, cast_out)
        for c in range(w_ref.shape[1] // IN_SUB):
            cs = slice(c * IN_SUB, (c + 1) * IN_SUB)
            epilogue(jnp.dot(h_ref[...], w_ref[:, cs], preferred_element_type=F32), cs)

    @pl.when((j == 0) | (j == 3))
    def _silu():
        def epilogue(acc, cs):
            out_ref[:, cs] = (acc * _sigmoid(acc)).astype(BF16)
        section(epilogue)

    @pl.when(j == 1)
    def _forget():
        l = lbl_ref[...]
        e = jnp.exp(l - jnp.max(l, axis=0, keepdims=True))
        lb_row = e[0:1] / jnp.sum(e, axis=0, keepdims=True)

        def epilogue(acc, cs):
            lb = lb_row[:, cs]
            sig = _sigmoid(acc)
            lf = jnp.log2(lb + (1.0 - lb) * sig)
            hi = lf.astype(BF16)
            r1 = lf - hi.astype(F32)
            mid = r1.astype(BF16)
            tn = w_ref.shape[1]
            logf_ref[:, cs] = hi
            logf_ref[:, slice(tn + cs.start, tn + cs.stop)] = mid
            logf_ref[:, slice(2 * tn + cs.start, 2 * tn + cs.stop)] = (
                r1 - mid.astype(F32)).astype(BF16)
            out_ref[:, cs] = ((1.0 - lb) * (1.0 - sig)).astype(BF16)
        section(epilogue)

    @pl.when((j == 2) | (j == 6))
    def _plain():
        def epilogue(acc, cs):
            out_ref[:, cs] = acc.astype(BF16)
        section(epilogue)

    def head_norm(gain_ref_, mult):
        def epilogue(acc, cs):
            for h in range(IN_SUB // LANES):
                t = acc[:, h * LANES:(h + 1) * LANES]
                sl = slice(cs.start + h * LANES, cs.start + (h + 1) * LANES)
                ms = jnp.mean(t * t, axis=-1, keepdims=True)
                out_ref[:, sl] = (t * lax.rsqrt(ms + EPS) * (gain_ref_[:, sl] * mult)).astype(BF16)
        section(epilogue)

    @pl.when(j == 4)
    def _qnorm():
        head_norm(qg_ref, scale)

    @pl.when(j == 5)
    def _knorm():
        head_norm(kg_ref, 1.0)

    @pl.when(j >= 7)
    def _gates():
        def epilogue(acc, cs):
            out_ref[:, cs] = _sigmoid(acc + bias_ref[:, cs]).astype(BF16)
        section(epilogue)


def _in_proj(x2, gain, w, bias_full, lb_logits, qg, kg, cast_srcs, *, scale):
    m, d = x2.shape
    n = w.shape[1]
    tm, tn = IN_TM, IN_TN
    grid = (m // tm, n // tn)
    cast_specs = [_cast_spec(a.shape, grid) for a in cast_srcs]
    return pl.pallas_call(
        functools.partial(_in_proj_kernel, scale=scale, n_cast=len(cast_srcs)),
        out_shape=(jax.ShapeDtypeStruct((m, n), BF16),
                   jax.ShapeDtypeStruct((m, LOGF_PIECES * tn), BF16),
                   *[jax.ShapeDtypeStruct(a.shape, BF16) for a in cast_srcs]),
        grid=grid,
        in_specs=[
            pl.BlockSpec((tm, d), lambda i, j: (i, 0)),
            pl.BlockSpec((1, d), lambda i, j: (0, 0)),
            pl.BlockSpec((d, tn), lambda i, j: (0, j)),
            pl.BlockSpec((1, tn), lambda i, j: (0, j)),
            pl.BlockSpec(lb_logits.shape, lambda i, j: (0, 0)),
            pl.BlockSpec((1, tn), lambda i, j: (0, 0)),
            pl.BlockSpec((1, tn), lambda i, j: (0, 0)),
            *cast_specs,
        ],
        out_specs=(pl.BlockSpec((tm, tn), lambda i, j: (i, j)),
                   pl.BlockSpec((tm, LOGF_PIECES * tn), lambda i, j: (i, 0)),
                   *cast_specs),
        scratch_shapes=[pltpu.VMEM((tm, d), BF16)],
        compiler_params=pltpu.CompilerParams(
            dimension_semantics=("arbitrary", "arbitrary"),
            vmem_limit_bytes=VMEM_LIMIT_BYTES),
        name="in_proj",
    )(x2, gain, w, bias_full, lb_logits, qg, kg, *cast_srcs)


def _hgrn_kernel(q_ref, k_ref, v_ref, g_ref, lf_ref, gain_ref, out_ref,
                 st_ref, cb_ref, sel_ref):
    n_g = st_ref.shape[0]
    L, SUB = HG_L, HG_SUB

    @pl.when(pl.program_id(1) == 0)
    def _init():
        st_ref[...] = jnp.zeros_like(st_ref)
        cb_ref[...] = jnp.full(cb_ref.shape, BIG_EXPONENT, F32)
        r = lax.broadcasted_iota(jnp.int32, sel_ref.shape, 0)
        c = lax.broadcasted_iota(jnp.int32, sel_ref.shape, 1)
        sel_ref[...] = jnp.where(c == SUB - 1 - r // LANES, 1.0, 0.0).astype(BF16)

    row = lax.broadcasted_iota(jnp.int32, (L, L), 0)
    col = lax.broadcasted_iota(jnp.int32, (L, L), 1)
    tri = (row >= col).astype(BF16)


    heads = [slice(g * LANES, (g + 1) * LANES) for g in range(n_g)]

    gw = n_g * LANES
    b = sum(jnp.dot(tri, lf_ref[0, :, p * gw:(p + 1) * gw], preferred_element_type=F32)
            for p in range(LOGF_PIECES))
    q = q_ref[0].astype(F32)
    k = k_ref[0].astype(F32)
    v_bf = v_ref[0]
    v = v_bf.astype(F32)
    c = b - jnp.log2(k)
    for g, sl in enumerate(heads):
        cb_ref[g, SUB:SUB + L, :] = c[:, sl]

    qe = (q * jnp.exp2(b)).astype(BF16)
    o_blocks = [[None] * (L // SUB) for _ in heads]
    st_old = []
    for g, sl in enumerate(heads):
        st = st_ref[g]
        st_old.append(st)
        o = lax.dot_general(qe[:, sl], st.astype(BF16), NT_DIMS, preferred_element_type=F32)
        for i in range(L // SUB):
            o_blocks[g][i] = o[i * SUB:(i + 1) * SUB]

    s_far = {}
    for i in range(1, L // SUB):
        r0 = i * SUB
        bref = b[r0:r0 + 1]
        qt = (q[r0:r0 + SUB] * jnp.exp2(b[r0:r0 + SUB] - bref)).astype(BF16)
        kt = (k[0:r0] * jnp.exp2(bref - b[0:r0])).astype(BF16)
        for g, sl in enumerate(heads):
            s_far[i, g] = lax.dot_general(qt[:, sl], kt[:, sl], NT_DIMS,
                                          preferred_element_type=F32)

    b_last = b[L - 1:L]
    kdec = (k * jnp.exp2(b_last - b)).astype(BF16)
    eb_last = jnp.exp2(b_last)
    for g, sl in enumerate(heads):
        upd = jnp.dot(jnp.transpose(v[:, sl]).astype(BF16), kdec[:, sl],
                      preferred_element_type=F32)
        st_ref[g] = st_old[g] * eb_last[:, sl] + upd

    for i in range(1, L // SUB):
        r0 = i * SUB
        far = (lax.broadcasted_iota(jnp.int32, (SUB, r0), 1)
               <= lax.broadcasted_iota(jnp.int32, (SUB, r0), 0) + (r0 - SUB))
        for g, sl in enumerate(heads):
            s = jnp.where(far, s_far[i, g], 0.0).astype(BF16)
            oi = jnp.dot(s, v_bf[0:r0, sl], preferred_element_type=F32)
            o_blocks[g][i] = o_blocks[g][i] + oi

    near = []
    for g, sl in enumerate(heads):
        qg, bg = q[:, sl], b[:, sl]
        e = [(qg * jnp.exp2(bg - cb_ref[g, SUB - d:SUB - d + L, :])).astype(BF16)
             for d in range(SUB)]
        near.append(jnp.dot(jnp.concatenate(e, axis=1), sel_ref[...],
                            preferred_element_type=F32))

    for g, sl in enumerate(heads):
        s = pltpu.roll(near[g], LANES - (SUB - 1), axis=1, stride=1, stride_axis=0)
        o = jnp.concatenate(o_blocks[g], axis=0) + jnp.dot(
            s[:, 0:L].astype(BF16), v_bf[:, sl], preferred_element_type=F32)
        ms = jnp.mean(o * o, axis=-1, keepdims=True)
        y = o * lax.rsqrt(ms + EPS) * gain_ref[:, sl] * g_ref[0, :, sl].astype(F32)
        out_ref[0, :, sl] = y.astype(BF16)


def _hgrn(proj3, logf3, gain, *, d_a):
    bsz, t, _ = proj3.shape
    n_g = d_a // LANES

    def sec(s):
        return pl.BlockSpec((1, HG_L, d_a), lambda b, c, s=s: (b, c, s))

    return pl.pallas_call(
        _hgrn_kernel,
        out_shape=jax.ShapeDtypeStruct((bsz, t, d_a), BF16),
        grid=(bsz, t // HG_L),
        in_specs=[sec(0), sec(1), sec(2), sec(3),
                  pl.BlockSpec((1, HG_L, LOGF_PIECES * d_a), lambda b, c: (b, c, 0)),
                  pl.BlockSpec((1, d_a), lambda b, c: (0, 0))],
        out_specs=pl.BlockSpec((1, HG_L, d_a), lambda b, c: (b, c, 0)),
        scratch_shapes=[pltpu.VMEM((n_g, LANES, LANES), F32),
                        pltpu.VMEM((n_g, HG_L + HG_SUB, LANES), F32),
                        pltpu.VMEM((HG_SUB * LANES, LANES), BF16)],
        compiler_params=pltpu.CompilerParams(
            dimension_semantics=("arbitrary", "arbitrary")),
        name="hgrn",
    )(proj3, proj3, proj3, proj3, logf3, gain)


def _attn_kernel(q_ref, k_ref, v_ref, gvec_ref, *rest, n_cast):
    cast_in = rest[:n_cast]
    out_ref = rest[n_cast]
    cast_out = rest[n_cast + 1:2 * n_cast + 1]
    band_ref, bias_ref = rest[-2:]
    _cast_blocks(cast_in, cast_out)
    t = q_ref.shape[1]
    nq = AT_Q // CHUNK
    nk = nq + N_PAST_CHUNKS
    n_tb = 3
    wb = (nk + nq - 1) * CHUNK

    @pl.when(pl.program_id(1) == 0)
    def _build_bias():
        grow = gvec_ref[0]
        xb = jnp.broadcast_to(grow, (CHUNK, 2 * LANES))
        r = lax.broadcasted_iota(jnp.int32, (CHUNK, 2 * LANES), 0)
        for bit in range(6):
            xb = jnp.where(((r >> bit) & 1) == 1, pltpu.roll(xb, 1 << bit, axis=1), xb)
        const = grow[:, 0:1]

        n_const = nk - nq + 1 - n_tb
        lo = (nq - 1) * CHUNK
        band_ref[:, 0:lo] = jnp.full((CHUNK, lo), MASK_VALUE, F32)
        band_ref[:, lo:lo + n_const * CHUNK] = jnp.broadcast_to(const, (CHUNK, n_const * CHUNK))
        band_ref[:, lo + n_const * CHUNK:lo + (n_const + n_tb) * CHUNK] = xb[:, CHUNK:]
        band_ref[:, wb - lo:wb] = jnp.full((CHUNK, lo), MASK_VALUE, F32)
        for qi in range(nq):
            off = (nq - 1 - qi) * CHUNK
            bias_ref[qi * CHUNK:(qi + 1) * CHUNK, :] = band_ref[:, off:off + nk * CHUNK]

    for g in range(t // AT_Q):
        q0 = g * AT_Q
        ks = max(0, q0 - N_PAST_CHUNKS * CHUNK)
        kw = q0 + AT_Q - ks
        q = q_ref[0, q0:q0 + AT_Q, :]
        k = k_ref[0, ks:ks + kw, :]
        v = v_ref[0, ks:ks + kw, :]
        s = lax.dot_general(q, k, NT_DIMS, preferred_element_type=F32)
        s = s + bias_ref[:, nk * CHUNK - kw:nk * CHUNK]
        m = jnp.max(s, axis=-1, keepdims=True)
        p = jnp.exp(s - m)
        l = jnp.sum(p, axis=-1, keepdims=True)
        o = jnp.dot(p.astype(BF16), v, preferred_element_type=F32)
        out_ref[0, q0:q0 + AT_Q, :] = (o / l).astype(BF16)


def _attn(proj3, gvec, cast_srcs, *, col0, n_heads):
    bsz, t, _ = proj3.shape
    d_b = n_heads * LANES
    nq = AT_Q // CHUNK
    nk = nq + N_PAST_CHUNKS
    blk0 = col0 // LANES
    grid = (n_heads, bsz)
    cast_specs = [_cast_spec(a.shape, grid) for a in cast_srcs]

    def sec(s):
        return pl.BlockSpec((1, t, LANES), lambda h, b, s=s: (b, 0, blk0 + s * n_heads + h))

    return pl.pallas_call(
        functools.partial(_attn_kernel, n_cast=len(cast_srcs)),
        out_shape=(jax.ShapeDtypeStruct((bsz, t, d_b), BF16),
                   *[jax.ShapeDtypeStruct(a.shape, BF16) for a in cast_srcs]),
        grid=grid,
        in_specs=[sec(0), sec(1), sec(2),
                  pl.BlockSpec((1, 1, 2 * LANES), lambda h, b: (h, 0, 0)),
                  *cast_specs],
        out_specs=(pl.BlockSpec((1, t, LANES), lambda h, b: (b, 0, h)), *cast_specs),
        scratch_shapes=[pltpu.VMEM((CHUNK, (nk + nq - 1) * CHUNK), F32),
                        pltpu.VMEM((AT_Q, nk * CHUNK), F32)],
        compiler_params=pltpu.CompilerParams(
            dimension_semantics=("arbitrary", "arbitrary")),
        name="attn",
    )(proj3, proj3, proj3, gvec, *cast_srcs)


def _merge_kernel(ya_ref, yb_ref, ga0_ref, ga1_ref, gb0_ref, gb1_ref, x_ref,
                  wa_ref, wb_ref, wo_ref, gain_ref, x1_ref, h2_ref):
    pa = jnp.dot(ya_ref[...], wa_ref[...], preferred_element_type=F32)
    pb = jnp.dot(yb_ref[...], wb_ref[...], preferred_element_type=F32)
    ga = jnp.concatenate([ga0_ref[...], ga1_ref[...]], axis=1).astype(F32)
    gb = jnp.concatenate([gb0_ref[...], gb1_ref[...]], axis=1).astype(F32)
    merged = (ga * pa + gb * pb).astype(BF16)
    x1 = x_ref[...] + jnp.dot(merged, wo_ref[...], preferred_element_type=F32)
    x1_ref[...] = x1
    ms = jnp.mean(x1 * x1, axis=-1, keepdims=True)
    h2_ref[...] = (x1 * lax.rsqrt(ms + EPS) * gain_ref[...]).astype(BF16)


def _merge(ya, yb, proj, x2, wa, wb, wo, gain, *, gate_col0):
    m, d = x2.shape
    da, db = ya.shape[1], yb.shape[1]
    tm = MG_TM
    gblk = gate_col0 // IN_TN

    def gate(s):
        return pl.BlockSpec((tm, IN_TN), lambda i, s=s: (i, gblk + s))

    def whole(a):
        return pl.BlockSpec(a.shape, lambda i: (0, 0), pipeline_mode=pl.Buffered(1))

    return pl.pallas_call(
        _merge_kernel,
        out_shape=(jax.ShapeDtypeStruct((m, d), F32),
                   jax.ShapeDtypeStruct((m, d), BF16)),
        grid=(m // tm,),
        in_specs=[pl.BlockSpec((tm, da), lambda i: (i, 0)),
                  pl.BlockSpec((tm, db), lambda i: (i, 0)),
                  gate(0), gate(1), gate(2), gate(3),
                  pl.BlockSpec((tm, d), lambda i: (i, 0)),
                  whole(wa), whole(wb), whole(wo), whole(gain)],
        out_specs=(pl.BlockSpec((tm, d), lambda i: (i, 0)),
                   pl.BlockSpec((tm, d), lambda i: (i, 0))),
        compiler_params=pltpu.CompilerParams(
            dimension_semantics=("arbitrary",),
            vmem_limit_bytes=VMEM_LIMIT_BYTES),
        name="merge",
    )(ya, yb, proj, proj, proj, proj, x2, wa, wb, wo, gain)


def _ffn_kernel(h_ref, wg_ref, wu_ref, wd_ref, x1_ref, out_ref):
    @pl.when(pl.program_id(1) == 0)
    def _residual():
        out_ref[...] = x1_ref[...]

    h = h_ref[...]
    gate = jnp.dot(h, wg_ref[...], preferred_element_type=F32)
    up = jnp.dot(h, wu_ref[...], preferred_element_type=F32)
    act = (gate * _sigmoid(gate) * up).astype(BF16)
    out_ref[...] += jnp.dot(act, wd_ref[...], preferred_element_type=F32)


def _ffn(h2, x1, w_in, w_out):
    m, d = h2.shape
    dff = w_out.shape[0]
    tm, tf = FF_TM, FF_TF
    nf = dff // tf
    return pl.pallas_call(
        _ffn_kernel,
        out_shape=jax.ShapeDtypeStruct((m, d), F32),
        grid=(m // tm, nf),
        in_specs=[pl.BlockSpec((tm, d), lambda i, f: (i, 0)),
                  pl.BlockSpec((d, tf), lambda i, f: (0, f)),
                  pl.BlockSpec((d, tf), lambda i, f: (0, nf + f)),
                  pl.BlockSpec((tf, d), lambda i, f: (f, 0)),
                  pl.BlockSpec((tm, d), lambda i, f: (i, 0))],
        out_specs=pl.BlockSpec((tm, d), lambda i, f: (i, 0)),
        compiler_params=pltpu.CompilerParams(
            dimension_semantics=("arbitrary", "arbitrary"),
            vmem_limit_bytes=VMEM_LIMIT_BYTES),
        name="ffn",
    )(h2, w_in, w_in, w_out, x1)


def _bias_vector(rel_bias):
    rev = rel_bias[:, ::-1]
    pad = 2 * LANES - N_REL
    return jnp.pad(rev, ((0, 0), (pad, 0)), mode="edge")[:, None, :]


def kernel(x, w_in, b_gate, norm_mix, norm_ffn, hgrn_lb_logits, hgrn_out_gain,
           q_gain, k_gain, rel_bias, w_proj_a, w_proj_b, w_out, w_ffn_in, w_ffn_out):
    bsz, t, d = x.shape
    depth = w_in.shape[0]
    d_a = hgrn_out_gain.shape[1]
    dh = q_gain.shape[1]
    n_heads_b = rel_bias.shape[1]
    d_b = n_heads_b * dh
    n_in = w_in.shape[2]
    gate_col0 = 4 * d_a + 3 * d_b
    assert dh == LANES and d_a == IN_TN and d_b == IN_TN and depth == 1
    assert hgrn_lb_logits.shape[0] == depth + 1
    assert n_in == gate_col0 + 2 * d and t % AT_Q == 0

    m = bsz * t
    x2 = x.reshape(m, d)
    for l in range(depth):
        bias_full = jnp.concatenate([jnp.zeros((gate_col0,), F32), b_gate[l]])[None, :]
        proj, logf, w_ffn_in_bf, w_ffn_out_bf = _in_proj(
            x2, norm_mix[l][None, :], w_in[l].astype(BF16), bias_full, hgrn_lb_logits,
            jnp.tile(q_gain[l], n_heads_b)[None, :], jnp.tile(k_gain[l], n_heads_b)[None, :],
            (w_ffn_in[l], w_ffn_out[l]), scale=dh ** -0.5)
        proj3 = proj.reshape(bsz, t, n_in)
        y_a = _hgrn(proj3, logf.reshape(bsz, t, LOGF_PIECES * d_a), hgrn_out_gain[l][None, :],
                    d_a=d_a)
        y_b, wa_bf, wb_bf, wo_bf = _attn(
            proj3, _bias_vector(rel_bias[l]), (w_proj_a[l], w_proj_b[l], w_out[l]),
            col0=4 * d_a, n_heads=n_heads_b)
        x1, h2 = _merge(y_a.reshape(m, d_a), y_b.reshape(m, d_b), proj, x2,
                        wa_bf, wb_bf, wo_bf, norm_ffn[l][None, :], gate_col0=gate_col0)
        x2 = _ffn(h2, x1, w_ffn_in_bf, w_ffn_out_bf)
    return x2.reshape(bsz, t, d)
```

```python
import functools

import jax
import jax.numpy as jnp
from jax import lax
from jax.experimental import pallas as pl
from jax.experimental.pallas import tpu as pltpu

F32 = jnp.float32
BF16 = jnp.bfloat16

EPS = 1e-6
LANES = 128
BF16_SUBLANES = 16
VMEM_LIMIT_BYTES = 56 * 1024 * 1024
CHUNK = 64
N_PAST_CHUNKS = 8
REL_FUTURE = CHUNK - 1
REL_PAST = 2 * CHUNK - 1
N_REL = REL_FUTURE + REL_PAST + 1
MASK_VALUE = -1e30
BIG_EXPONENT = 1e30
LOGF_PIECES = 3
LOG2E = 1.4426950408889634

IN_TM = 1024
IN_TN = 1024
IN_SUB = 256
HG_L = 128
HG_SUB = 16
AT_Q = 4 * CHUNK
MG_TM = 256
FF_TM = 512
FF_TF = 512

NT_DIMS = (((1,), (1,)), ((), ()))


def _sigmoid(x):
    return 0.5 * jnp.tanh(0.5 * x) + 0.5


def _cast_spec(shape, grid):
    rows, cols = shape
    steps = 1
    for g in grid:
        steps *= g
    if rows % (steps * BF16_SUBLANES) == 0:
        strides = [steps // g0 for g0 in _running_products(grid)]

        def index_map(*ids):
            return (sum(i * s for i, s in zip(ids, strides)), 0)
        return pl.BlockSpec((rows // steps, cols), index_map)
    g0, g1 = grid
    assert rows % (g0 * BF16_SUBLANES) == 0 and cols % (g1 * LANES) == 0, (shape, grid)
    return pl.BlockSpec((rows // g0, cols // g1), lambda i, j: (i, j))


def _running_products(grid):
    out, p = [], 1
    for g in grid:
        p *= g
        out.append(p)
    return out


def _cast_blocks(src_refs, dst_refs):
    for src, dst in zip(src_refs, dst_refs):
        dst[...] = src[...].astype(BF16)


def _in_proj_kernel(x_ref, gain_ref, w_ref, bias_ref, lbl_ref, qg_ref, kg_ref, *rest,
                    scale, n_cast):
    cast_in = rest[:n_cast]
    out_ref, logf_ref = rest[n_cast:n_cast + 2]
    cast_out = rest[n_cast + 2:2 * n_cast + 2]
    h_ref = rest[-1]
    j = pl.program_id(1)

    @pl.when(j == 0)
    def _norm():
        x = x_ref[...]
        ms = jnp.mean(x * x, axis=-1, keepdims=True)
        h_ref[...] = (x * lax.rsqrt(ms + EPS) * gain_ref[...]).astype(BF16)

    def section(epilogue):
        _cast_blocks(cast_in, cast_out)
        for c in range(w_ref.shape[1] // IN_SUB):
            cs = slice(c * IN_SUB, (c + 1) * IN_SUB)
            epilogue(jnp.dot(h_ref[...], w_ref[:, cs], preferred_element_type=F32), cs)

    @pl.when((j == 0) | (j == 3))
    def _silu():
        def epilogue(acc, cs):
            out_ref[:, cs] = (acc * _sigmoid(acc)).astype(BF16)
        section(epilogue)

    @pl.when(j == 1)
    def _forget():
        l = lbl_ref[...]
        e = jnp.exp(l - jnp.max(l, axis=0, keepdims=True))
        lb_row = e[0:1] / jnp.sum(e, axis=0, keepdims=True)

        def epilogue(acc, cs):
            lb = lb_row[:, cs]
            sig = _sigmoid(acc)
            lf = jnp.log2(lb + (1.0 - lb) * sig)
            hi = lf.astype(BF16)
            r1 = lf - hi.astype(F32)
            mid = r1.astype(BF16)
            tn = w_ref.shape[1]
            logf_ref[:, cs] = hi
            logf_ref[:, slice(tn + cs.start, tn + cs.stop)] = mid
            logf_ref[:, slice(2 * tn + cs.start, 2 * tn + cs.stop)] = (
                r1 - mid.astype(F32)).astype(BF16)
            out_ref[:, cs] = ((1.0 - lb) * (1.0 - sig)).astype(BF16)
        section(epilogue)

    @pl.when((j == 2) | (j == 6))
    def _plain():
        def epilogue(acc, cs):
            out_ref[:, cs] = acc.astype(BF16)
        section(epilogue)

    def head_norm(gain_ref_, mult):
        def epilogue(acc, cs):
            for h in range(IN_SUB // LANES):
                t = acc[:, h * LANES:(h + 1) * LANES]
                sl = slice(cs.start + h * LANES, cs.start + (h + 1) * LANES)
                ms = jnp.mean(t * t, axis=-1, keepdims=True)
                out_ref[:, sl] = (t * lax.rsqrt(ms + EPS) * (gain_ref_[:, sl] * mult)).astype(BF16)
        section(epilogue)

    @pl.when(j == 4)
    def _qnorm():
        head_norm(qg_ref, scale)

    @pl.when(j == 5)
    def _knorm():
        head_norm(kg_ref, 1.0)

    @pl.when(j >= 7)
    def _gates():
        def epilogue(acc, cs):
            out_ref[:, cs] = _sigmoid(acc + bias_ref[:, cs]).astype(BF16)
        section(epilogue)


def _in_proj(x2, gain, w, bias_full, lb_logits, qg, kg, cast_srcs, *, scale):
    m, d = x2.shape
    n = w.shape[1]
    tm, tn = IN_TM, IN_TN
    grid = (m // tm, n // tn)
    cast_specs = [_cast_spec(a.shape, grid) for a in cast_srcs]
    return pl.pallas_call(
        functools.partial(_in_proj_kernel, scale=scale, n_cast=len(cast_srcs)),
        out_shape=(jax.ShapeDtypeStruct((m, n), BF16),
                   jax.ShapeDtypeStruct((m, LOGF_PIECES * tn), BF16),
                   *[jax.ShapeDtypeStruct(a.shape, BF16) for a in cast_srcs]),
        grid=grid,
        in_specs=[
            pl.BlockSpec((tm, d), lambda i, j: (i, 0)),
            pl.BlockSpec((1, d), lambda i, j: (0, 0)),
            pl.BlockSpec((d, tn), lambda i, j: (0, j)),
            pl.BlockSpec((1, tn), lambda i, j: (0, j)),
            pl.BlockSpec(lb_logits.shape, lambda i, j: (0, 0)),
            pl.BlockSpec((1, tn), lambda i, j: (0, 0)),
            pl.BlockSpec((1, tn), lambda i, j: (0, 0)),
            *cast_specs,
        ],
        out_specs=(pl.BlockSpec((tm, tn), lambda i, j: (i, j)),
                   pl.BlockSpec((tm, LOGF_PIECES * tn), lambda i, j: (i, 0)),
                   *cast_specs),
        scratch_shapes=[pltpu.VMEM((tm, d), BF16)],
        compiler_params=pltpu.CompilerParams(
            dimension_semantics=("arbitrary", "arbitrary"),
            vmem_limit_bytes=VMEM_LIMIT_BYTES),
        name="in_proj",
    )(x2, gain, w, bias_full, lb_logits, qg, kg, *cast_srcs)


def _hgrn_kernel(q_ref, k_ref, v_ref, g_ref, lf_ref, gain_ref, out_ref,
                 st_ref, cb_ref, sel_ref):
    n_g = st_ref.shape[0]
    L, SUB = HG_L, HG_SUB

    @pl.when(pl.program_id(1) == 0)
    def _init():
        st_ref[...] = jnp.zeros_like(st_ref)
        cb_ref[...] = jnp.full(cb_ref.shape, BIG_EXPONENT, F32)
        r = lax.broadcasted_iota(jnp.int32, sel_ref.shape, 0)
        c = lax.broadcasted_iota(jnp.int32, sel_ref.shape, 1)
        sel_ref[...] = jnp.where(c == SUB - 1 - r // LANES, 1.0, 0.0).astype(BF16)

    row = lax.broadcasted_iota(jnp.int32, (L, L), 0)
    col = lax.broadcasted_iota(jnp.int32, (L, L), 1)
    tri = (row >= col).astype(BF16)


    heads = [slice(g * LANES, (g + 1) * LANES) for g in range(n_g)]

    gw = n_g * LANES
    b = sum(jnp.dot(tri, lf_ref[0, :, p * gw:(p + 1) * gw], preferred_element_type=F32)
            for p in range(LOGF_PIECES))
    q = q_ref[0].astype(F32)
    k = k_ref[0].astype(F32)
    v_bf = v_ref[0]
    v = v_bf.astype(F32)
    c = b - jnp.log2(k)
    for g, sl in enumerate(heads):
        cb_ref[g, SUB:SUB + L, :] = c[:, sl]

    qe = (q * jnp.exp2(b)).astype(BF16)
    o_blocks = [[None] * (L // SUB) for _ in heads]
    st_old = []
    for g, sl in enumerate(heads):
        st = st_ref[g]
        st_old.append(st)
        o = lax.dot_general(qe[:, sl], st.astype(BF16), NT_DIMS, preferred_element_type=F32)
        for i in range(L // SUB):
            o_blocks[g][i] = o[i * SUB:(i + 1) * SUB]

    s_far = {}
    for i in range(1, L // SUB):
        r0 = i * SUB
        bref = b[r0:r0 + 1]
        qt = (q[r0:r0 + SUB] * jnp.exp2(b[r0:r0 + SUB] - bref)).astype(BF16)
        kt = (k[0:r0] * jnp.exp2(bref - b[0:r0])).astype(BF16)
        for g, sl in enumerate(heads):
            s_far[i, g] = lax.dot_general(qt[:, sl], kt[:, sl], NT_DIMS,
                                          preferred_element_type=F32)

    b_last = b[L - 1:L]
    kdec = (k * jnp.exp2(b_last - b)).astype(BF16)
    eb_last = jnp.exp2(b_last)
    for g, sl in enumerate(heads):
        upd = jnp.dot(jnp.transpose(v[:, sl]).astype(BF16), kdec[:, sl],
                      preferred_element_type=F32)
        st_ref[g] = st_old[g] * eb_last[:, sl] + upd

    for i in range(1, L // SUB):
        r0 = i * SUB
        far = (lax.broadcasted_iota(jnp.int32, (SUB, r0), 1)
               <= lax.broadcasted_iota(jnp.int32, (SUB, r0), 0) + (r0 - SUB))
        for g, sl in enumerate(heads):
            s = jnp.where(far, s_far[i, g], 0.0).astype(BF16)
            oi = jnp.dot(s, v_bf[0:r0, sl], preferred_element_type=F32)
            o_blocks[g][i] = o_blocks[g][i] + oi

    near = []
    for g, sl in enumerate(heads):
        qg, bg = q[:, sl], b[:, sl]
        e = [(qg * jnp.exp2(bg - cb_ref[g, SUB - d:SUB - d + L, :])).astype(BF16)
             for d in range(SUB)]
        near.append(jnp.dot(jnp.concatenate(e, axis=1), sel_ref[...],
                            preferred_element_type=F32))

    for g, sl in enumerate(heads):
        s = pltpu.roll(near[g], LANES - (SUB - 1), axis=1, stride=1, stride_axis=0)
        o = jnp.concatenate(o_blocks[g], axis=0) + jnp.dot(
            s[:, 0:L].astype(BF16), v_bf[:, sl], preferred_element_type=F32)
        ms = jnp.mean(o * o, axis=-1, keepdims=True)
        y = o * lax.rsqrt(ms + EPS) * gain_ref[:, sl] * g_ref[0, :, sl].astype(F32)
        out_ref[0, :, sl] = y.astype(BF16)


def _hgrn(proj3, logf3, gain, *, d_a):
    bsz, t, _ = proj3.shape
    n_g = d_a // LANES

    def sec(s):
        return pl.BlockSpec((1, HG_L, d_a), lambda b, c, s=s: (b, c, s))

    return pl.pallas_call(
        _hgrn_kernel,
        out_shape=jax.ShapeDtypeStruct((bsz, t, d_a), BF16),
        grid=(bsz, t // HG_L),
        in_specs=[sec(0), sec(1), sec(2), sec(3),
                  pl.BlockSpec((1, HG_L, LOGF_PIECES * d_a), lambda b, c: (b, c, 0)),
                  pl.BlockSpec((1, d_a), lambda b, c: (0, 0))],
        out_specs=pl.BlockSpec((1, HG_L, d_a), lambda b, c: (b, c, 0)),
        scratch_shapes=[pltpu.VMEM((n_g, LANES, LANES), F32),
                        pltpu.VMEM((n_g, HG_L + HG_SUB, LANES), F32),
                        pltpu.VMEM((HG_SUB * LANES, LANES), BF16)],
        compiler_params=pltpu.CompilerParams(
            dimension_semantics=("arbitrary", "arbitrary")),
        name="hgrn",
    )(proj3, proj3, proj3, proj3, logf3, gain)


def _attn_kernel(q_ref, k_ref, v_ref, gvec_ref, *rest, n_cast):
    cast_in = rest[:n_cast]
    out_ref = rest[n_cast]
    cast_out = rest[n_cast + 1:2 * n_cast + 1]
    band_ref, bias_ref, vx_ref = rest[-3:]
    _cast_blocks(cast_in, cast_out)
    t = q_ref.shape[1]
    nq = AT_Q // CHUNK
    nk = nq + N_PAST_CHUNKS
    n_tb = 3
    wb = (nk + nq - 1) * CHUNK

    @pl.when(pl.program_id(1) == 0)
    def _build_bias():
        grow = gvec_ref[0] * LOG2E
        xb = jnp.broadcast_to(grow, (CHUNK, 2 * LANES))
        r = lax.broadcasted_iota(jnp.int32, (CHUNK, 2 * LANES), 0)
        for bit in range(6):
            xb = jnp.where(((r >> bit) & 1) == 1, pltpu.roll(xb, 1 << bit, axis=1), xb)
        const = grow[:, 0:1]

        n_const = nk - nq + 1 - n_tb
        lo = (nq - 1) * CHUNK
        band_ref[:, 0:lo] = jnp.full((CHUNK, lo), MASK_VALUE, F32)
        band_ref[:, lo:lo + n_const * CHUNK] = jnp.broadcast_to(const, (CHUNK, n_const * CHUNK))
        band_ref[:, lo + n_const * CHUNK:lo + (n_const + n_tb) * CHUNK] = xb[:, CHUNK:]
        band_ref[:, wb - lo:wb] = jnp.full((CHUNK, lo), MASK_VALUE, F32)
        for qi in range(nq):
            off = (nq - 1 - qi) * CHUNK
            bias_ref[qi * CHUNK:(qi + 1) * CHUNK, :] = band_ref[:, off:off + nk * CHUNK]

    vx_ref[:, 0:LANES] = v_ref[0]
    vx_ref[:, LANES:2 * LANES] = jnp.ones((t, LANES), BF16)

    def window(g):
        q0 = g * AT_Q
        ks = max(0, q0 - N_PAST_CHUNKS * CHUNK)
        return q0, ks, q0 + AT_Q - ks

    def scores(g):
        q0, ks, kw = window(g)
        return lax.dot_general(q_ref[0, q0:q0 + AT_Q, :], k_ref[0, ks:ks + kw, :], NT_DIMS,
                               preferred_element_type=F32)

    n_groups = t // AT_Q
    s_next = scores(0)
    for g in range(n_groups):
        q0, ks, kw = window(g)
        s = s_next + bias_ref[:, nk * CHUNK - kw:nk * CHUNK]
        if g + 1 < n_groups:
            s_next = scores(g + 1)
        p = jnp.exp2(s - jnp.max(s, axis=-1, keepdims=True))
        ox = jnp.dot(p.astype(BF16), vx_ref[ks:ks + kw, :], preferred_element_type=F32)
        out_ref[0, q0:q0 + AT_Q, :] = (ox[:, 0:LANES] / ox[:, LANES:2 * LANES]).astype(BF16)


def _attn(proj3, gvec, cast_srcs, *, col0, n_heads):
    bsz, t, _ = proj3.shape
    d_b = n_heads * LANES
    nq = AT_Q // CHUNK
    nk = nq + N_PAST_CHUNKS
    blk0 = col0 // LANES
    grid = (n_heads, bsz)
    cast_specs = [_cast_spec(a.shape, grid) for a in cast_srcs]

    def sec(s):
        return pl.BlockSpec((1, t, LANES), lambda h, b, s=s: (b, 0, blk0 + s * n_heads + h))

    return pl.pallas_call(
        functools.partial(_attn_kernel, n_cast=len(cast_srcs)),
        out_shape=(jax.ShapeDtypeStruct((bsz, t, d_b), BF16),
                   *[jax.ShapeDtypeStruct(a.shape, BF16) for a in cast_srcs]),
        grid=grid,
        in_specs=[sec(0), sec(1), sec(2),
                  pl.BlockSpec((1, 1, 2 * LANES), lambda h, b: (h, 0, 0)),
                  *cast_specs],
        out_specs=(pl.BlockSpec((1, t, LANES), lambda h, b: (b, 0, h)), *cast_specs),
        scratch_shapes=[pltpu.VMEM((CHUNK, (nk + nq - 1) * CHUNK), F32),
                        pltpu.VMEM((AT_Q, nk * CHUNK), F32),
                        pltpu.VMEM((t, 2 * LANES), BF16)],
        compiler_params=pltpu.CompilerParams(
            dimension_semantics=("arbitrary", "arbitrary")),
        name="attn",
    )(proj3, proj3, proj3, gvec, *cast_srcs)


def _merge_kernel(ya_ref, yb_ref, ga0_ref, ga1_ref, gb0_ref, gb1_ref, x_ref,
                  wa_ref, wb_ref, wo_ref, gain_ref, x1_ref, h2_ref):
    pa = jnp.dot(ya_ref[...], wa_ref[...], preferred_element_type=F32)
    pb = jnp.dot(yb_ref[...], wb_ref[...], preferred_element_type=F32)
    ga = jnp.concatenate([ga0_ref[...], ga1_ref[...]], axis=1).astype(F32)
    gb = jnp.concatenate([gb0_ref[...], gb1_ref[...]], axis=1).astype(F32)
    merged = (ga * pa + gb * pb).astype(BF16)
    x1 = x_ref[...] + jnp.dot(merged, wo_ref[...], preferred_element_type=F32)
    x1_ref[...] = x1
    ms = jnp.mean(x1 * x1, axis=-1, keepdims=True)
    h2_ref[...] = (x1 * lax.rsqrt(ms + EPS) * gain_ref[...]).astype(BF16)


def _merge(ya, yb, proj, x2, wa, wb, wo, gain, *, gate_col0):
    m, d = x2.shape
    da, db = ya.shape[1], yb.shape[1]
    tm = MG_TM
    gblk = gate_col0 // IN_TN

    def gate(s):
        return pl.BlockSpec((tm, IN_TN), lambda i, s=s: (i, gblk + s))

    def whole(a):
        return pl.BlockSpec(a.shape, lambda i: (0, 0), pipeline_mode=pl.Buffered(1))

    return pl.pallas_call(
        _merge_kernel,
        out_shape=(jax.ShapeDtypeStruct((m, d), F32),
                   jax.ShapeDtypeStruct((m, d), BF16)),
        grid=(m // tm,),
        in_specs=[pl.BlockSpec((tm, da), lambda i: (i, 0)),
                  pl.BlockSpec((tm, db), lambda i: (i, 0)),
                  gate(0), gate(1), gate(2), gate(3),
                  pl.BlockSpec((tm, d), lambda i: (i, 0)),
                  whole(wa), whole(wb), whole(wo), whole(gain)],
        out_specs=(pl.BlockSpec((tm, d), lambda i: (i, 0)),
                   pl.BlockSpec((tm, d), lambda i: (i, 0))),
        compiler_params=pltpu.CompilerParams(
            dimension_semantics=("arbitrary",),
            vmem_limit_bytes=VMEM_LIMIT_BYTES),
        name="merge",
    )(ya, yb, proj, proj, proj, proj, x2, wa, wb, wo, gain)


def _ffn_kernel(h_ref, wg_ref, wu_ref, wd_ref, x1_ref, out_ref):
    @pl.when(pl.program_id(1) == 0)
    def _residual():
        out_ref[...] = x1_ref[...]

    h = h_ref[...]
    gate = jnp.dot(h, wg_ref[...], preferred_element_type=F32)
    up = jnp.dot(h, wu_ref[...], preferred_element_type=F32)
    act = (gate * _sigmoid(gate) * up).astype(BF16)
    out_ref[...] += jnp.dot(act, wd_ref[...], preferred_element_type=F32)


def _ffn(h2, x1, w_in, w_out):
    m, d = h2.shape
    dff = w_out.shape[0]
    tm, tf = FF_TM, FF_TF
    nf = dff // tf
    return pl.pallas_call(
        _ffn_kernel,
        out_shape=jax.ShapeDtypeStruct((m, d), F32),
        grid=(m // tm, nf),
        in_specs=[pl.BlockSpec((tm, d), lambda i, f: (i, 0)),
                  pl.BlockSpec((d, tf), lambda i, f: (0, f)),
                  pl.BlockSpec((d, tf), lambda i, f: (0, nf + f)),
                  pl.BlockSpec((tf, d), lambda i, f: (f, 0)),
                  pl.BlockSpec((tm, d), lambda i, f: (i, 0))],
        out_specs=pl.BlockSpec((tm, d), lambda i, f: (i, 0)),
        compiler_params=pltpu.CompilerParams(
            dimension_semantics=("arbitrary", "arbitrary"),
            vmem_limit_bytes=VMEM_LIMIT_BYTES),
        name="ffn",
    )(h2, w_in, w_in, w_out, x1)


def _bias_vector(rel_bias):
    rev = rel_bias[:, ::-1]
    pad = 2 * LANES - N_REL
    return jnp.pad(rev, ((0, 0), (pad, 0)), mode="edge")[:, None, :]


def kernel(x, w_in, b_gate, norm_mix, norm_ffn, hgrn_lb_logits, hgrn_out_gain,
           q_gain, k_gain, rel_bias, w_proj_a, w_proj_b, w_out, w_ffn_in, w_ffn_out):
    bsz, t, d = x.shape
    depth = w_in.shape[0]
    d_a = hgrn_out_gain.shape[1]
    dh = q_gain.shape[1]
    n_heads_b = rel_bias.shape[1]
    d_b = n_heads_b * dh
    n_in = w_in.shape[2]
    gate_col0 = 4 * d_a + 3 * d_b
    assert dh == LANES and d_a == IN_TN and d_b == IN_TN and depth == 1
    assert hgrn_lb_logits.shape[0] == depth + 1
    assert n_in == gate_col0 + 2 * d and t % AT_Q == 0

    m = bsz * t
    x2 = x.reshape(m, d)
    for l in range(depth):
        bias_full = jnp.concatenate([jnp.zeros((gate_col0,), F32), b_gate[l]])[None, :]
        proj, logf, w_ffn_in_bf, w_ffn_out_bf = _in_proj(
            x2, norm_mix[l][None, :], w_in[l].astype(BF16), bias_full, hgrn_lb_logits,
            jnp.tile(q_gain[l], n_heads_b)[None, :], jnp.tile(k_gain[l], n_heads_b)[None, :],
            (w_ffn_in[l], w_ffn_out[l]), scale=dh ** -0.5 * LOG2E)
        proj3 = proj.reshape(bsz, t, n_in)
        y_a = _hgrn(proj3, logf.reshape(bsz, t, LOGF_PIECES * d_a), hgrn_out_gain[l][None, :],
                    d_a=d_a)
        y_b, wa_bf, wb_bf, wo_bf = _attn(
            proj3, _bias_vector(rel_bias[l]), (w_proj_a[l], w_proj_b[l], w_out[l]),
            col0=4 * d_a, n_heads=n_heads_b)
        x1, h2 = _merge(y_a.reshape(m, d_a), y_b.reshape(m, d_b), proj, x2,
                        wa_bf, wb_bf, wo_bf, norm_ffn[l][None, :], gate_col0=gate_col0)
        x2 = _ffn(h2, x1, w_ffn_in_bf, w_ffn_out_bf)
    return x2.reshape(bsz, t, d)
```

```python
import functools

import jax
import jax.numpy as jnp
from jax import lax
from jax.experimental import pallas as pl
from jax.experimental.pallas import tpu as pltpu

F32 = jnp.float32
BF16 = jnp.bfloat16

EPS = 1e-6
LANES = 128
BF16_SUBLANES = 16
VMEM_LIMIT_BYTES = 56 * 1024 * 1024
CHUNK = 64
N_PAST_CHUNKS = 8
REL_FUTURE = CHUNK - 1
REL_PAST = 2 * CHUNK - 1
N_REL = REL_FUTURE + REL_PAST + 1
MASK_VALUE = -1e30
BIG_EXPONENT = 1e30
LOGF_PIECES = 3
LOG2E = 1.4426950408889634

IN_TM = 1024
IN_TN = 1024
IN_SUB = 256
HG_L = 128
HG_SUB = 16
AT_Q = 4 * CHUNK
MG_TM = 512
FF_TM = 1024
FF_TF = 512

NT_DIMS = (((1,), (1,)), ((), ()))


def _sigmoid(x):
    return 0.5 * jnp.tanh(0.5 * x) + 0.5


def _cast_spec(shape, grid):
    rows, cols = shape
    steps = 1
    for g in grid:
        steps *= g
    if rows % (steps * BF16_SUBLANES) == 0:
        strides = [steps // g0 for g0 in _running_products(grid)]

        def index_map(*ids):
            return (sum(i * s for i, s in zip(ids, strides)), 0)
        return pl.BlockSpec((rows // steps, cols), index_map)
    g0, g1 = grid
    assert rows % (g0 * BF16_SUBLANES) == 0 and cols % (g1 * LANES) == 0, (shape, grid)
    return pl.BlockSpec((rows // g0, cols // g1), lambda i, j: (i, j))


def _running_products(grid):
    out, p = [], 1
    for g in grid:
        p *= g
        out.append(p)
    return out


def _cast_blocks(src_refs, dst_refs):
    for src, dst in zip(src_refs, dst_refs):
        dst[...] = src[...].astype(BF16)


def _in_proj_kernel(x_hbm, gain_ref, w_ref, bias_ref, lbl_ref, qg_ref, kg_ref, *rest,
                    scale, n_cast):
    cast_in = rest[:n_cast]
    out_ref, logf_ref = rest[n_cast:n_cast + 2]
    cast_out = rest[n_cast + 2:2 * n_cast + 2]
    h_ref, x_ref, x_sem = rest[-3:]
    i, j = pl.program_id(0), pl.program_id(1)
    tm = x_ref.shape[0]

    def x_copy(tile):
        return pltpu.make_async_copy(x_hbm.at[pl.ds(tile * tm, tm), :], x_ref, x_sem)

    @pl.when((j == 0) & (i == 0))
    def _first_fetch():
        x_copy(0).start()

    @pl.when(j == 0)
    def _norm():
        x_copy(i).wait()
        x = x_ref[...]
        ms = jnp.mean(x * x, axis=-1, keepdims=True)
        h_ref[...] = (x * lax.rsqrt(ms + EPS) * gain_ref[...]).astype(BF16)

    @pl.when((j == 1) & (i + 1 < pl.num_programs(0)))
    def _next_fetch():
        x_copy(i + 1).start()

    def section(epilogue):
        _cast_blocks(cast_in, cast_out)
        rs = slice(None)
        for c in range(w_ref.shape[1] // IN_SUB):
            cs = slice(c * IN_SUB, (c + 1) * IN_SUB)
            epilogue(jnp.dot(h_ref[...], w_ref[:, cs].astype(BF16),
                             preferred_element_type=F32), rs, cs)

    @pl.when((j == 0) | (j == 3))
    def _silu():
        def epilogue(acc, rs, cs):
            out_ref[rs, cs] = (acc * _sigmoid(acc)).astype(BF16)
        section(epilogue)

    @pl.when(j == 1)
    def _forget():
        l = lbl_ref[...]
        e = jnp.exp(l - jnp.max(l, axis=0, keepdims=True))
        lb_row = e[0:1] / jnp.sum(e, axis=0, keepdims=True)

        def epilogue(acc, rs, cs):
            lb = lb_row[:, cs]
            sig = _sigmoid(acc)
            lf = jnp.log2(lb + (1.0 - lb) * sig)
            hi = lf.astype(BF16)
            r1 = lf - hi.astype(F32)
            mid = r1.astype(BF16)
            tn = w_ref.shape[1]
            logf_ref[rs, cs] = hi
            logf_ref[rs, slice(tn + cs.start, tn + cs.stop)] = mid
            logf_ref[rs, slice(2 * tn + cs.start, 2 * tn + cs.stop)] = (
                r1 - mid.astype(F32)).astype(BF16)
            out_ref[rs, cs] = ((1.0 - lb) * (1.0 - sig)).astype(BF16)
        section(epilogue)

    @pl.when((j == 2) | (j == 6))
    def _plain():
        def epilogue(acc, rs, cs):
            out_ref[rs, cs] = acc.astype(BF16)
        section(epilogue)

    def head_norm(gain_ref_, mult):
        def epilogue(acc, rs, cs):
            for h in range(IN_SUB // LANES):
                t = acc[:, h * LANES:(h + 1) * LANES]
                sl = slice(cs.start + h * LANES, cs.start + (h + 1) * LANES)
                ms = jnp.mean(t * t, axis=-1, keepdims=True)
                out_ref[rs, sl] = (t * lax.rsqrt(ms + EPS) * (gain_ref_[:, sl] * mult)).astype(BF16)
        section(epilogue)

    @pl.when(j == 4)
    def _qnorm():
        head_norm(qg_ref, scale)

    @pl.when(j == 5)
    def _knorm():
        head_norm(kg_ref, 1.0)

    @pl.when(j >= 7)
    def _gates():
        def epilogue(acc, rs, cs):
            out_ref[rs, cs] = _sigmoid(acc + bias_ref[:, cs]).astype(BF16)
        section(epilogue)


def _in_proj(x2, gain, w, bias_full, lb_logits, qg, kg, cast_srcs, *, scale):
    m, d = x2.shape
    n = w.shape[1]
    tm, tn = IN_TM, IN_TN
    grid = (m // tm, n // tn)
    cast_specs = [_cast_spec(a.shape, grid) for a in cast_srcs]
    return pl.pallas_call(
        functools.partial(_in_proj_kernel, scale=scale, n_cast=len(cast_srcs)),
        out_shape=(jax.ShapeDtypeStruct((m, n), BF16),
                   jax.ShapeDtypeStruct((m, LOGF_PIECES * tn), BF16),
                   *[jax.ShapeDtypeStruct(a.shape, BF16) for a in cast_srcs]),
        grid=grid,
        in_specs=[
            pl.BlockSpec(memory_space=pl.ANY),
            pl.BlockSpec((1, d), lambda i, j: (0, 0)),
            pl.BlockSpec((d, tn), lambda i, j: (0, j)),
            pl.BlockSpec((1, tn), lambda i, j: (0, j)),
            pl.BlockSpec(lb_logits.shape, lambda i, j: (0, 0)),
            pl.BlockSpec((1, tn), lambda i, j: (0, 0)),
            pl.BlockSpec((1, tn), lambda i, j: (0, 0)),
            *cast_specs,
        ],
        out_specs=(pl.BlockSpec((tm, tn), lambda i, j: (i, j)),
                   pl.BlockSpec((tm, LOGF_PIECES * tn), lambda i, j: (i, 0)),
                   *cast_specs),
        scratch_shapes=[pltpu.VMEM((tm, d), BF16), pltpu.VMEM((tm, d), F32),
                        pltpu.SemaphoreType.DMA(())],
        compiler_params=pltpu.CompilerParams(
            dimension_semantics=("arbitrary", "arbitrary"),
            vmem_limit_bytes=VMEM_LIMIT_BYTES),
        name="in_proj",
    )(x2, gain, w, bias_full, lb_logits, qg, kg, *cast_srcs)


def _hgrn_kernel(q_ref, k_ref, v_ref, g_ref, lf_ref, gain_ref, out_ref,
                 st_ref, cb_ref, sel_ref):
    n_g = st_ref.shape[0]
    L, SUB = HG_L, HG_SUB

    @pl.when(pl.program_id(1) == 0)
    def _init():
        st_ref[...] = jnp.zeros_like(st_ref)
        cb_ref[...] = jnp.full(cb_ref.shape, BIG_EXPONENT, F32)
        r = lax.broadcasted_iota(jnp.int32, sel_ref.shape, 0)
        c = lax.broadcasted_iota(jnp.int32, sel_ref.shape, 1)
        sel_ref[...] = jnp.where(c == SUB - 1 - r // LANES, 1.0, 0.0).astype(BF16)

    row = lax.broadcasted_iota(jnp.int32, (L, L), 0)
    col = lax.broadcasted_iota(jnp.int32, (L, L), 1)
    tri = (row >= col).astype(BF16)


    heads = [slice(g * LANES, (g + 1) * LANES) for g in range(n_g)]

    gw = n_g * LANES
    b = sum(jnp.dot(tri, lf_ref[0, :, p * gw:(p + 1) * gw], preferred_element_type=F32)
            for p in range(LOGF_PIECES))
    q = q_ref[0].astype(F32)
    k = k_ref[0].astype(F32)
    v_bf = v_ref[0]
    v = v_bf.astype(F32)
    c = b - jnp.log2(k)
    for g, sl in enumerate(heads):
        cb_ref[g, SUB:SUB + L, :] = c[:, sl]

    qe = (q * jnp.exp2(b)).astype(BF16)
    o_blocks = [[None] * (L // SUB) for _ in heads]
    st_old = []
    for g, sl in enumerate(heads):
        st = st_ref[g]
        st_old.append(st)
        o = lax.dot_general(qe[:, sl], st.astype(BF16), NT_DIMS, preferred_element_type=F32)
        for i in range(L // SUB):
            o_blocks[g][i] = o[i * SUB:(i + 1) * SUB]

    s_far = {}
    for i in range(1, L // SUB):
        r0 = i * SUB
        bref = b[r0:r0 + 1]
        qt = (q[r0:r0 + SUB] * jnp.exp2(b[r0:r0 + SUB] - bref)).astype(BF16)
        kt = (k[0:r0] * jnp.exp2(bref - b[0:r0])).astype(BF16)
        for g, sl in enumerate(heads):
            s_far[i, g] = lax.dot_general(qt[:, sl], kt[:, sl], NT_DIMS,
                                          preferred_element_type=F32)

    b_last = b[L - 1:L]
    kdec = (k * jnp.exp2(b_last - b)).astype(BF16)
    eb_last = jnp.exp2(b_last)
    for g, sl in enumerate(heads):
        upd = jnp.dot(jnp.transpose(v[:, sl]).astype(BF16), kdec[:, sl],
                      preferred_element_type=F32)
        st_ref[g] = st_old[g] * eb_last[:, sl] + upd

    for i in range(1, L // SUB):
        r0 = i * SUB
        far = (lax.broadcasted_iota(jnp.int32, (SUB, r0), 1)
               <= lax.broadcasted_iota(jnp.int32, (SUB, r0), 0) + (r0 - SUB))
        for g, sl in enumerate(heads):
            s = jnp.where(far, s_far[i, g], 0.0).astype(BF16)
            oi = jnp.dot(s, v_bf[0:r0, sl], preferred_element_type=F32)
            o_blocks[g][i] = o_blocks[g][i] + oi

    near = []
    for g, sl in enumerate(heads):
        qg, bg = q[:, sl], b[:, sl]
        e = [(qg * jnp.exp2(bg - cb_ref[g, SUB - d:SUB - d + L, :])).astype(BF16)
             for d in range(SUB)]
        near.append(jnp.dot(jnp.concatenate(e, axis=1), sel_ref[...],
                            preferred_element_type=F32))

    for g, sl in enumerate(heads):
        s = pltpu.roll(near[g], LANES - (SUB - 1), axis=1, stride=1, stride_axis=0)
        o = jnp.concatenate(o_blocks[g], axis=0) + jnp.dot(
            s[:, 0:L].astype(BF16), v_bf[:, sl], preferred_element_type=F32)
        ms = jnp.mean(o * o, axis=-1, keepdims=True)
        y = o * lax.rsqrt(ms + EPS) * gain_ref[:, sl] * g_ref[0, :, sl].astype(F32)
        out_ref[0, :, sl] = y.astype(BF16)


def _hgrn(proj3, logf3, gain, *, d_a):
    bsz, t, _ = proj3.shape
    n_g = d_a // LANES

    def sec(s):
        return pl.BlockSpec((1, HG_L, d_a), lambda b, c, s=s: (b, c, s))

    return pl.pallas_call(
        _hgrn_kernel,
        out_shape=jax.ShapeDtypeStruct((bsz, t, d_a), BF16),
        grid=(bsz, t // HG_L),
        in_specs=[sec(0), sec(1), sec(2), sec(3),
                  pl.BlockSpec((1, HG_L, LOGF_PIECES * d_a), lambda b, c: (b, c, 0)),
                  pl.BlockSpec((1, d_a), lambda b, c: (0, 0))],
        out_specs=pl.BlockSpec((1, HG_L, d_a), lambda b, c: (b, c, 0)),
        scratch_shapes=[pltpu.VMEM((n_g, LANES, LANES), F32),
                        pltpu.VMEM((n_g, HG_L + HG_SUB, LANES), F32),
                        pltpu.VMEM((HG_SUB * LANES, LANES), BF16)],
        compiler_params=pltpu.CompilerParams(
            dimension_semantics=("arbitrary", "arbitrary")),
        name="hgrn",
    )(proj3, proj3, proj3, proj3, logf3, gain)


def _attn_kernel(q_ref, k_ref, v_ref, gvec_ref, *rest, n_cast):
    cast_in = rest[:n_cast]
    out_ref = rest[n_cast]
    cast_out = rest[n_cast + 1:2 * n_cast + 1]
    band_ref, bias_ref, vx_ref = rest[-3:]
    _cast_blocks(cast_in, cast_out)
    t = q_ref.shape[1]
    nq = AT_Q // CHUNK
    nk = nq + N_PAST_CHUNKS
    n_tb = 3
    wb = (nk + nq - 1) * CHUNK

    @pl.when(pl.program_id(1) == 0)
    def _build_bias():
        grow = gvec_ref[0] * LOG2E
        xb = jnp.broadcast_to(grow, (CHUNK, 2 * LANES))
        r = lax.broadcasted_iota(jnp.int32, (CHUNK, 2 * LANES), 0)
        for bit in range(6):
            xb = jnp.where(((r >> bit) & 1) == 1, pltpu.roll(xb, 1 << bit, axis=1), xb)
        const = grow[:, 0:1]

        n_const = nk - nq + 1 - n_tb
        lo = (nq - 1) * CHUNK
        band_ref[:, 0:lo] = jnp.full((CHUNK, lo), MASK_VALUE, F32)
        band_ref[:, lo:lo + n_const * CHUNK] = jnp.broadcast_to(const, (CHUNK, n_const * CHUNK))
        band_ref[:, lo + n_const * CHUNK:lo + (n_const + n_tb) * CHUNK] = xb[:, CHUNK:]
        band_ref[:, wb - lo:wb] = jnp.full((CHUNK, lo), MASK_VALUE, F32)
        for qi in range(nq):
            off = (nq - 1 - qi) * CHUNK
            bias_ref[qi * CHUNK:(qi + 1) * CHUNK, :] = band_ref[:, off:off + nk * CHUNK]

    vx_ref[:, 0:LANES] = v_ref[0]
    vx_ref[:, LANES:2 * LANES] = jnp.ones((t, LANES), BF16)

    def window(g):
        q0 = g * AT_Q
        ks = max(0, q0 - N_PAST_CHUNKS * CHUNK)
        return q0, ks, q0 + AT_Q - ks

    def scores(g):
        q0, ks, kw = window(g)
        return lax.dot_general(q_ref[0, q0:q0 + AT_Q, :], k_ref[0, ks:ks + kw, :], NT_DIMS,
                               preferred_element_type=F32)

    n_groups = t // AT_Q
    s_next = scores(0)
    for g in range(n_groups):
        q0, ks, kw = window(g)
        s = s_next + bias_ref[:, nk * CHUNK - kw:nk * CHUNK]
        if g + 1 < n_groups:
            s_next = scores(g + 1)
        p = jnp.exp2(s - jnp.max(s, axis=-1, keepdims=True))
        ox = jnp.dot(p.astype(BF16), vx_ref[ks:ks + kw, :], preferred_element_type=F32)
        out_ref[0, q0:q0 + AT_Q, :] = (ox[:, 0:LANES] / ox[:, LANES:2 * LANES]).astype(BF16)


def _attn(proj3, gvec, cast_srcs, *, col0, n_heads):
    bsz, t, _ = proj3.shape
    d_b = n_heads * LANES
    nq = AT_Q // CHUNK
    nk = nq + N_PAST_CHUNKS
    blk0 = col0 // LANES
    grid = (n_heads, bsz)
    cast_specs = [_cast_spec(a.shape, grid) for a in cast_srcs]

    def sec(s):
        return pl.BlockSpec((1, t, LANES), lambda h, b, s=s: (b, 0, blk0 + s * n_heads + h))

    return pl.pallas_call(
        functools.partial(_attn_kernel, n_cast=len(cast_srcs)),
        out_shape=(jax.ShapeDtypeStruct((bsz, t, d_b), BF16),
                   *[jax.ShapeDtypeStruct(a.shape, BF16) for a in cast_srcs]),
        grid=grid,
        in_specs=[sec(0), sec(1), sec(2),
                  pl.BlockSpec((1, 1, 2 * LANES), lambda h, b: (h, 0, 0)),
                  *cast_specs],
        out_specs=(pl.BlockSpec((1, t, LANES), lambda h, b: (b, 0, h)), *cast_specs),
        scratch_shapes=[pltpu.VMEM((CHUNK, (nk + nq - 1) * CHUNK), F32),
                        pltpu.VMEM((AT_Q, nk * CHUNK), F32),
                        pltpu.VMEM((t, 2 * LANES), BF16)],
        compiler_params=pltpu.CompilerParams(
            dimension_semantics=("arbitrary", "arbitrary")),
        name="attn",
    )(proj3, proj3, proj3, gvec, *cast_srcs)


def _merge_kernel(ya_ref, yb_ref, ga0_ref, ga1_ref, gb0_ref, gb1_ref, x_ref,
                  wa_ref, wb_ref, wo_ref, gain_ref, x1_ref, h2_ref):
    pa = jnp.dot(ya_ref[...], wa_ref[...], preferred_element_type=F32)
    pb = jnp.dot(yb_ref[...], wb_ref[...], preferred_element_type=F32)
    ga = jnp.concatenate([ga0_ref[...], ga1_ref[...]], axis=1).astype(F32)
    gb = jnp.concatenate([gb0_ref[...], gb1_ref[...]], axis=1).astype(F32)
    merged = (ga * pa + gb * pb).astype(BF16)
    x1 = x_ref[...] + jnp.dot(merged, wo_ref[...], preferred_element_type=F32)
    x1_ref[...] = x1
    ms = jnp.mean(x1 * x1, axis=-1, keepdims=True)
    h2_ref[...] = (x1 * lax.rsqrt(ms + EPS) * gain_ref[...]).astype(BF16)


def _merge(ya, yb, proj, x2, wa, wb, wo, gain, *, gate_col0):
    m, d = x2.shape
    da, db = ya.shape[1], yb.shape[1]
    tm = MG_TM
    gblk = gate_col0 // IN_TN

    def gate(s):
        return pl.BlockSpec((tm, IN_TN), lambda i, s=s: (i, gblk + s))

    def whole(a):
        return pl.BlockSpec(a.shape, lambda i: (0, 0), pipeline_mode=pl.Buffered(1))

    return pl.pallas_call(
        _merge_kernel,
        out_shape=(jax.ShapeDtypeStruct((m, d), F32),
                   jax.ShapeDtypeStruct((m, d), BF16)),
        grid=(m // tm,),
        in_specs=[pl.BlockSpec((tm, da), lambda i: (i, 0)),
                  pl.BlockSpec((tm, db), lambda i: (i, 0)),
                  gate(0), gate(1), gate(2), gate(3),
                  pl.BlockSpec((tm, d), lambda i: (i, 0)),
                  whole(wa), whole(wb), whole(wo), whole(gain)],
        out_specs=(pl.BlockSpec((tm, d), lambda i: (i, 0)),
                   pl.BlockSpec((tm, d), lambda i: (i, 0))),
        compiler_params=pltpu.CompilerParams(
            dimension_semantics=("arbitrary",),
            vmem_limit_bytes=VMEM_LIMIT_BYTES),
        name="merge",
    )(ya, yb, proj, proj, proj, proj, x2, wa, wb, wo, gain)


def _ffn_kernel(h_ref, wg_ref, wu_ref, wd_ref, x1_hbm, out_ref, x1_ref, sem):
    i, f = pl.program_id(0), pl.program_id(1)
    tm = out_ref.shape[0]

    def residual_copy():
        return pltpu.make_async_copy(x1_hbm.at[pl.ds(i * tm, tm), :], x1_ref, sem)

    def hidden_tile():
        h = h_ref[...]
        gate = jnp.dot(h, wg_ref[...], preferred_element_type=F32)
        up = jnp.dot(h, wu_ref[...], preferred_element_type=F32)
        act = (gate * _sigmoid(gate) * up).astype(BF16)
        return jnp.dot(act, wd_ref[...], preferred_element_type=F32)

    last = pl.num_programs(1) - 1

    @pl.when(f == 0)
    def _first():
        residual_copy().start()
        out_ref[...] = hidden_tile()

    @pl.when((f > 0) & (f < last))
    def _middle():
        out_ref[...] += hidden_tile()

    @pl.when(f == last)
    def _last():
        residual_copy().wait()
        out_ref[...] += hidden_tile() + x1_ref[...]


def _ffn(h2, x1, w_in, w_out):
    m, d = h2.shape
    dff = w_out.shape[0]
    tm, tf = FF_TM, FF_TF
    nf = dff // tf
    return pl.pallas_call(
        _ffn_kernel,
        out_shape=jax.ShapeDtypeStruct((m, d), F32),
        grid=(m // tm, nf),
        in_specs=[pl.BlockSpec((tm, d), lambda i, f: (i, 0)),
                  pl.BlockSpec((d, tf), lambda i, f: (0, f)),
                  pl.BlockSpec((d, tf), lambda i, f: (0, nf + f)),
                  pl.BlockSpec((tf, d), lambda i, f: (f, 0)),
                  pl.BlockSpec(memory_space=pl.ANY)],
        out_specs=pl.BlockSpec((tm, d), lambda i, f: (i, 0)),
        scratch_shapes=[pltpu.VMEM((tm, d), F32), pltpu.SemaphoreType.DMA(())],
        compiler_params=pltpu.CompilerParams(
            dimension_semantics=("arbitrary", "arbitrary"),
            vmem_limit_bytes=VMEM_LIMIT_BYTES),
        name="ffn",
    )(h2, w_in, w_in, w_out, x1)


def _bias_vector(rel_bias):
    rev = rel_bias[:, ::-1]
    pad = 2 * LANES - N_REL
    return jnp.pad(rev, ((0, 0), (pad, 0)), mode="edge")[:, None, :]


def kernel(x, w_in, b_gate, norm_mix, norm_ffn, hgrn_lb_logits, hgrn_out_gain,
           q_gain, k_gain, rel_bias, w_proj_a, w_proj_b, w_out, w_ffn_in, w_ffn_out):
    bsz, t, d = x.shape
    depth = w_in.shape[0]
    d_a = hgrn_out_gain.shape[1]
    dh = q_gain.shape[1]
    n_heads_b = rel_bias.shape[1]
    d_b = n_heads_b * dh
    n_in = w_in.shape[2]
    gate_col0 = 4 * d_a + 3 * d_b
    assert dh == LANES and d_a == IN_TN and d_b == IN_TN and depth == 1
    assert hgrn_lb_logits.shape[0] == depth + 1
    assert n_in == gate_col0 + 2 * d and t % AT_Q == 0

    m = bsz * t
    x2 = x.reshape(m, d)
    for l in range(depth):
        bias_full = jnp.concatenate([jnp.zeros((gate_col0,), F32), b_gate[l]])[None, :]
        proj, logf, w_ffn_in_bf, w_ffn_out_bf = _in_proj(
            x2, norm_mix[l][None, :], w_in[l], bias_full, hgrn_lb_logits,
            jnp.tile(q_gain[l], n_heads_b)[None, :], jnp.tile(k_gain[l], n_heads_b)[None, :],
            (w_ffn_in[l], w_ffn_out[l]), scale=dh ** -0.5 * LOG2E)
        proj3 = proj.reshape(bsz, t, n_in)
        y_a = _hgrn(proj3, logf.reshape(bsz, t, LOGF_PIECES * d_a), hgrn_out_gain[l][None, :],
                    d_a=d_a)
        y_b, wa_bf, wb_bf, wo_bf = _attn(
            proj3, _bias_vector(rel_bias[l]), (w_proj_a[l], w_proj_b[l], w_out[l]),
            col0=4 * d_a, n_heads=n_heads_b)
        x1, h2 = _merge(y_a.reshape(m, d_a), y_b.reshape(m, d_b), proj, x2,
                        wa_bf, wb_bf, wo_bf, norm_ffn[l][None, :], gate_col0=gate_col0)
        x2 = _ffn(h2, x1, w_ffn_in_bf, w_ffn_out_bf)
    return x2.reshape(bsz, t, d)
```

```python
import functools

import jax
import jax.numpy as jnp
from jax import lax
from jax.experimental import pallas as pl
from jax.experimental.pallas import tpu as pltpu

F32 = jnp.float32
BF16 = jnp.bfloat16

EPS = 1e-6
LANES = 128
BF16_SUBLANES = 16
VMEM_LIMIT_BYTES = 56 * 1024 * 1024
CHUNK = 64
N_PAST_CHUNKS = 8
REL_FUTURE = CHUNK - 1
REL_PAST = 2 * CHUNK - 1
N_REL = REL_FUTURE + REL_PAST + 1
MASK_VALUE = -1e30
BIG_EXPONENT = 1e30
LOGF_PIECES = 3
LOG2E = 1.4426950408889634

IN_TM = 1024
IN_TN = 1024
IN_SUB = 256
IN_FIRST_GATE_SECTION = 7
HG_L = 128
HG_CHUNKS = 2
HG_SUB = 16
AT_Q = 4 * CHUNK
MG_TM = 512
FF_TM = 1024
FF_TF = 512

NT_DIMS = (((1,), (1,)), ((), ()))


def _sigmoid(x):
    return 0.5 * jnp.tanh(0.5 * x) + 0.5


def _cast_spec(shape, grid):
    rows, cols = shape
    steps = 1
    for g in grid:
        steps *= g
    if rows % (steps * BF16_SUBLANES) == 0:
        strides = [steps // g0 for g0 in _running_products(grid)]

        def index_map(*ids):
            return (sum(i * s for i, s in zip(ids, strides)), 0)
        return pl.BlockSpec((rows // steps, cols), index_map)
    g0, g1 = grid
    assert rows % (g0 * BF16_SUBLANES) == 0 and cols % (g1 * LANES) == 0, (shape, grid)
    return pl.BlockSpec((rows // g0, cols // g1), lambda i, j: (i, j))


def _running_products(grid):
    out, p = [], 1
    for g in grid:
        p *= g
        out.append(p)
    return out


def _cast_blocks(src_refs, dst_refs):
    for src, dst in zip(src_refs, dst_refs):
        dst[...] = src[...].astype(BF16)


def _in_proj_kernel(x_hbm, gain_ref, w_ref, bias_ref, lbl_ref, qg_ref, kg_ref, *rest,
                    scale, n_cast):
    cast_in = rest[:n_cast]
    out_ref, logf_ref = rest[n_cast:n_cast + 2]
    cast_out = rest[n_cast + 2:2 * n_cast + 2]
    h_ref, x_ref, x_sem = rest[-3:]
    i, j = pl.program_id(0), pl.program_id(1)
    tm = x_ref.shape[0]

    def x_copy(tile):
        return pltpu.make_async_copy(x_hbm.at[pl.ds(tile * tm, tm), :], x_ref, x_sem)

    @pl.when((j == 0) & (i == 0))
    def _first_fetch():
        x_copy(0).start()

    @pl.when(j == 0)
    def _norm():
        x_copy(i).wait()
        x = x_ref[...]
        ms = jnp.mean(x * x, axis=-1, keepdims=True)
        h_ref[...] = (x * lax.rsqrt(ms + EPS) * gain_ref[...]).astype(BF16)

    @pl.when((j == 1) & (i + 1 < pl.num_programs(0)))
    def _next_fetch():
        x_copy(i + 1).start()

    def section(epilogue):
        _cast_blocks(cast_in, cast_out)
        rs = slice(None)
        for c in range(w_ref.shape[1] // IN_SUB):
            cs = slice(c * IN_SUB, (c + 1) * IN_SUB)
            epilogue(jnp.dot(h_ref[...], w_ref[:, cs].astype(BF16),
                             preferred_element_type=F32), rs, cs)

    @pl.when((j == 0) | (j == 3))
    def _silu():
        def epilogue(acc, rs, cs):
            out_ref[rs, cs] = (acc * _sigmoid(acc)).astype(BF16)
        section(epilogue)

    @pl.when(j == 1)
    def _forget():
        l = lbl_ref[...]
        e = jnp.exp(l - jnp.max(l, axis=0, keepdims=True))
        lb_row = e[0:1] / jnp.sum(e, axis=0, keepdims=True)

        def epilogue(acc, rs, cs):
            lb = lb_row[:, cs]
            sig = _sigmoid(acc)
            lf = jnp.log2(lb + (1.0 - lb) * sig)
            hi = lf.astype(BF16)
            r1 = lf - hi.astype(F32)
            mid = r1.astype(BF16)
            tn = w_ref.shape[1]
            logf_ref[rs, cs] = hi
            logf_ref[rs, slice(tn + cs.start, tn + cs.stop)] = mid
            logf_ref[rs, slice(2 * tn + cs.start, 2 * tn + cs.stop)] = (
                r1 - mid.astype(F32)).astype(BF16)
            out_ref[rs, cs] = ((1.0 - lb) * (1.0 - sig)).astype(BF16)
        section(epilogue)

    @pl.when((j == 2) | (j == 6))
    def _plain():
        def epilogue(acc, rs, cs):
            out_ref[rs, cs] = acc.astype(BF16)
        section(epilogue)

    def head_norm(gain_ref_, mult):
        def epilogue(acc, rs, cs):
            for h in range(IN_SUB // LANES):
                t = acc[:, h * LANES:(h + 1) * LANES]
                sl = slice(cs.start + h * LANES, cs.start + (h + 1) * LANES)
                ms = jnp.mean(t * t, axis=-1, keepdims=True)
                out_ref[rs, sl] = (t * lax.rsqrt(ms + EPS) * (gain_ref_[...] * mult)).astype(BF16)
        section(epilogue)

    @pl.when(j == 4)
    def _qnorm():
        head_norm(qg_ref, scale)

    @pl.when(j == 5)
    def _knorm():
        head_norm(kg_ref, 1.0)

    @pl.when(j >= IN_FIRST_GATE_SECTION)
    def _gates():
        def epilogue(acc, rs, cs):
            out_ref[rs, cs] = _sigmoid(acc + bias_ref[:, cs]).astype(BF16)
        section(epilogue)


def _in_proj(x2, gain, w, b_gate, lb_logits, qg, kg, cast_srcs, *, scale):
    m, d = x2.shape
    n = w.shape[1]
    tm, tn = IN_TM, IN_TN
    grid = (m // tm, n // tn)
    first_gate = (n - b_gate.shape[1]) // tn
    assert first_gate == IN_FIRST_GATE_SECTION
    cast_specs = [_cast_spec(a.shape, grid) for a in cast_srcs]
    return pl.pallas_call(
        functools.partial(_in_proj_kernel, scale=scale, n_cast=len(cast_srcs)),
        out_shape=(jax.ShapeDtypeStruct((m, n), BF16),
                   jax.ShapeDtypeStruct((m, LOGF_PIECES * tn), BF16),
                   *[jax.ShapeDtypeStruct(a.shape, BF16) for a in cast_srcs]),
        grid=grid,
        in_specs=[
            pl.BlockSpec(memory_space=pl.ANY),
            pl.BlockSpec((1, d), lambda i, j: (0, 0)),
            pl.BlockSpec((d, tn), lambda i, j: (0, j)),
            pl.BlockSpec((1, tn), lambda i, j: (0, jnp.maximum(j - first_gate, 0))),
            pl.BlockSpec(lb_logits.shape, lambda i, j: (0, 0)),
            pl.BlockSpec((1, LANES), lambda i, j: (0, 0)),
            pl.BlockSpec((1, LANES), lambda i, j: (0, 0)),
            *cast_specs,
        ],
        out_specs=(pl.BlockSpec((tm, tn), lambda i, j: (i, j)),
                   pl.BlockSpec((tm, LOGF_PIECES * tn), lambda i, j: (i, 0)),
                   *cast_specs),
        scratch_shapes=[pltpu.VMEM((tm, d), BF16), pltpu.VMEM((tm, d), F32),
                        pltpu.SemaphoreType.DMA(())],
        compiler_params=pltpu.CompilerParams(
            dimension_semantics=("arbitrary", "arbitrary"),
            vmem_limit_bytes=VMEM_LIMIT_BYTES),
        name="in_proj",
    )(x2, gain, w, b_gate, lb_logits, qg, kg, *cast_srcs)


def _hgrn_kernel(q_ref, k_ref, v_ref, g_ref, lf_ref, gain_ref, out_ref,
                 st_ref, cb_ref, sel_ref, tri_ref):
    n_g = st_ref.shape[0]
    L, SUB = HG_L, HG_SUB

    @pl.when(pl.program_id(1) == 0)
    def _init():
        st_ref[...] = jnp.zeros_like(st_ref)
        cb_ref[...] = jnp.full(cb_ref.shape, BIG_EXPONENT, F32)
        r = lax.broadcasted_iota(jnp.int32, sel_ref.shape, 0)
        c = lax.broadcasted_iota(jnp.int32, sel_ref.shape, 1)
        sel_ref[...] = jnp.where(c == SUB - 1 - r // LANES, 1.0, 0.0).astype(BF16)
        row = lax.broadcasted_iota(jnp.int32, (L, L), 0)
        col = lax.broadcasted_iota(jnp.int32, (L, L), 1)
        tri_ref[...] = jnp.where(row >= col, 1.0, 0.0).astype(BF16)

    heads = [slice(g * LANES, (g + 1) * LANES) for g in range(n_g)]
    gw = n_g * LANES
    state = [st_ref[g] for g in range(n_g)]
    for ch in range(HG_CHUNKS):
        state = _hgrn_chunk(slice(ch * L, (ch + 1) * L), ch, state, heads, gw,
                            q_ref, k_ref, v_ref, g_ref, lf_ref, gain_ref, out_ref,
                            cb_ref, sel_ref, tri_ref)
    for g in range(n_g):
        st_ref[g] = state[g]


def _hgrn_chunk(rows, ch, state, heads, gw, q_ref, k_ref, v_ref, g_ref, lf_ref, gain_ref,
                out_ref, cb_ref, sel_ref, tri_ref):
    L, SUB = HG_L, HG_SUB
    tri = tri_ref[...]

    b = sum(jnp.dot(tri, lf_ref[0, rows, p * gw:(p + 1) * gw], preferred_element_type=F32)
            for p in range(LOGF_PIECES))
    q = q_ref[0, rows, :].astype(F32)
    k = k_ref[0, rows, :].astype(F32)
    v_bf = v_ref[0, rows, :]
    v = v_bf.astype(F32)
    c = b - jnp.log2(k)
    for g, sl in enumerate(heads):
        cb_ref[ch, g, SUB:SUB + L, :] = c[:, sl]

    qe = (q * jnp.exp2(b)).astype(BF16)
    o_blocks = [[None] * (L // SUB) for _ in heads]
    for g, sl in enumerate(heads):
        o = lax.dot_general(qe[:, sl], state[g].astype(BF16), NT_DIMS,
                            preferred_element_type=F32)
        for i in range(L // SUB):
            o_blocks[g][i] = o[i * SUB:(i + 1) * SUB]

    s_far = {}
    for i in range(1, L // SUB):
        r0 = i * SUB
        bref = b[r0:r0 + 1]
        qt = (q[r0:r0 + SUB] * jnp.exp2(b[r0:r0 + SUB] - bref)).astype(BF16)
        kt = (k[0:r0] * jnp.exp2(bref - b[0:r0])).astype(BF16)
        for g, sl in enumerate(heads):
            s_far[i, g] = lax.dot_general(qt[:, sl], kt[:, sl], NT_DIMS,
                                          preferred_element_type=F32)

    b_last = b[L - 1:L]
    kdec = (k * jnp.exp2(b_last - b)).astype(BF16)
    eb_last = jnp.exp2(b_last)
    new_state = []
    for g, sl in enumerate(heads):
        upd = jnp.dot(jnp.transpose(v[:, sl]).astype(BF16), kdec[:, sl],
                      preferred_element_type=F32)
        new_state.append(state[g] * eb_last[:, sl] + upd)

    for i in range(1, L // SUB):
        r0 = i * SUB
        far = (lax.broadcasted_iota(jnp.int32, (SUB, r0), 1)
               <= lax.broadcasted_iota(jnp.int32, (SUB, r0), 0) + (r0 - SUB))
        for g, sl in enumerate(heads):
            s = jnp.where(far, s_far[i, g], 0.0).astype(BF16)
            oi = jnp.dot(s, v_bf[0:r0, sl], preferred_element_type=F32)
            o_blocks[g][i] = o_blocks[g][i] + oi

    near = []
    for g, sl in enumerate(heads):
        qg, bg = q[:, sl], b[:, sl]
        e = [(qg * jnp.exp2(bg - cb_ref[ch, g, SUB - d:SUB - d + L, :])).astype(BF16)
             for d in range(SUB)]
        near.append(jnp.dot(jnp.concatenate(e, axis=1), sel_ref[...],
                            preferred_element_type=F32))

    for g, sl in enumerate(heads):
        s = pltpu.roll(near[g], LANES - (SUB - 1), axis=1, stride=1, stride_axis=0)
        o = jnp.concatenate(o_blocks[g], axis=0) + jnp.dot(
            s[:, 0:L].astype(BF16), v_bf[:, sl], preferred_element_type=F32)
        ms = jnp.mean(o * o, axis=-1, keepdims=True)
        y = o * lax.rsqrt(ms + EPS) * gain_ref[:, sl] * g_ref[0, rows, sl].astype(F32)
        out_ref[0, rows, sl] = y.astype(BF16)
    return new_state


def _hgrn(proj3, logf3, gain, *, d_a):
    bsz, t, _ = proj3.shape
    n_g = d_a // LANES
    rows = HG_CHUNKS * HG_L

    def sec(s):
        return pl.BlockSpec((1, rows, d_a), lambda b, c, s=s: (b, c, s))

    return pl.pallas_call(
        _hgrn_kernel,
        out_shape=jax.ShapeDtypeStruct((bsz, t, d_a), BF16),
        grid=(bsz, t // rows),
        in_specs=[sec(0), sec(1), sec(2), sec(3),
                  pl.BlockSpec((1, rows, LOGF_PIECES * d_a), lambda b, c: (b, c, 0)),
                  pl.BlockSpec((1, d_a), lambda b, c: (0, 0))],
        out_specs=pl.BlockSpec((1, rows, d_a), lambda b, c: (b, c, 0)),
        scratch_shapes=[pltpu.VMEM((n_g, LANES, LANES), F32),
                        pltpu.VMEM((HG_CHUNKS, n_g, HG_L + HG_SUB, LANES), F32),
                        pltpu.VMEM((HG_SUB * LANES, LANES), BF16),
                        pltpu.VMEM((HG_L, HG_L), BF16)],
        compiler_params=pltpu.CompilerParams(
            dimension_semantics=("arbitrary", "arbitrary")),
        name="hgrn",
    )(proj3, proj3, proj3, proj3, logf3, gain)


def _attn_kernel(q_ref, k_ref, v_ref, gvec_ref, *rest, n_cast):
    cast_in = rest[:n_cast]
    out_ref = rest[n_cast]
    cast_out = rest[n_cast + 1:2 * n_cast + 1]
    band_ref, bias_ref, vx_ref = rest[-3:]
    _cast_blocks(cast_in, cast_out)
    t = q_ref.shape[1]
    nq = AT_Q // CHUNK
    nk = nq + N_PAST_CHUNKS
    n_tb = 3
    wb = (nk + nq - 1) * CHUNK

    @pl.when(pl.program_id(1) == 0)
    def _build_bias():
        grow = gvec_ref[0] * LOG2E
        xb = jnp.broadcast_to(grow, (CHUNK, 2 * LANES))
        r = lax.broadcasted_iota(jnp.int32, (CHUNK, 2 * LANES), 0)
        for bit in range(6):
            xb = jnp.where(((r >> bit) & 1) == 1, pltpu.roll(xb, 1 << bit, axis=1), xb)
        const = grow[:, 0:1]

        n_const = nk - nq + 1 - n_tb
        lo = (nq - 1) * CHUNK
        band_ref[:, 0:lo] = jnp.full((CHUNK, lo), MASK_VALUE, F32)
        band_ref[:, lo:lo + n_const * CHUNK] = jnp.broadcast_to(const, (CHUNK, n_const * CHUNK))
        band_ref[:, lo + n_const * CHUNK:lo + (n_const + n_tb) * CHUNK] = xb[:, CHUNK:]
        band_ref[:, wb - lo:wb] = jnp.full((CHUNK, lo), MASK_VALUE, F32)
        for qi in range(nq):
            off = (nq - 1 - qi) * CHUNK
            bias_ref[qi * CHUNK:(qi + 1) * CHUNK, :] = band_ref[:, off:off + nk * CHUNK]

    vx_ref[:, 0:LANES] = v_ref[0]
    vx_ref[:, LANES:2 * LANES] = jnp.ones((t, LANES), BF16)

    def window(g):
        q0 = g * AT_Q
        ks = max(0, q0 - N_PAST_CHUNKS * CHUNK)
        return q0, ks, q0 + AT_Q - ks

    def scores(g):
        q0, ks, kw = window(g)
        return lax.dot_general(q_ref[0, q0:q0 + AT_Q, :], k_ref[0, ks:ks + kw, :], NT_DIMS,
                               preferred_element_type=F32)

    n_groups = t // AT_Q
    s_next = scores(0)
    for g in range(n_groups):
        q0, ks, kw = window(g)
        s = s_next + bias_ref[:, nk * CHUNK - kw:nk * CHUNK]
        if g + 1 < n_groups:
            s_next = scores(g + 1)
        p = jnp.exp2(s - jnp.max(s, axis=-1, keepdims=True))
        ox = jnp.dot(p.astype(BF16), vx_ref[ks:ks + kw, :], preferred_element_type=F32)
        out_ref[0, q0:q0 + AT_Q, :] = (ox[:, 0:LANES] / ox[:, LANES:2 * LANES]).astype(BF16)


def _attn(proj3, gvec, cast_srcs, *, col0, n_heads):
    bsz, t, _ = proj3.shape
    d_b = n_heads * LANES
    nq = AT_Q // CHUNK
    nk = nq + N_PAST_CHUNKS
    blk0 = col0 // LANES
    grid = (n_heads, bsz)
    cast_specs = [_cast_spec(a.shape, grid) for a in cast_srcs]

    def sec(s):
        return pl.BlockSpec((1, t, LANES), lambda h, b, s=s: (b, 0, blk0 + s * n_heads + h))

    return pl.pallas_call(
        functools.partial(_attn_kernel, n_cast=len(cast_srcs)),
        out_shape=(jax.ShapeDtypeStruct((bsz, t, d_b), BF16),
                   *[jax.ShapeDtypeStruct(a.shape, BF16) for a in cast_srcs]),
        grid=grid,
        in_specs=[sec(0), sec(1), sec(2),
                  pl.BlockSpec((1, 1, 2 * LANES), lambda h, b: (h, 0, 0)),
                  *cast_specs],
        out_specs=(pl.BlockSpec((1, t, LANES), lambda h, b: (b, 0, h)), *cast_specs),
        scratch_shapes=[pltpu.VMEM((CHUNK, (nk + nq - 1) * CHUNK), F32),
                        pltpu.VMEM((AT_Q, nk * CHUNK), F32),
                        pltpu.VMEM((t, 2 * LANES), BF16)],
        compiler_params=pltpu.CompilerParams(
            dimension_semantics=("arbitrary", "arbitrary")),
        name="attn",
    )(proj3, proj3, proj3, gvec, *cast_srcs)


def _merge_kernel(ya_ref, yb_ref, ga0_ref, ga1_ref, gb0_ref, gb1_ref, x_ref,
                  wa_ref, wb_ref, wo_ref, gain_ref, x1_ref, h2_ref):
    pa = jnp.dot(ya_ref[...], wa_ref[...], preferred_element_type=F32)
    pb = jnp.dot(yb_ref[...], wb_ref[...], preferred_element_type=F32)
    ga = jnp.concatenate([ga0_ref[...], ga1_ref[...]], axis=1).astype(F32)
    gb = jnp.concatenate([gb0_ref[...], gb1_ref[...]], axis=1).astype(F32)
    merged = (ga * pa + gb * pb).astype(BF16)
    x1 = x_ref[...] + jnp.dot(merged, wo_ref[...], preferred_element_type=F32)
    x1_ref[...] = x1
    ms = jnp.mean(x1 * x1, axis=-1, keepdims=True)
    h2_ref[...] = (x1 * lax.rsqrt(ms + EPS) * gain_ref[...]).astype(BF16)


def _merge(ya, yb, proj, x2, wa, wb, wo, gain, *, gate_col0):
    m, d = x2.shape
    da, db = ya.shape[1], yb.shape[1]
    tm = MG_TM
    gblk = gate_col0 // IN_TN

    def gate(s):
        return pl.BlockSpec((tm, IN_TN), lambda i, s=s: (i, gblk + s))

    def whole(a):
        return pl.BlockSpec(a.shape, lambda i: (0, 0), pipeline_mode=pl.Buffered(1))

    return pl.pallas_call(
        _merge_kernel,
        out_shape=(jax.ShapeDtypeStruct((m, d), F32),
                   jax.ShapeDtypeStruct((m, d), BF16)),
        grid=(m // tm,),
        in_specs=[pl.BlockSpec((tm, da), lambda i: (i, 0)),
                  pl.BlockSpec((tm, db), lambda i: (i, 0)),
                  gate(0), gate(1), gate(2), gate(3),
                  pl.BlockSpec((tm, d), lambda i: (i, 0)),
                  whole(wa), whole(wb), whole(wo), whole(gain)],
        out_specs=(pl.BlockSpec((tm, d), lambda i: (i, 0)),
                   pl.BlockSpec((tm, d), lambda i: (i, 0))),
        compiler_params=pltpu.CompilerParams(
            dimension_semantics=("arbitrary",),
            vmem_limit_bytes=VMEM_LIMIT_BYTES),
        name="merge",
    )(ya, yb, proj, proj, proj, proj, x2, wa, wb, wo, gain)


def _ffn_kernel(h_ref, wg_ref, wu_ref, wd_ref, x1_hbm, out_ref, x1_ref, sem):
    i, f = pl.program_id(0), pl.program_id(1)
    tm = out_ref.shape[0]

    def residual_copy():
        return pltpu.make_async_copy(x1_hbm.at[pl.ds(i * tm, tm), :], x1_ref, sem)

    def hidden_tile():
        h = h_ref[...]
        gate = jnp.dot(h, wg_ref[...], preferred_element_type=F32)
        up = jnp.dot(h, wu_ref[...], preferred_element_type=F32)
        act = (gate * _sigmoid(gate) * up).astype(BF16)
        return jnp.dot(act, wd_ref[...], preferred_element_type=F32)

    last = pl.num_programs(1) - 1

    @pl.when(f == 0)
    def _first():
        residual_copy().start()
        out_ref[...] = hidden_tile()

    @pl.when((f > 0) & (f < last))
    def _middle():
        out_ref[...] += hidden_tile()

    @pl.when(f == last)
    def _last():
        residual_copy().wait()
        out_ref[...] += hidden_tile() + x1_ref[...]


def _ffn(h2, x1, w_in, w_out):
    m, d = h2.shape
    dff = w_out.shape[0]
    tm, tf = FF_TM, FF_TF
    nf = dff // tf
    return pl.pallas_call(
        _ffn_kernel,
        out_shape=jax.ShapeDtypeStruct((m, d), F32),
        grid=(m // tm, nf),
        in_specs=[pl.BlockSpec((tm, d), lambda i, f: (i, 0)),
                  pl.BlockSpec((d, tf), lambda i, f: (0, f)),
                  pl.BlockSpec((d, tf), lambda i, f: (0, nf + f)),
                  pl.BlockSpec((tf, d), lambda i, f: (f, 0)),
                  pl.BlockSpec(memory_space=pl.ANY)],
        out_specs=pl.BlockSpec((tm, d), lambda i, f: (i, 0)),
        scratch_shapes=[pltpu.VMEM((tm, d), F32), pltpu.SemaphoreType.DMA(())],
        compiler_params=pltpu.CompilerParams(
            dimension_semantics=("arbitrary", "arbitrary"),
            vmem_limit_bytes=VMEM_LIMIT_BYTES),
        name="ffn",
    )(h2, w_in, w_in, w_out, x1)


def _bias_vector(rel_bias):
    rev = rel_bias[:, ::-1]
    pad = 2 * LANES - N_REL
    return jnp.pad(rev, ((0, 0), (pad, 0)), mode="edge")[:, None, :]


def kernel(x, w_in, b_gate, norm_mix, norm_ffn, hgrn_lb_logits, hgrn_out_gain,
           q_gain, k_gain, rel_bias, w_proj_a, w_proj_b, w_out, w_ffn_in, w_ffn_out):
    bsz, t, d = x.shape
    depth = w_in.shape[0]
    d_a = hgrn_out_gain.shape[1]
    dh = q_gain.shape[1]
    n_heads_b = rel_bias.shape[1]
    d_b = n_heads_b * dh
    n_in = w_in.shape[2]
    gate_col0 = 4 * d_a + 3 * d_b
    assert dh == LANES and d_a == IN_TN and d_b == IN_TN and depth == 1
    assert hgrn_lb_logits.shape[0] == depth + 1
    assert n_in == gate_col0 + 2 * d and t % AT_Q == 0

    m = bsz * t
    x2 = x.reshape(m, d)
    for l in range(depth):
        proj, logf, w_ffn_in_bf, w_ffn_out_bf = _in_proj(
            x2, norm_mix[l][None, :], w_in[l], b_gate[l][None, :], hgrn_lb_logits,
            q_gain[l][None, :], k_gain[l][None, :],
            (w_ffn_in[l], w_ffn_out[l]), scale=dh ** -0.5 * LOG2E)
        proj3 = proj.reshape(bsz, t, n_in)
        y_a = _hgrn(proj3, logf.reshape(bsz, t, LOGF_PIECES * d_a), hgrn_out_gain[l][None, :],
                    d_a=d_a)
        y_b, wa_bf, wb_bf, wo_bf = _attn(
            proj3, _bias_vector(rel_bias[l]), (w_proj_a[l], w_proj_b[l], w_out[l]),
            col0=4 * d_a, n_heads=n_heads_b)
        x1, h2 = _merge(y_a.reshape(m, d_a), y_b.reshape(m, d_b), proj, x2,
                        wa_bf, wb_bf, wo_bf, norm_ffn[l][None, :], gate_col0=gate_col0)
        x2 = _ffn(h2, x1, w_ffn_in_bf, w_ffn_out_bf)
    return x2.reshape(bsz, t, d)
```

```python
import functools

import jax
import jax.numpy as jnp
from jax import lax
from jax.experimental import pallas as pl
from jax.experimental.pallas import tpu as pltpu

F32 = jnp.float32
BF16 = jnp.bfloat16

EPS = 1e-6
LANES = 128
BF16_SUBLANES = 16
VMEM_LIMIT_BYTES = 56 * 1024 * 1024
CHUNK = 64
N_PAST_CHUNKS = 8
REL_FUTURE = CHUNK - 1
REL_PAST = 2 * CHUNK - 1
N_REL = REL_FUTURE + REL_PAST + 1
MASK_VALUE = -1e30
BIG_EXPONENT = 1e30
LOGF_PIECES = 3
LOG2E = 1.4426950408889634

IN_TM = 1024
IN_TN = 1024
IN_SUB = 256
IN_FIRST_GATE_SECTION = 7
HG_L = 128
HG_CHUNKS = 2
HG_SUB = 16
AT_Q = 4 * CHUNK
MG_TM = 512
FF_TM = 1024
FF_TF = 512

NT_DIMS = (((1,), (1,)), ((), ()))


def _sigmoid(x):
    return 0.5 * jnp.tanh(0.5 * x) + 0.5


def _cast_spec(shape, grid):
    rows, cols = shape
    steps = 1
    for g in grid:
        steps *= g
    if rows % (steps * BF16_SUBLANES) == 0:
        strides = [steps // g0 for g0 in _running_products(grid)]

        def index_map(*ids):
            return (sum(i * s for i, s in zip(ids, strides)), 0)
        return pl.BlockSpec((rows // steps, cols), index_map)
    g0, g1 = grid
    assert rows % (g0 * BF16_SUBLANES) == 0 and cols % (g1 * LANES) == 0, (shape, grid)
    return pl.BlockSpec((rows // g0, cols // g1), lambda i, j: (i, j))


def _running_products(grid):
    out, p = [], 1
    for g in grid:
        p *= g
        out.append(p)
    return out


def _cast_blocks(src_refs, dst_refs):
    for src, dst in zip(src_refs, dst_refs):
        dst[...] = src[...].astype(BF16)


def _in_proj_kernel(x_hbm, gain_ref, w_ref, bias_ref, lbl_ref, qg_ref, kg_ref, *rest,
                    scale, n_cast):
    cast_in = rest[:n_cast]
    out_ref, logf_ref = rest[n_cast:n_cast + 2]
    cast_out = rest[n_cast + 2:2 * n_cast + 2]
    h_ref, x_ref, x_sem = rest[-3:]
    i, j = pl.program_id(0), pl.program_id(1)
    tm = x_ref.shape[0]

    def x_copy(tile):
        return pltpu.make_async_copy(x_hbm.at[pl.ds(tile * tm, tm), :], x_ref, x_sem)

    @pl.when((j == 0) & (i == 0))
    def _first_fetch():
        x_copy(0).start()

    @pl.when(j == 0)
    def _norm():
        x_copy(i).wait()
        x = x_ref[...]
        ms = jnp.mean(x * x, axis=-1, keepdims=True)
        h_ref[...] = (x * lax.rsqrt(ms + EPS) * gain_ref[...]).astype(BF16)

    @pl.when((j == 1) & (i + 1 < pl.num_programs(0)))
    def _next_fetch():
        x_copy(i + 1).start()

    def section(epilogue):
        _cast_blocks(cast_in, cast_out)
        rs = slice(None)
        for c in range(w_ref.shape[1] // IN_SUB):
            cs = slice(c * IN_SUB, (c + 1) * IN_SUB)
            epilogue(jnp.dot(h_ref[...], w_ref[:, cs].astype(BF16),
                             preferred_element_type=F32), rs, cs)

    @pl.when((j == 0) | (j == 3))
    def _silu():
        def epilogue(acc, rs, cs):
            out_ref[rs, cs] = (acc * _sigmoid(acc)).astype(BF16)
        section(epilogue)

    @pl.when(j == 1)
    def _forget():
        l = lbl_ref[...]
        e = jnp.exp(l - jnp.max(l, axis=0, keepdims=True))
        lb_row = e[0:1] / jnp.sum(e, axis=0, keepdims=True)

        def epilogue(acc, rs, cs):
            lb = lb_row[:, cs]
            sig = _sigmoid(acc)
            lf = jnp.log2(lb + (1.0 - lb) * sig)
            hi = lf.astype(BF16)
            r1 = lf - hi.astype(F32)
            mid = r1.astype(BF16)
            tn = w_ref.shape[1]
            logf_ref[rs, cs] = hi
            logf_ref[rs, slice(tn + cs.start, tn + cs.stop)] = mid
            logf_ref[rs, slice(2 * tn + cs.start, 2 * tn + cs.stop)] = (
                r1 - mid.astype(F32)).astype(BF16)
            out_ref[rs, cs] = ((1.0 - lb) * (1.0 - sig)).astype(BF16)
        section(epilogue)

    @pl.when((j == 2) | (j == 6))
    def _plain():
        def epilogue(acc, rs, cs):
            out_ref[rs, cs] = acc.astype(BF16)
        section(epilogue)

    def head_norm(gain_ref_, mult):
        def epilogue(acc, rs, cs):
            for h in range(IN_SUB // LANES):
                t = acc[:, h * LANES:(h + 1) * LANES]
                sl = slice(cs.start + h * LANES, cs.start + (h + 1) * LANES)
                ms = jnp.mean(t * t, axis=-1, keepdims=True)
                out_ref[rs, sl] = (t * lax.rsqrt(ms + EPS) * (gain_ref_[...] * mult)).astype(BF16)
        section(epilogue)

    @pl.when(j == 4)
    def _qnorm():
        head_norm(qg_ref, scale)

    @pl.when(j == 5)
    def _knorm():
        head_norm(kg_ref, 1.0)

    @pl.when(j >= IN_FIRST_GATE_SECTION)
    def _gates():
        def epilogue(acc, rs, cs):
            out_ref[rs, cs] = _sigmoid(acc + bias_ref[:, cs]).astype(BF16)
        section(epilogue)


def _in_proj(x2, gain, w, b_gate, lb_logits, qg, kg, cast_srcs, *, scale):
    m, d = x2.shape
    n = w.shape[1]
    tm, tn = IN_TM, IN_TN
    grid = (m // tm, n // tn)
    first_gate = (n - b_gate.shape[1]) // tn
    assert first_gate == IN_FIRST_GATE_SECTION
    cast_specs = [_cast_spec(a.shape, grid) for a in cast_srcs]
    return pl.pallas_call(
        functools.partial(_in_proj_kernel, scale=scale, n_cast=len(cast_srcs)),
        out_shape=(jax.ShapeDtypeStruct((m, n), BF16),
                   jax.ShapeDtypeStruct((m, LOGF_PIECES * tn), BF16),
                   *[jax.ShapeDtypeStruct(a.shape, BF16) for a in cast_srcs]),
        grid=grid,
        in_specs=[
            pl.BlockSpec(memory_space=pl.ANY),
            pl.BlockSpec((1, d), lambda i, j: (0, 0)),
            pl.BlockSpec((d, tn), lambda i, j: (0, j)),
            pl.BlockSpec((1, tn), lambda i, j: (0, jnp.maximum(j - first_gate, 0))),
            pl.BlockSpec(lb_logits.shape, lambda i, j: (0, 0)),
            pl.BlockSpec((1, LANES), lambda i, j: (0, 0)),
            pl.BlockSpec((1, LANES), lambda i, j: (0, 0)),
            *cast_specs,
        ],
        out_specs=(pl.BlockSpec((tm, tn), lambda i, j: (i, j)),
                   pl.BlockSpec((tm, LOGF_PIECES * tn), lambda i, j: (i, 0)),
                   *cast_specs),
        scratch_shapes=[pltpu.VMEM((tm, d), BF16), pltpu.VMEM((tm, d), F32),
                        pltpu.SemaphoreType.DMA(())],
        compiler_params=pltpu.CompilerParams(
            dimension_semantics=("arbitrary", "arbitrary"),
            vmem_limit_bytes=VMEM_LIMIT_BYTES),
        name="in_proj",
    )(x2, gain, w, b_gate, lb_logits, qg, kg, *cast_srcs)


def _hgrn_kernel(q_ref, k_ref, v_ref, g_ref, lf_ref, gain_ref, *rest, n_cast):
    cast_in = rest[:n_cast]
    out_ref = rest[n_cast]
    cast_out = rest[n_cast + 1:2 * n_cast + 1]
    st_ref, cb_ref, sel_ref, tri_ref = rest[-4:]
    _cast_blocks(cast_in, cast_out)
    n_g = st_ref.shape[0]
    L, SUB = HG_L, HG_SUB

    @pl.when(pl.program_id(1) == 0)
    def _init():
        st_ref[...] = jnp.zeros_like(st_ref)
        cb_ref[...] = jnp.full(cb_ref.shape, BIG_EXPONENT, F32)
        r = lax.broadcasted_iota(jnp.int32, sel_ref.shape, 0)
        c = lax.broadcasted_iota(jnp.int32, sel_ref.shape, 1)
        sel_ref[...] = jnp.where(c == SUB - 1 - r // LANES, 1.0, 0.0).astype(BF16)
        row = lax.broadcasted_iota(jnp.int32, (L, L), 0)
        col = lax.broadcasted_iota(jnp.int32, (L, L), 1)
        tri_ref[...] = jnp.where(row >= col, 1.0, 0.0).astype(BF16)

    heads = [slice(g * LANES, (g + 1) * LANES) for g in range(n_g)]
    gw = n_g * LANES
    state = [st_ref[g] for g in range(n_g)]
    for ch in range(HG_CHUNKS):
        state = _hgrn_chunk(slice(ch * L, (ch + 1) * L), ch, state, heads, gw,
                            q_ref, k_ref, v_ref, g_ref, lf_ref, gain_ref, out_ref,
                            cb_ref, sel_ref, tri_ref)
    for g in range(n_g):
        st_ref[g] = state[g]


def _hgrn_chunk(rows, ch, state, heads, gw, q_ref, k_ref, v_ref, g_ref, lf_ref, gain_ref,
                out_ref, cb_ref, sel_ref, tri_ref):
    L, SUB = HG_L, HG_SUB
    tri = tri_ref[...]

    b = sum(jnp.dot(tri, lf_ref[0, rows, p * gw:(p + 1) * gw], preferred_element_type=F32)
            for p in range(LOGF_PIECES))
    q = q_ref[0, rows, :].astype(F32)
    k = k_ref[0, rows, :].astype(F32)
    v_bf = v_ref[0, rows, :]
    v = v_bf.astype(F32)
    c = b - jnp.log2(k)
    for g, sl in enumerate(heads):
        cb_ref[ch, g, SUB:SUB + L, :] = c[:, sl]

    qe = (q * jnp.exp2(b)).astype(BF16)
    o_blocks = [[None] * (L // SUB) for _ in heads]
    for g, sl in enumerate(heads):
        o = lax.dot_general(qe[:, sl], state[g].astype(BF16), NT_DIMS,
                            preferred_element_type=F32)
        for i in range(L // SUB):
            o_blocks[g][i] = o[i * SUB:(i + 1) * SUB]

    s_far = {}
    for i in range(1, L // SUB):
        r0 = i * SUB
        bref = b[r0:r0 + 1]
        qt = (q[r0:r0 + SUB] * jnp.exp2(b[r0:r0 + SUB] - bref)).astype(BF16)
        kt = (k[0:r0] * jnp.exp2(bref - b[0:r0])).astype(BF16)
        for g, sl in enumerate(heads):
            s_far[i, g] = lax.dot_general(qt[:, sl], kt[:, sl], NT_DIMS,
                                          preferred_element_type=F32)

    b_last = b[L - 1:L]
    kdec = (k * jnp.exp2(b_last - b)).astype(BF16)
    eb_last = jnp.exp2(b_last)
    new_state = []
    for g, sl in enumerate(heads):
        upd = jnp.dot(jnp.transpose(v[:, sl]).astype(BF16), kdec[:, sl],
                      preferred_element_type=F32)
        new_state.append(state[g] * eb_last[:, sl] + upd)

    for i in range(1, L // SUB):
        r0 = i * SUB
        far = (lax.broadcasted_iota(jnp.int32, (SUB, r0), 1)
               <= lax.broadcasted_iota(jnp.int32, (SUB, r0), 0) + (r0 - SUB))
        for g, sl in enumerate(heads):
            s = jnp.where(far, s_far[i, g], 0.0).astype(BF16)
            oi = jnp.dot(s, v_bf[0:r0, sl], preferred_element_type=F32)
            o_blocks[g][i] = o_blocks[g][i] + oi

    near = []
    for g, sl in enumerate(heads):
        qg, bg = q[:, sl], b[:, sl]
        e = [(qg * jnp.exp2(bg - cb_ref[ch, g, SUB - d:SUB - d + L, :])).astype(BF16)
             for d in range(SUB)]
        near.append(jnp.dot(jnp.concatenate(e, axis=1), sel_ref[...],
                            preferred_element_type=F32))

    for g, sl in enumerate(heads):
        s = pltpu.roll(near[g], LANES - (SUB - 1), axis=1, stride=1, stride_axis=0)
        o = jnp.concatenate(o_blocks[g], axis=0) + jnp.dot(
            s[:, 0:L].astype(BF16), v_bf[:, sl], preferred_element_type=F32)
        ms = jnp.mean(o * o, axis=-1, keepdims=True)
        y = o * lax.rsqrt(ms + EPS) * gain_ref[:, sl] * g_ref[0, rows, sl].astype(F32)
        out_ref[0, rows, sl] = y.astype(BF16)
    return new_state


def _hgrn(proj3, logf3, gain, cast_srcs, *, d_a):
    bsz, t, _ = proj3.shape
    n_g = d_a // LANES
    rows = HG_CHUNKS * HG_L
    grid = (bsz, t // rows)
    cast_specs = [_cast_spec(a.shape, grid) for a in cast_srcs]

    def sec(s):
        return pl.BlockSpec((1, rows, d_a), lambda b, c, s=s: (b, c, s))

    return pl.pallas_call(
        functools.partial(_hgrn_kernel, n_cast=len(cast_srcs)),
        out_shape=(jax.ShapeDtypeStruct((bsz, t, d_a), BF16),
                   *[jax.ShapeDtypeStruct(a.shape, BF16) for a in cast_srcs]),
        grid=grid,
        in_specs=[sec(0), sec(1), sec(2), sec(3),
                  pl.BlockSpec((1, rows, LOGF_PIECES * d_a), lambda b, c: (b, c, 0)),
                  pl.BlockSpec((1, d_a), lambda b, c: (0, 0)),
                  *cast_specs],
        out_specs=(pl.BlockSpec((1, rows, d_a), lambda b, c: (b, c, 0)), *cast_specs),
        scratch_shapes=[pltpu.VMEM((n_g, LANES, LANES), F32),
                        pltpu.VMEM((HG_CHUNKS, n_g, HG_L + HG_SUB, LANES), F32),
                        pltpu.VMEM((HG_SUB * LANES, LANES), BF16),
                        pltpu.VMEM((HG_L, HG_L), BF16)],
        compiler_params=pltpu.CompilerParams(
            dimension_semantics=("arbitrary", "arbitrary")),
        name="hgrn",
    )(proj3, proj3, proj3, proj3, logf3, gain, *cast_srcs)


def _attn_kernel(q_ref, k_ref, v_ref, gvec_ref, *rest, n_cast):
    cast_in = rest[:n_cast]
    out_ref = rest[n_cast]
    cast_out = rest[n_cast + 1:2 * n_cast + 1]
    band_ref, bias_ref, vx_ref = rest[-3:]
    _cast_blocks(cast_in, cast_out)
    t = q_ref.shape[1]
    nq = AT_Q // CHUNK
    nk = nq + N_PAST_CHUNKS
    n_tb = 3
    wb = (nk + nq - 1) * CHUNK

    @pl.when(pl.program_id(1) == 0)
    def _build_bias():
        grow = gvec_ref[0] * LOG2E
        xb = jnp.broadcast_to(grow, (CHUNK, 2 * LANES))
        r = lax.broadcasted_iota(jnp.int32, (CHUNK, 2 * LANES), 0)
        for bit in range(6):
            xb = jnp.where(((r >> bit) & 1) == 1, pltpu.roll(xb, 1 << bit, axis=1), xb)
        const = grow[:, 0:1]

        n_const = nk - nq + 1 - n_tb
        lo = (nq - 1) * CHUNK
        band_ref[:, 0:lo] = jnp.full((CHUNK, lo), MASK_VALUE, F32)
        band_ref[:, lo:lo + n_const * CHUNK] = jnp.broadcast_to(const, (CHUNK, n_const * CHUNK))
        band_ref[:, lo + n_const * CHUNK:lo + (n_const + n_tb) * CHUNK] = xb[:, CHUNK:]
        band_ref[:, wb - lo:wb] = jnp.full((CHUNK, lo), MASK_VALUE, F32)
        for qi in range(nq):
            off = (nq - 1 - qi) * CHUNK
            bias_ref[qi * CHUNK:(qi + 1) * CHUNK, :] = band_ref[:, off:off + nk * CHUNK]

    vx_ref[:, 0:LANES] = v_ref[0]
    vx_ref[:, LANES:2 * LANES] = jnp.ones((t, LANES), BF16)

    def window(g):
        q0 = g * AT_Q
        ks = max(0, q0 - N_PAST_CHUNKS * CHUNK)
        return q0, ks, q0 + AT_Q - ks

    def scores(g):
        q0, ks, kw = window(g)
        return lax.dot_general(q_ref[0, q0:q0 + AT_Q, :], k_ref[0, ks:ks + kw, :], NT_DIMS,
                               preferred_element_type=F32)

    n_groups = t // AT_Q
    s_next = scores(0)
    for g in range(n_groups):
        q0, ks, kw = window(g)
        s = s_next + bias_ref[:, nk * CHUNK - kw:nk * CHUNK]
        if g + 1 < n_groups:
            s_next = scores(g + 1)
        p = jnp.exp2(s - jnp.max(s, axis=-1, keepdims=True))
        ox = jnp.dot(p.astype(BF16), vx_ref[ks:ks + kw, :], preferred_element_type=F32)
        out_ref[0, q0:q0 + AT_Q, :] = (ox[:, 0:LANES] / ox[:, LANES:2 * LANES]).astype(BF16)


def _attn(proj3, gvec, cast_srcs, *, col0, n_heads):
    bsz, t, _ = proj3.shape
    d_b = n_heads * LANES
    nq = AT_Q // CHUNK
    nk = nq + N_PAST_CHUNKS
    blk0 = col0 // LANES
    grid = (n_heads, bsz)
    cast_specs = [_cast_spec(a.shape, grid) for a in cast_srcs]

    def sec(s):
        return pl.BlockSpec((1, t, LANES), lambda h, b, s=s: (b, 0, blk0 + s * n_heads + h))

    return pl.pallas_call(
        functools.partial(_attn_kernel, n_cast=len(cast_srcs)),
        out_shape=(jax.ShapeDtypeStruct((bsz, t, d_b), BF16),
                   *[jax.ShapeDtypeStruct(a.shape, BF16) for a in cast_srcs]),
        grid=grid,
        in_specs=[sec(0), sec(1), sec(2),
                  pl.BlockSpec((1, 1, 2 * LANES), lambda h, b: (h, 0, 0)),
                  *cast_specs],
        out_specs=(pl.BlockSpec((1, t, LANES), lambda h, b: (b, 0, h)), *cast_specs),
        scratch_shapes=[pltpu.VMEM((CHUNK, (nk + nq - 1) * CHUNK), F32),
                        pltpu.VMEM((AT_Q, nk * CHUNK), F32),
                        pltpu.VMEM((t, 2 * LANES), BF16)],
        compiler_params=pltpu.CompilerParams(
            dimension_semantics=("arbitrary", "arbitrary")),
        name="attn",
    )(proj3, proj3, proj3, gvec, *cast_srcs)


def _merge_kernel(ya_ref, yb_ref, ga0_ref, ga1_ref, gb0_ref, gb1_ref, x_ref,
                  wa_ref, wb_ref, wo_ref, gain_ref, x1_ref, h2_ref):
    pa = jnp.dot(ya_ref[...], wa_ref[...], preferred_element_type=F32)
    pb = jnp.dot(yb_ref[...], wb_ref[...], preferred_element_type=F32)
    ga = jnp.concatenate([ga0_ref[...], ga1_ref[...]], axis=1).astype(F32)
    gb = jnp.concatenate([gb0_ref[...], gb1_ref[...]], axis=1).astype(F32)
    merged = (ga * pa + gb * pb).astype(BF16)
    x1 = x_ref[...] + jnp.dot(merged, wo_ref[...], preferred_element_type=F32)
    x1_ref[...] = x1
    ms = jnp.mean(x1 * x1, axis=-1, keepdims=True)
    h2_ref[...] = (x1 * lax.rsqrt(ms + EPS) * gain_ref[...]).astype(BF16)


def _merge(ya, yb, proj, x2, wa, wb, wo, gain, *, gate_col0):
    m, d = x2.shape
    da, db = ya.shape[1], yb.shape[1]
    tm = MG_TM
    gblk = gate_col0 // IN_TN

    def gate(s):
        return pl.BlockSpec((tm, IN_TN), lambda i, s=s: (i, gblk + s))

    def whole(a):
        return pl.BlockSpec(a.shape, lambda i: (0, 0), pipeline_mode=pl.Buffered(1))

    return pl.pallas_call(
        _merge_kernel,
        out_shape=(jax.ShapeDtypeStruct((m, d), F32),
                   jax.ShapeDtypeStruct((m, d), BF16)),
        grid=(m // tm,),
        in_specs=[pl.BlockSpec((tm, da), lambda i: (i, 0)),
                  pl.BlockSpec((tm, db), lambda i: (i, 0)),
                  gate(0), gate(1), gate(2), gate(3),
                  pl.BlockSpec((tm, d), lambda i: (i, 0)),
                  whole(wa), whole(wb), whole(wo), whole(gain)],
        out_specs=(pl.BlockSpec((tm, d), lambda i: (i, 0)),
                   pl.BlockSpec((tm, d), lambda i: (i, 0))),
        compiler_params=pltpu.CompilerParams(
            dimension_semantics=("arbitrary",),
            vmem_limit_bytes=VMEM_LIMIT_BYTES),
        name="merge",
    )(ya, yb, proj, proj, proj, proj, x2, wa, wb, wo, gain)


def _ffn_kernel(h_ref, wg_ref, wu_ref, wd_ref, x1_hbm, out_ref, x1_ref, sem):
    i, f = pl.program_id(0), pl.program_id(1)
    tm = out_ref.shape[0]

    def residual_copy():
        return pltpu.make_async_copy(x1_hbm.at[pl.ds(i * tm, tm), :], x1_ref, sem)

    def hidden_tile():
        h = h_ref[...]
        gate = jnp.dot(h, wg_ref[...], preferred_element_type=F32)
        up = jnp.dot(h, wu_ref[...], preferred_element_type=F32)
        act = (gate * _sigmoid(gate) * up).astype(BF16)
        return jnp.dot(act, wd_ref[...], preferred_element_type=F32)

    last = pl.num_programs(1) - 1

    @pl.when(f == 0)
    def _first():
        residual_copy().start()
        out_ref[...] = hidden_tile()

    @pl.when((f > 0) & (f < last))
    def _middle():
        out_ref[...] += hidden_tile()

    @pl.when(f == last)
    def _last():
        residual_copy().wait()
        out_ref[...] += hidden_tile() + x1_ref[...]


def _ffn(h2, x1, w_in, w_out):
    m, d = h2.shape
    dff = w_out.shape[0]
    tm, tf = FF_TM, FF_TF
    nf = dff // tf
    return pl.pallas_call(
        _ffn_kernel,
        out_shape=jax.ShapeDtypeStruct((m, d), F32),
        grid=(m // tm, nf),
        in_specs=[pl.BlockSpec((tm, d), lambda i, f: (i, 0)),
                  pl.BlockSpec((d, tf), lambda i, f: (0, f)),
                  pl.BlockSpec((d, tf), lambda i, f: (0, nf + f)),
                  pl.BlockSpec((tf, d), lambda i, f: (f, 0)),
                  pl.BlockSpec(memory_space=pl.ANY)],
        out_specs=pl.BlockSpec((tm, d), lambda i, f: (i, 0)),
        scratch_shapes=[pltpu.VMEM((tm, d), F32), pltpu.SemaphoreType.DMA(())],
        compiler_params=pltpu.CompilerParams(
            dimension_semantics=("arbitrary", "arbitrary"),
            vmem_limit_bytes=VMEM_LIMIT_BYTES),
        name="ffn",
    )(h2, w_in, w_in, w_out, x1)


def _bias_vector(rel_bias):
    rev = rel_bias[:, ::-1]
    pad = 2 * LANES - N_REL
    return jnp.pad(rev, ((0, 0), (pad, 0)), mode="edge")[:, None, :]


def kernel(x, w_in, b_gate, norm_mix, norm_ffn, hgrn_lb_logits, hgrn_out_gain,
           q_gain, k_gain, rel_bias, w_proj_a, w_proj_b, w_out, w_ffn_in, w_ffn_out):
    bsz, t, d = x.shape
    depth = w_in.shape[0]
    d_a = hgrn_out_gain.shape[1]
    dh = q_gain.shape[1]
    n_heads_b = rel_bias.shape[1]
    d_b = n_heads_b * dh
    n_in = w_in.shape[2]
    gate_col0 = 4 * d_a + 3 * d_b
    assert dh == LANES and d_a == IN_TN and d_b == IN_TN and depth == 1
    assert hgrn_lb_logits.shape[0] == depth + 1
    assert n_in == gate_col0 + 2 * d and t % AT_Q == 0

    m = bsz * t
    x2 = x.reshape(m, d)
    for l in range(depth):
        proj, logf = _in_proj(
            x2, norm_mix[l][None, :], w_in[l], b_gate[l][None, :], hgrn_lb_logits,
            q_gain[l][None, :], k_gain[l][None, :], (), scale=dh ** -0.5 * LOG2E)
        proj3 = proj.reshape(bsz, t, n_in)
        y_a, w_ffn_in_bf, w_ffn_out_bf = _hgrn(
            proj3, logf.reshape(bsz, t, LOGF_PIECES * d_a), hgrn_out_gain[l][None, :],
            (w_ffn_in[l], w_ffn_out[l]), d_a=d_a)
        y_b, wa_bf, wb_bf, wo_bf = _attn(
            proj3, _bias_vector(rel_bias[l]), (w_proj_a[l], w_proj_b[l], w_out[l]),
            col0=4 * d_a, n_heads=n_heads_b)
        x1, h2 = _merge(y_a.reshape(m, d_a), y_b.reshape(m, d_b), proj, x2,
                        wa_bf, wb_bf, wo_bf, norm_ffn[l][None, :], gate_col0=gate_col0)
        x2 = _ffn(h2, x1, w_ffn_in_bf, w_ffn_out_bf)
    return x2.reshape(bsz, t, d)
```

```python
import functools

import jax
import jax.numpy as jnp
from jax import lax
from jax.experimental import pallas as pl
from jax.experimental.pallas import tpu as pltpu

F32 = jnp.float32
BF16 = jnp.bfloat16

EPS = 1e-6
LANES = 128
BF16_SUBLANES = 16
VMEM_LIMIT_BYTES = 56 * 1024 * 1024
CHUNK = 64
N_PAST_CHUNKS = 8
REL_FUTURE = CHUNK - 1
REL_PAST = 2 * CHUNK - 1
N_REL = REL_FUTURE + REL_PAST + 1
MASK_VALUE = -1e30
BIG_EXPONENT = 1e30
LOGF_PIECES = 3
LOG2E = 1.4426950408889634

IN_TM = 1024
IN_TN = 1024
IN_SUB = 256
IN_FIRST_GATE_SECTION = 7
HG_L = 128
HG_CHUNKS = 4
HG_SUB = 16
AT_Q = 4 * CHUNK
MG_TM = 512
FF_TM = 1024
FF_TF = 512

NT_DIMS = (((1,), (1,)), ((), ()))


def _sigmoid(x):
    return 0.5 * jnp.tanh(0.5 * x) + 0.5


def _cast_spec(shape, grid):
    rows, cols = shape
    steps = 1
    for g in grid:
        steps *= g
    if rows % (steps * BF16_SUBLANES) == 0:
        strides = [steps // g0 for g0 in _running_products(grid)]

        def index_map(*ids):
            return (sum(i * s for i, s in zip(ids, strides)), 0)
        return pl.BlockSpec((rows // steps, cols), index_map)
    g0, g1 = grid
    assert rows % (g0 * BF16_SUBLANES) == 0 and cols % (g1 * LANES) == 0, (shape, grid)
    return pl.BlockSpec((rows // g0, cols // g1), lambda i, j: (i, j))


def _running_products(grid):
    out, p = [], 1
    for g in grid:
        p *= g
        out.append(p)
    return out


def _cast_blocks(src_refs, dst_refs):
    for src, dst in zip(src_refs, dst_refs):
        dst[...] = src[...].astype(BF16)


def _in_proj_kernel(x_hbm, gain_ref, w_ref, bias_ref, lbl_ref, qg_ref, kg_ref, *rest,
                    scale, n_cast):
    cast_in = rest[:n_cast]
    out_ref, logf_ref = rest[n_cast:n_cast + 2]
    cast_out = rest[n_cast + 2:2 * n_cast + 2]
    h_ref, x_ref, x_sem = rest[-3:]
    i, j = pl.program_id(0), pl.program_id(1)
    tm = x_ref.shape[0]

    def x_copy(tile):
        return pltpu.make_async_copy(x_hbm.at[pl.ds(tile * tm, tm), :], x_ref, x_sem)

    @pl.when((j == 0) & (i == 0))
    def _first_fetch():
        x_copy(0).start()

    @pl.when(j == 0)
    def _norm():
        x_copy(i).wait()
        x = x_ref[...]
        ms = jnp.mean(x * x, axis=-1, keepdims=True)
        h_ref[...] = (x * lax.rsqrt(ms + EPS) * gain_ref[...]).astype(BF16)

    @pl.when((j == 1) & (i + 1 < pl.num_programs(0)))
    def _next_fetch():
        x_copy(i + 1).start()

    def section(epilogue):
        _cast_blocks(cast_in, cast_out)
        rs = slice(None)
        for c in range(w_ref.shape[1] // IN_SUB):
            cs = slice(c * IN_SUB, (c + 1) * IN_SUB)
            epilogue(jnp.dot(h_ref[...], w_ref[:, cs].astype(BF16),
                             preferred_element_type=F32), rs, cs)

    @pl.when((j == 0) | (j == 3))
    def _silu():
        def epilogue(acc, rs, cs):
            out_ref[rs, cs] = (acc * _sigmoid(acc)).astype(BF16)
        section(epilogue)

    @pl.when(j == 1)
    def _forget():
        l = lbl_ref[...]
        e = jnp.exp(l - jnp.max(l, axis=0, keepdims=True))
        lb_row = e[0:1] / jnp.sum(e, axis=0, keepdims=True)

        def epilogue(acc, rs, cs):
            lb = lb_row[:, cs]
            sig = _sigmoid(acc)
            lf = jnp.log2(lb + (1.0 - lb) * sig)
            hi = lf.astype(BF16)
            r1 = lf - hi.astype(F32)
            mid = r1.astype(BF16)
            tn = w_ref.shape[1]
            logf_ref[rs, cs] = hi
            logf_ref[rs, slice(tn + cs.start, tn + cs.stop)] = mid
            logf_ref[rs, slice(2 * tn + cs.start, 2 * tn + cs.stop)] = (
                r1 - mid.astype(F32)).astype(BF16)
            out_ref[rs, cs] = ((1.0 - lb) * (1.0 - sig)).astype(BF16)
        section(epilogue)

    @pl.when((j == 2) | (j == 6))
    def _plain():
        def epilogue(acc, rs, cs):
            out_ref[rs, cs] = acc.astype(BF16)
        section(epilogue)

    def head_norm(gain_ref_, mult):
        def epilogue(acc, rs, cs):
            for h in range(IN_SUB // LANES):
                t = acc[:, h * LANES:(h + 1) * LANES]
                sl = slice(cs.start + h * LANES, cs.start + (h + 1) * LANES)
                ms = jnp.mean(t * t, axis=-1, keepdims=True)
                out_ref[rs, sl] = (t * lax.rsqrt(ms + EPS) * (gain_ref_[...] * mult)).astype(BF16)
        section(epilogue)

    @pl.when(j == 4)
    def _qnorm():
        head_norm(qg_ref, scale)

    @pl.when(j == 5)
    def _knorm():
        head_norm(kg_ref, 1.0)

    @pl.when(j >= IN_FIRST_GATE_SECTION)
    def _gates():
        def epilogue(acc, rs, cs):
            out_ref[rs, cs] = _sigmoid(acc + bias_ref[:, cs]).astype(BF16)
        section(epilogue)


def _in_proj(x2, gain, w, b_gate, lb_logits, qg, kg, cast_srcs, *, scale):
    m, d = x2.shape
    n = w.shape[1]
    tm, tn = IN_TM, IN_TN
    grid = (m // tm, n // tn)
    first_gate = (n - b_gate.shape[1]) // tn
    assert first_gate == IN_FIRST_GATE_SECTION
    cast_specs = [_cast_spec(a.shape, grid) for a in cast_srcs]
    return pl.pallas_call(
        functools.partial(_in_proj_kernel, scale=scale, n_cast=len(cast_srcs)),
        out_shape=(jax.ShapeDtypeStruct((m, n), BF16),
                   jax.ShapeDtypeStruct((m, LOGF_PIECES * tn), BF16),
                   *[jax.ShapeDtypeStruct(a.shape, BF16) for a in cast_srcs]),
        grid=grid,
        in_specs=[
            pl.BlockSpec(memory_space=pl.ANY),
            pl.BlockSpec((1, d), lambda i, j: (0, 0)),
            pl.BlockSpec((d, tn), lambda i, j: (0, j)),
            pl.BlockSpec((1, tn), lambda i, j: (0, jnp.maximum(j - first_gate, 0))),
            pl.BlockSpec(lb_logits.shape, lambda i, j: (0, 0)),
            pl.BlockSpec((1, LANES), lambda i, j: (0, 0)),
            pl.BlockSpec((1, LANES), lambda i, j: (0, 0)),
            *cast_specs,
        ],
        out_specs=(pl.BlockSpec((tm, tn), lambda i, j: (i, j)),
                   pl.BlockSpec((tm, LOGF_PIECES * tn), lambda i, j: (i, 0)),
                   *cast_specs),
        scratch_shapes=[pltpu.VMEM((tm, d), BF16), pltpu.VMEM((tm, d), F32),
                        pltpu.SemaphoreType.DMA(())],
        compiler_params=pltpu.CompilerParams(
            dimension_semantics=("arbitrary", "arbitrary"),
            vmem_limit_bytes=VMEM_LIMIT_BYTES),
        name="in_proj",
    )(x2, gain, w, b_gate, lb_logits, qg, kg, *cast_srcs)


def _hgrn_kernel(q_ref, k_ref, v_ref, g_ref, lf_ref, gain_ref, *rest, n_cast):
    cast_in = rest[:n_cast]
    out_ref = rest[n_cast]
    cast_out = rest[n_cast + 1:2 * n_cast + 1]
    st_ref, cb_ref, sel_ref, tri_ref = rest[-4:]
    _cast_blocks(cast_in, cast_out)
    n_g = st_ref.shape[0]
    L, SUB = HG_L, HG_SUB

    @pl.when(pl.program_id(1) == 0)
    def _init():
        st_ref[...] = jnp.zeros_like(st_ref)
        cb_ref[...] = jnp.full(cb_ref.shape, BIG_EXPONENT, F32)
        r = lax.broadcasted_iota(jnp.int32, sel_ref.shape, 0)
        c = lax.broadcasted_iota(jnp.int32, sel_ref.shape, 1)
        sel_ref[...] = jnp.where(c == SUB - 1 - r // LANES, 1.0, 0.0).astype(BF16)
        row = lax.broadcasted_iota(jnp.int32, (L, L), 0)
        col = lax.broadcasted_iota(jnp.int32, (L, L), 1)
        tri_ref[...] = jnp.where(row >= col, 1.0, 0.0).astype(BF16)

    heads = [slice(g * LANES, (g + 1) * LANES) for g in range(n_g)]
    gw = n_g * LANES
    state = [st_ref[g] for g in range(n_g)]
    for ch in range(HG_CHUNKS):
        state = _hgrn_chunk(slice(ch * L, (ch + 1) * L), ch, state, heads, gw,
                            q_ref, k_ref, v_ref, g_ref, lf_ref, gain_ref, out_ref,
                            cb_ref, sel_ref, tri_ref)
    for g in range(n_g):
        st_ref[g] = state[g]


def _hgrn_chunk(rows, ch, state, heads, gw, q_ref, k_ref, v_ref, g_ref, lf_ref, gain_ref,
                out_ref, cb_ref, sel_ref, tri_ref):
    L, SUB = HG_L, HG_SUB
    tri = tri_ref[...]

    b = sum(jnp.dot(tri, lf_ref[0, rows, p * gw:(p + 1) * gw], preferred_element_type=F32)
            for p in range(LOGF_PIECES))
    q = q_ref[0, rows, :].astype(F32)
    k = k_ref[0, rows, :].astype(F32)
    v_bf = v_ref[0, rows, :]
    v = v_bf.astype(F32)
    c = b - jnp.log2(k)
    for g, sl in enumerate(heads):
        cb_ref[ch, g, SUB:SUB + L, :] = c[:, sl]

    qe = (q * jnp.exp2(b)).astype(BF16)
    o_blocks = [[None] * (L // SUB) for _ in heads]
    for g, sl in enumerate(heads):
        o = lax.dot_general(qe[:, sl], state[g].astype(BF16), NT_DIMS,
                            preferred_element_type=F32)
        for i in range(L // SUB):
            o_blocks[g][i] = o[i * SUB:(i + 1) * SUB]

    s_far = {}
    for i in range(1, L // SUB):
        r0 = i * SUB
        bref = b[r0:r0 + 1]
        qt = (q[r0:r0 + SUB] * jnp.exp2(b[r0:r0 + SUB] - bref)).astype(BF16)
        kt = (k[0:r0] * jnp.exp2(bref - b[0:r0])).astype(BF16)
        for g, sl in enumerate(heads):
            s_far[i, g] = lax.dot_general(qt[:, sl], kt[:, sl], NT_DIMS,
                                          preferred_element_type=F32)

    b_last = b[L - 1:L]
    kdec = (k * jnp.exp2(b_last - b)).astype(BF16)
    eb_last = jnp.exp2(b_last)
    new_state = []
    for g, sl in enumerate(heads):
        upd = jnp.dot(jnp.transpose(v[:, sl]).astype(BF16), kdec[:, sl],
                      preferred_element_type=F32)
        new_state.append(state[g] * eb_last[:, sl] + upd)

    for i in range(1, L // SUB):
        r0 = i * SUB
        far = (lax.broadcasted_iota(jnp.int32, (SUB, r0), 1)
               <= lax.broadcasted_iota(jnp.int32, (SUB, r0), 0) + (r0 - SUB))
        for g, sl in enumerate(heads):
            s = jnp.where(far, s_far[i, g], 0.0).astype(BF16)
            oi = jnp.dot(s, v_bf[0:r0, sl], preferred_element_type=F32)
            o_blocks[g][i] = o_blocks[g][i] + oi

    near = []
    for g, sl in enumerate(heads):
        qg, bg = q[:, sl], b[:, sl]
        e = [(qg * jnp.exp2(bg - cb_ref[ch, g, SUB - d:SUB - d + L, :])).astype(BF16)
             for d in range(SUB)]
        near.append(jnp.dot(jnp.concatenate(e, axis=1), sel_ref[...],
                            preferred_element_type=F32))

    for g, sl in enumerate(heads):
        s = pltpu.roll(near[g], LANES - (SUB - 1), axis=1, stride=1, stride_axis=0)
        o = jnp.concatenate(o_blocks[g], axis=0) + jnp.dot(
            s[:, 0:L].astype(BF16), v_bf[:, sl], preferred_element_type=F32)
        ms = jnp.mean(o * o, axis=-1, keepdims=True)
        y = o * lax.rsqrt(ms + EPS) * gain_ref[:, sl] * g_ref[0, rows, sl].astype(F32)
        out_ref[0, rows, sl] = y.astype(BF16)
    return new_state


def _hgrn(proj3, logf3, gain, cast_srcs, *, d_a):
    bsz, t, _ = proj3.shape
    n_g = d_a // LANES
    rows = HG_CHUNKS * HG_L
    grid = (bsz, t // rows)
    cast_specs = [_cast_spec(a.shape, grid) for a in cast_srcs]

    def sec(s):
        return pl.BlockSpec((1, rows, d_a), lambda b, c, s=s: (b, c, s))

    return pl.pallas_call(
        functools.partial(_hgrn_kernel, n_cast=len(cast_srcs)),
        out_shape=(jax.ShapeDtypeStruct((bsz, t, d_a), BF16),
                   *[jax.ShapeDtypeStruct(a.shape, BF16) for a in cast_srcs]),
        grid=grid,
        in_specs=[sec(0), sec(1), sec(2), sec(3),
                  pl.BlockSpec((1, rows, LOGF_PIECES * d_a), lambda b, c: (b, c, 0)),
                  pl.BlockSpec((1, d_a), lambda b, c: (0, 0)),
                  *cast_specs],
        out_specs=(pl.BlockSpec((1, rows, d_a), lambda b, c: (b, c, 0)), *cast_specs),
        scratch_shapes=[pltpu.VMEM((n_g, LANES, LANES), F32),
                        pltpu.VMEM((HG_CHUNKS, n_g, HG_L + HG_SUB, LANES), F32),
                        pltpu.VMEM((HG_SUB * LANES, LANES), BF16),
                        pltpu.VMEM((HG_L, HG_L), BF16)],
        compiler_params=pltpu.CompilerParams(
            dimension_semantics=("arbitrary", "arbitrary")),
        name="hgrn",
    )(proj3, proj3, proj3, proj3, logf3, gain, *cast_srcs)


def _attn_kernel(q_ref, k_ref, v_ref, gvec_ref, *rest, n_cast):
    cast_in = rest[:n_cast]
    out_ref = rest[n_cast]
    cast_out = rest[n_cast + 1:2 * n_cast + 1]
    band_ref, bias_ref, vx_ref = rest[-3:]
    _cast_blocks(cast_in, cast_out)
    t = q_ref.shape[1]
    nq = AT_Q // CHUNK
    nk = nq + N_PAST_CHUNKS
    n_tb = 3
    wb = (nk + nq - 1) * CHUNK

    @pl.when(pl.program_id(1) == 0)
    def _build_bias():
        grow = gvec_ref[0] * LOG2E
        xb = jnp.broadcast_to(grow, (CHUNK, 2 * LANES))
        r = lax.broadcasted_iota(jnp.int32, (CHUNK, 2 * LANES), 0)
        for bit in range(6):
            xb = jnp.where(((r >> bit) & 1) == 1, pltpu.roll(xb, 1 << bit, axis=1), xb)
        const = grow[:, 0:1]

        n_const = nk - nq + 1 - n_tb
        lo = (nq - 1) * CHUNK
        band_ref[:, 0:lo] = jnp.full((CHUNK, lo), MASK_VALUE, F32)
        band_ref[:, lo:lo + n_const * CHUNK] = jnp.broadcast_to(const, (CHUNK, n_const * CHUNK))
        band_ref[:, lo + n_const * CHUNK:lo + (n_const + n_tb) * CHUNK] = xb[:, CHUNK:]
        band_ref[:, wb - lo:wb] = jnp.full((CHUNK, lo), MASK_VALUE, F32)
        for qi in range(nq):
            off = (nq - 1 - qi) * CHUNK
            bias_ref[qi * CHUNK:(qi + 1) * CHUNK, :] = band_ref[:, off:off + nk * CHUNK]

    vx_ref[:, 0:LANES] = v_ref[0]
    vx_ref[:, LANES:2 * LANES] = jnp.ones((t, LANES), BF16)

    def window(g):
        q0 = g * AT_Q
        ks = max(0, q0 - N_PAST_CHUNKS * CHUNK)
        return q0, ks, q0 + AT_Q - ks

    def scores(g):
        q0, ks, kw = window(g)
        return lax.dot_general(q_ref[0, q0:q0 + AT_Q, :], k_ref[0, ks:ks + kw, :], NT_DIMS,
                               preferred_element_type=F32)

    n_groups = t // AT_Q
    s_next = scores(0)
    for g in range(n_groups):
        q0, ks, kw = window(g)
        s = s_next + bias_ref[:, nk * CHUNK - kw:nk * CHUNK]
        if g + 1 < n_groups:
            s_next = scores(g + 1)
        p = jnp.exp2(s - jnp.max(s, axis=-1, keepdims=True))
        ox = jnp.dot(p.astype(BF16), vx_ref[ks:ks + kw, :], preferred_element_type=F32)
        out_ref[0, q0:q0 + AT_Q, :] = (ox[:, 0:LANES] / ox[:, LANES:2 * LANES]).astype(BF16)


def _attn(proj3, gvec, cast_srcs, *, col0, n_heads):
    bsz, t, _ = proj3.shape
    d_b = n_heads * LANES
    nq = AT_Q // CHUNK
    nk = nq + N_PAST_CHUNKS
    blk0 = col0 // LANES
    grid = (n_heads, bsz)
    cast_specs = [_cast_spec(a.shape, grid) for a in cast_srcs]

    def sec(s):
        return pl.BlockSpec((1, t, LANES), lambda h, b, s=s: (b, 0, blk0 + s * n_heads + h))

    return pl.pallas_call(
        functools.partial(_attn_kernel, n_cast=len(cast_srcs)),
        out_shape=(jax.ShapeDtypeStruct((bsz, t, d_b), BF16),
                   *[jax.ShapeDtypeStruct(a.shape, BF16) for a in cast_srcs]),
        grid=grid,
        in_specs=[sec(0), sec(1), sec(2),
                  pl.BlockSpec((1, 1, 2 * LANES), lambda h, b: (h, 0, 0)),
                  *cast_specs],
        out_specs=(pl.BlockSpec((1, t, LANES), lambda h, b: (b, 0, h)), *cast_specs),
        scratch_shapes=[pltpu.VMEM((CHUNK, (nk + nq - 1) * CHUNK), F32),
                        pltpu.VMEM((AT_Q, nk * CHUNK), F32),
                        pltpu.VMEM((t, 2 * LANES), BF16)],
        compiler_params=pltpu.CompilerParams(
            dimension_semantics=("arbitrary", "arbitrary")),
        name="attn",
    )(proj3, proj3, proj3, gvec, *cast_srcs)


def _merge_kernel(ya_ref, yb_ref, ga0_ref, ga1_ref, gb0_ref, gb1_ref, x_ref,
                  wa_ref, wb_ref, wo_ref, gain_ref, x1_ref, h2_ref):
    pa = jnp.dot(ya_ref[...], wa_ref[...], preferred_element_type=F32)
    pb = jnp.dot(yb_ref[...], wb_ref[...], preferred_element_type=F32)
    ga = jnp.concatenate([ga0_ref[...], ga1_ref[...]], axis=1).astype(F32)
    gb = jnp.concatenate([gb0_ref[...], gb1_ref[...]], axis=1).astype(F32)
    merged = (ga * pa + gb * pb).astype(BF16)
    x1 = x_ref[...] + jnp.dot(merged, wo_ref[...], preferred_element_type=F32)
    x1_ref[...] = x1
    ms = jnp.mean(x1 * x1, axis=-1, keepdims=True)
    h2_ref[...] = (x1 * lax.rsqrt(ms + EPS) * gain_ref[...]).astype(BF16)


def _merge(ya, yb, proj, x2, wa, wb, wo, gain, *, gate_col0):
    m, d = x2.shape
    da, db = ya.shape[1], yb.shape[1]
    tm = MG_TM
    gblk = gate_col0 // IN_TN

    def gate(s):
        return pl.BlockSpec((tm, IN_TN), lambda i, s=s: (i, gblk + s))

    def whole(a):
        return pl.BlockSpec(a.shape, lambda i: (0, 0), pipeline_mode=pl.Buffered(1))

    return pl.pallas_call(
        _merge_kernel,
        out_shape=(jax.ShapeDtypeStruct((m, d), F32),
                   jax.ShapeDtypeStruct((m, d), BF16)),
        grid=(m // tm,),
        in_specs=[pl.BlockSpec((tm, da), lambda i: (i, 0)),
                  pl.BlockSpec((tm, db), lambda i: (i, 0)),
                  gate(0), gate(1), gate(2), gate(3),
                  pl.BlockSpec((tm, d), lambda i: (i, 0)),
                  whole(wa), whole(wb), whole(wo), whole(gain)],
        out_specs=(pl.BlockSpec((tm, d), lambda i: (i, 0)),
                   pl.BlockSpec((tm, d), lambda i: (i, 0))),
        compiler_params=pltpu.CompilerParams(
            dimension_semantics=("arbitrary",),
            vmem_limit_bytes=VMEM_LIMIT_BYTES),
        name="merge",
    )(ya, yb, proj, proj, proj, proj, x2, wa, wb, wo, gain)


def _ffn_kernel(h_ref, wg_ref, wu_ref, wd_ref, x1_hbm, out_ref, x1_ref, sem):
    i, f = pl.program_id(0), pl.program_id(1)
    tm = out_ref.shape[0]

    def residual_copy():
        return pltpu.make_async_copy(x1_hbm.at[pl.ds(i * tm, tm), :], x1_ref, sem)

    def hidden_tile():
        h = h_ref[...]
        gate = jnp.dot(h, wg_ref[...], preferred_element_type=F32)
        up = jnp.dot(h, wu_ref[...], preferred_element_type=F32)
        act = (gate * _sigmoid(gate) * up).astype(BF16)
        return jnp.dot(act, wd_ref[...], preferred_element_type=F32)

    last = pl.num_programs(1) - 1

    @pl.when(f == 0)
    def _first():
        residual_copy().start()
        out_ref[...] = hidden_tile()

    @pl.when((f > 0) & (f < last))
    def _middle():
        out_ref[...] += hidden_tile()

    @pl.when(f == last)
    def _last():
        residual_copy().wait()
        out_ref[...] += hidden_tile() + x1_ref[...]


def _ffn(h2, x1, w_in, w_out):
    m, d = h2.shape
    dff = w_out.shape[0]
    tm, tf = FF_TM, FF_TF
    nf = dff // tf
    return pl.pallas_call(
        _ffn_kernel,
        out_shape=jax.ShapeDtypeStruct((m, d), F32),
        grid=(m // tm, nf),
        in_specs=[pl.BlockSpec((tm, d), lambda i, f: (i, 0)),
                  pl.BlockSpec((d, tf), lambda i, f: (0, f)),
                  pl.BlockSpec((d, tf), lambda i, f: (0, nf + f)),
                  pl.BlockSpec((tf, d), lambda i, f: (f, 0)),
                  pl.BlockSpec(memory_space=pl.ANY)],
        out_specs=pl.BlockSpec((tm, d), lambda i, f: (i, 0)),
        scratch_shapes=[pltpu.VMEM((tm, d), F32), pltpu.SemaphoreType.DMA(())],
        compiler_params=pltpu.CompilerParams(
            dimension_semantics=("arbitrary", "arbitrary"),
            vmem_limit_bytes=VMEM_LIMIT_BYTES),
        name="ffn",
    )(h2, w_in, w_in, w_out, x1)


def _bias_vector(rel_bias):
    rev = rel_bias[:, ::-1]
    pad = 2 * LANES - N_REL
    return jnp.pad(rev, ((0, 0), (pad, 0)), mode="edge")[:, None, :]


def kernel(x, w_in, b_gate, norm_mix, norm_ffn, hgrn_lb_logits, hgrn_out_gain,
           q_gain, k_gain, rel_bias, w_proj_a, w_proj_b, w_out, w_ffn_in, w_ffn_out):
    bsz, t, d = x.shape
    depth = w_in.shape[0]
    d_a = hgrn_out_gain.shape[1]
    dh = q_gain.shape[1]
    n_heads_b = rel_bias.shape[1]
    d_b = n_heads_b * dh
    n_in = w_in.shape[2]
    gate_col0 = 4 * d_a + 3 * d_b
    assert dh == LANES and d_a == IN_TN and d_b == IN_TN and depth == 1
    assert hgrn_lb_logits.shape[0] == depth + 1
    assert n_in == gate_col0 + 2 * d and t % AT_Q == 0

    m = bsz * t
    x2 = x.reshape(m, d)
    for l in range(depth):
        proj, logf = _in_proj(
            x2, norm_mix[l][None, :], w_in[l], b_gate[l][None, :], hgrn_lb_logits,
            q_gain[l][None, :], k_gain[l][None, :], (), scale=dh ** -0.5 * LOG2E)
        proj3 = proj.reshape(bsz, t, n_in)
        y_a, w_ffn_in_bf, w_ffn_out_bf = _hgrn(
            proj3, logf.reshape(bsz, t, LOGF_PIECES * d_a), hgrn_out_gain[l][None, :],
            (w_ffn_in[l], w_ffn_out[l]), d_a=d_a)
        y_b, wa_bf, wb_bf, wo_bf = _attn(
            proj3, _bias_vector(rel_bias[l]), (w_proj_a[l], w_proj_b[l], w_out[l]),
            col0=4 * d_a, n_heads=n_heads_b)
        x1, h2 = _merge(y_a.reshape(m, d_a), y_b.reshape(m, d_b), proj, x2,
                        wa_bf, wb_bf, wo_bf, norm_ffn[l][None, :], gate_col0=gate_col0)
        x2 = _ffn(h2, x1, w_ffn_in_bf, w_ffn_out_bf)
    return x2.reshape(bsz, t, d)
```

```python
import functools

import jax
import jax.numpy as jnp
from jax import lax
from jax.experimental import pallas as pl
from jax.experimental.pallas import tpu as pltpu

F32 = jnp.float32
BF16 = jnp.bfloat16

EPS = 1e-6
LANES = 128
BF16_SUBLANES = 16
VMEM_LIMIT_BYTES = 56 * 1024 * 1024
CHUNK = 64
N_PAST_CHUNKS = 8
REL_FUTURE = CHUNK - 1
REL_PAST = 2 * CHUNK - 1
N_REL = REL_FUTURE + REL_PAST + 1
MASK_VALUE = -1e30
BIG_EXPONENT = 1e30
LOGF_PIECES = 3
LOG2E = 1.4426950408889634

IN_TM = 1024
IN_TN = 1024
IN_SUB = 256
IN_FIRST_GATE_SECTION = 7
HG_L = 128
HG_CHUNKS = 2
HG_FAST_RANGE = 64.0
HG_SUB = 16
AT_Q = 4 * CHUNK
MG_TM = 512
FF_TM = 1024
FF_TF = 512

NT_DIMS = (((1,), (1,)), ((), ()))


def _sigmoid(x):
    return 0.5 * jnp.tanh(0.5 * x) + 0.5


def _cast_spec(shape, grid):
    rows, cols = shape
    steps = 1
    for g in grid:
        steps *= g
    if rows % (steps * BF16_SUBLANES) == 0:
        strides = [steps // g0 for g0 in _running_products(grid)]

        def index_map(*ids):
            return (sum(i * s for i, s in zip(ids, strides)), 0)
        return pl.BlockSpec((rows // steps, cols), index_map)
    g0, g1 = grid
    assert rows % (g0 * BF16_SUBLANES) == 0 and cols % (g1 * LANES) == 0, (shape, grid)
    return pl.BlockSpec((rows // g0, cols // g1), lambda i, j: (i, j))


def _running_products(grid):
    out, p = [], 1
    for g in grid:
        p *= g
        out.append(p)
    return out


def _cast_blocks(src_refs, dst_refs):
    for src, dst in zip(src_refs, dst_refs):
        dst[...] = src[...].astype(BF16)


def _in_proj_kernel(x_hbm, gain_ref, w_ref, bias_ref, lbl_ref, qg_ref, kg_ref, *rest,
                    scale, n_cast):
    cast_in = rest[:n_cast]
    out_ref, logf_ref = rest[n_cast:n_cast + 2]
    cast_out = rest[n_cast + 2:2 * n_cast + 2]
    h_ref, x_ref, x_sem = rest[-3:]
    i, j = pl.program_id(0), pl.program_id(1)
    tm = x_ref.shape[0]

    def x_copy(tile):
        return pltpu.make_async_copy(x_hbm.at[pl.ds(tile * tm, tm), :], x_ref, x_sem)

    @pl.when((j == 0) & (i == 0))
    def _first_fetch():
        x_copy(0).start()

    @pl.when(j == 0)
    def _norm():
        x_copy(i).wait()
        x = x_ref[...]
        ms = jnp.mean(x * x, axis=-1, keepdims=True)
        h_ref[...] = (x * lax.rsqrt(ms + EPS) * gain_ref[...]).astype(BF16)

    @pl.when((j == 1) & (i + 1 < pl.num_programs(0)))
    def _next_fetch():
        x_copy(i + 1).start()

    def section(epilogue):
        _cast_blocks(cast_in, cast_out)
        rs = slice(None)
        for c in range(w_ref.shape[1] // IN_SUB):
            cs = slice(c * IN_SUB, (c + 1) * IN_SUB)
            epilogue(jnp.dot(h_ref[...], w_ref[:, cs].astype(BF16),
                             preferred_element_type=F32), rs, cs)

    @pl.when((j == 0) | (j == 3))
    def _silu():
        def epilogue(acc, rs, cs):
            out_ref[rs, cs] = (acc * _sigmoid(acc)).astype(BF16)
        section(epilogue)

    @pl.when(j == 1)
    def _forget():
        l = lbl_ref[...]
        e = jnp.exp(l - jnp.max(l, axis=0, keepdims=True))
        lb_row = e[0:1] / jnp.sum(e, axis=0, keepdims=True)

        def epilogue(acc, rs, cs):
            lb = lb_row[:, cs]
            sig = _sigmoid(acc)
            lf = jnp.log2(lb + (1.0 - lb) * sig)
            hi = lf.astype(BF16)
            r1 = lf - hi.astype(F32)
            mid = r1.astype(BF16)
            tn = w_ref.shape[1]
            logf_ref[rs, cs] = hi
            logf_ref[rs, slice(tn + cs.start, tn + cs.stop)] = mid
            logf_ref[rs, slice(2 * tn + cs.start, 2 * tn + cs.stop)] = (
                r1 - mid.astype(F32)).astype(BF16)
            out_ref[rs, cs] = ((1.0 - lb) * (1.0 - sig)).astype(BF16)
        section(epilogue)

    @pl.when((j == 2) | (j == 6))
    def _plain():
        def epilogue(acc, rs, cs):
            out_ref[rs, cs] = acc.astype(BF16)
        section(epilogue)

    def head_norm(gain_ref_, mult):
        def epilogue(acc, rs, cs):
            for h in range(IN_SUB // LANES):
                t = acc[:, h * LANES:(h + 1) * LANES]
                sl = slice(cs.start + h * LANES, cs.start + (h + 1) * LANES)
                ms = jnp.mean(t * t, axis=-1, keepdims=True)
                out_ref[rs, sl] = (t * lax.rsqrt(ms + EPS) * (gain_ref_[...] * mult)).astype(BF16)
        section(epilogue)

    @pl.when(j == 4)
    def _qnorm():
        head_norm(qg_ref, scale)

    @pl.when(j == 5)
    def _knorm():
        head_norm(kg_ref, 1.0)

    @pl.when(j >= IN_FIRST_GATE_SECTION)
    def _gates():
        def epilogue(acc, rs, cs):
            out_ref[rs, cs] = _sigmoid(acc + bias_ref[:, cs]).astype(BF16)
        section(epilogue)


def _in_proj(x2, gain, w, b_gate, lb_logits, qg, kg, cast_srcs, *, scale):
    m, d = x2.shape
    n = w.shape[1]
    tm, tn = IN_TM, IN_TN
    grid = (m // tm, n // tn)
    first_gate = (n - b_gate.shape[1]) // tn
    assert first_gate == IN_FIRST_GATE_SECTION
    cast_specs = [_cast_spec(a.shape, grid) for a in cast_srcs]
    return pl.pallas_call(
        functools.partial(_in_proj_kernel, scale=scale, n_cast=len(cast_srcs)),
        out_shape=(jax.ShapeDtypeStruct((m, n), BF16),
                   jax.ShapeDtypeStruct((m, LOGF_PIECES * tn), BF16),
                   *[jax.ShapeDtypeStruct(a.shape, BF16) for a in cast_srcs]),
        grid=grid,
        in_specs=[
            pl.BlockSpec(memory_space=pl.ANY),
            pl.BlockSpec((1, d), lambda i, j: (0, 0)),
            pl.BlockSpec((d, tn), lambda i, j: (0, j)),
            pl.BlockSpec((1, tn), lambda i, j: (0, jnp.maximum(j - first_gate, 0))),
            pl.BlockSpec(lb_logits.shape, lambda i, j: (0, 0)),
            pl.BlockSpec((1, LANES), lambda i, j: (0, 0)),
            pl.BlockSpec((1, LANES), lambda i, j: (0, 0)),
            *cast_specs,
        ],
        out_specs=(pl.BlockSpec((tm, tn), lambda i, j: (i, j)),
                   pl.BlockSpec((tm, LOGF_PIECES * tn), lambda i, j: (i, 0)),
                   *cast_specs),
        scratch_shapes=[pltpu.VMEM((tm, d), BF16), pltpu.VMEM((tm, d), F32),
                        pltpu.SemaphoreType.DMA(())],
        compiler_params=pltpu.CompilerParams(
            dimension_semantics=("arbitrary", "arbitrary"),
            vmem_limit_bytes=VMEM_LIMIT_BYTES),
        name="in_proj",
    )(x2, gain, w, b_gate, lb_logits, qg, kg, *cast_srcs)


def _hgrn_kernel(q_ref, k_ref, v_ref, g_ref, lf_ref, gain_ref, *rest, n_cast):
    cast_in = rest[:n_cast]
    out_ref = rest[n_cast]
    cast_out = rest[n_cast + 1:2 * n_cast + 1]
    st_ref, cb_ref, sel_ref, tri_ref, trib_ref = rest[-5:]
    _cast_blocks(cast_in, cast_out)
    n_g = st_ref.shape[0]
    L, SUB = HG_L, HG_SUB

    @pl.when(pl.program_id(1) == 0)
    def _init():
        st_ref[...] = jnp.zeros_like(st_ref)
        cb_ref[...] = jnp.full(cb_ref.shape, BIG_EXPONENT, F32)
        r = lax.broadcasted_iota(jnp.int32, sel_ref.shape, 0)
        c = lax.broadcasted_iota(jnp.int32, sel_ref.shape, 1)
        sel_ref[...] = jnp.where(c == SUB - 1 - r // LANES, 1.0, 0.0).astype(BF16)
        row = lax.broadcasted_iota(jnp.int32, (L, L), 0)
        col = lax.broadcasted_iota(jnp.int32, (L, L), 1)
        tri_ref[...] = jnp.where(row >= col, 1.0, 0.0).astype(BF16)
        trib_ref[...] = jnp.where((row >= col) & (col > (row // SUB) * SUB), 1.0, 0.0
                                  ).astype(BF16)

    heads = [slice(g * LANES, (g + 1) * LANES) for g in range(n_g)]
    gw = n_g * LANES
    state = [st_ref[g] for g in range(n_g)]
    chunks = []
    for ch in range(HG_CHUNKS):
        rows = slice(ch * L, (ch + 1) * L)
        state, ctx = _hgrn_front(rows, state, heads, gw, q_ref, k_ref, v_ref, lf_ref,
                                 tri_ref, trib_ref)
        chunks.append((rows, ch, ctx))
    for g in range(n_g):
        st_ref[g] = state[g]

    d_min = functools.reduce(jnp.minimum, [jnp.min(ctx[3]) for _, _, ctx in chunks])
    fast = d_min >= -HG_FAST_RANGE

    @pl.when(fast)
    def _fast():
        _hgrn_back_fast(chunks, heads, g_ref, gain_ref, out_ref)

    @pl.when(jnp.logical_not(fast))
    def _exact():
        for rows, ch, ctx in chunks:
            _hgrn_back_exact(rows, ch, ctx, heads, g_ref, gain_ref, out_ref, cb_ref, sel_ref)


def _hgrn_front(rows, state, heads, gw, q_ref, k_ref, v_ref, lf_ref, tri_ref, trib_ref):
    L, SUB = HG_L, HG_SUB

    pieces = [lf_ref[0, rows, p * gw:(p + 1) * gw] for p in range(LOGF_PIECES)]
    b = sum(jnp.dot(tri_ref[...], p, preferred_element_type=F32) for p in pieces)
    d = sum(jnp.dot(trib_ref[...], p, preferred_element_type=F32) for p in pieces)
    q = q_ref[0, rows, :].astype(F32)
    k = k_ref[0, rows, :].astype(F32)
    v_bf = v_ref[0, rows, :]
    v = v_bf.astype(F32)

    qe = (q * jnp.exp2(b)).astype(BF16)
    o_blocks = [[None] * (L // SUB) for _ in heads]
    for g, sl in enumerate(heads):
        o = lax.dot_general(qe[:, sl], state[g].astype(BF16), NT_DIMS,
                            preferred_element_type=F32)
        for i in range(L // SUB):
            o_blocks[g][i] = o[i * SUB:(i + 1) * SUB]

    s_far = {}
    for i in range(1, L // SUB):
        r0 = i * SUB
        bref = b[r0:r0 + 1]
        qt = (q[r0:r0 + SUB] * jnp.exp2(b[r0:r0 + SUB] - bref)).astype(BF16)
        kt = (k[0:r0] * jnp.exp2(bref - b[0:r0])).astype(BF16)
        for g, sl in enumerate(heads):
            s_far[i, g] = lax.dot_general(qt[:, sl], kt[:, sl], NT_DIMS,
                                          preferred_element_type=F32)

    b_last = b[L - 1:L]
    kdec = (k * jnp.exp2(b_last - b)).astype(BF16)
    eb_last = jnp.exp2(b_last)
    new_state = []
    for g, sl in enumerate(heads):
        upd = jnp.dot(jnp.transpose(v[:, sl]).astype(BF16), kdec[:, sl],
                      preferred_element_type=F32)
        new_state.append(state[g] * eb_last[:, sl] + upd)
    return new_state, (q, k, b, d, v_bf, o_blocks, s_far)


def _hgrn_far(o_blocks, s_far, v_bf, heads, min_distance):
    L, SUB = HG_L, HG_SUB
    out = [list(blocks) for blocks in o_blocks]
    for i in range(1, L // SUB):
        r0 = i * SUB
        keep = (lax.broadcasted_iota(jnp.int32, (SUB, r0), 1) + min_distance
                <= lax.broadcasted_iota(jnp.int32, (SUB, r0), 0) + r0)
        for g, sl in enumerate(heads):
            s = s_far[i, g] if min_distance <= 1 else jnp.where(keep, s_far[i, g], 0.0)
            oi = jnp.dot(s.astype(BF16), v_bf[0:r0, sl], preferred_element_type=F32)
            out[g][i] = out[g][i] + oi
    return out


def _hgrn_finish(o, rows, sl, g_ref, gain_ref, out_ref):
    ms = jnp.mean(o * o, axis=-1, keepdims=True)
    y = o * lax.rsqrt(ms + EPS) * gain_ref[:, sl] * g_ref[0, rows, sl].astype(F32)
    out_ref[0, rows, sl] = y.astype(BF16)


def _hgrn_back_fast(chunks, heads, g_ref, gain_ref, out_ref):
    L, SUB = HG_L, HG_SUB
    row = lax.broadcasted_iota(jnp.int32, (L, L), 0)
    col = lax.broadcasted_iota(jnp.int32, (L, L), 1)
    same = (col <= row) & (col >= (row // SUB) * SUB)
    s_near = []
    for _, _, (q, k, b, d, v_bf, o_blocks, s_far) in chunks:
        qn = (q * jnp.exp2(d)).astype(BF16)
        kn = (k * jnp.exp2(-d)).astype(BF16)
        s_near.append([lax.dot_general(qn[:, sl], kn[:, sl], NT_DIMS,
                                       preferred_element_type=F32) for sl in heads])
    far = [_hgrn_far(o_blocks, s_far, v_bf, heads, min_distance=1)
           for _, _, (q, k, b, d, v_bf, o_blocks, s_far) in chunks]
    near = [[jnp.dot(jnp.where(same, s_near[n][g], 0.0).astype(BF16), ctx[4][:, sl],
                     preferred_element_type=F32) for g, sl in enumerate(heads)]
            for n, (_, _, ctx) in enumerate(chunks)]
    for n, (rows, _, _) in enumerate(chunks):
        for g, sl in enumerate(heads):
            _hgrn_finish(jnp.concatenate(far[n][g], axis=0) + near[n][g], rows, sl,
                         g_ref, gain_ref, out_ref)


def _hgrn_back_exact(rows, ch, ctx, heads, g_ref, gain_ref, out_ref, cb_ref, sel_ref):
    L, SUB = HG_L, HG_SUB
    q, k, b, d, v_bf, o_blocks, s_far = ctx
    o_blocks = _hgrn_far(o_blocks, s_far, v_bf, heads, min_distance=SUB)

    c = b - jnp.log2(k)
    for g, sl in enumerate(heads):
        cb_ref[ch, g, SUB:SUB + L, :] = c[:, sl]

    near = []
    for g, sl in enumerate(heads):
        qg, bg = q[:, sl], b[:, sl]
        e = [(qg * jnp.exp2(bg - cb_ref[ch, g, SUB - off:SUB - off + L, :])).astype(BF16)
             for off in range(SUB)]
        near.append(jnp.dot(jnp.concatenate(e, axis=1), sel_ref[...],
                            preferred_element_type=F32))

    for g, sl in enumerate(heads):
        s = pltpu.roll(near[g], LANES - (SUB - 1), axis=1, stride=1, stride_axis=0)
        o = jnp.concatenate(o_blocks[g], axis=0) + jnp.dot(
            s[:, 0:L].astype(BF16), v_bf[:, sl], preferred_element_type=F32)
        _hgrn_finish(o, rows, sl, g_ref, gain_ref, out_ref)


def _hgrn(proj3, logf3, gain, cast_srcs, *, d_a):
    bsz, t, _ = proj3.shape
    n_g = d_a // LANES
    rows = HG_CHUNKS * HG_L
    grid = (bsz, t // rows)
    cast_specs = [_cast_spec(a.shape, grid) for a in cast_srcs]

    def sec(s):
        return pl.BlockSpec((1, rows, d_a), lambda b, c, s=s: (b, c, s))

    return pl.pallas_call(
        functools.partial(_hgrn_kernel, n_cast=len(cast_srcs)),
        out_shape=(jax.ShapeDtypeStruct((bsz, t, d_a), BF16),
                   *[jax.ShapeDtypeStruct(a.shape, BF16) for a in cast_srcs]),
        grid=grid,
        in_specs=[sec(0), sec(1), sec(2), sec(3),
                  pl.BlockSpec((1, rows, LOGF_PIECES * d_a), lambda b, c: (b, c, 0)),
                  pl.BlockSpec((1, d_a), lambda b, c: (0, 0)),
                  *cast_specs],
        out_specs=(pl.BlockSpec((1, rows, d_a), lambda b, c: (b, c, 0)), *cast_specs),
        scratch_shapes=[pltpu.VMEM((n_g, LANES, LANES), F32),
                        pltpu.VMEM((HG_CHUNKS, n_g, HG_L + HG_SUB, LANES), F32),
                        pltpu.VMEM((HG_SUB * LANES, LANES), BF16),
                        pltpu.VMEM((HG_L, HG_L), BF16),
                        pltpu.VMEM((HG_L, HG_L), BF16)],
        compiler_params=pltpu.CompilerParams(
            dimension_semantics=("arbitrary", "arbitrary")),
        name="hgrn",
    )(proj3, proj3, proj3, proj3, logf3, gain, *cast_srcs)


def _attn_kernel(q_ref, k_ref, v_ref, gvec_ref, *rest, n_cast):
    cast_in = rest[:n_cast]
    out_ref = rest[n_cast]
    cast_out = rest[n_cast + 1:2 * n_cast + 1]
    band_ref, bias_ref, vx_ref = rest[-3:]
    _cast_blocks(cast_in, cast_out)
    t = q_ref.shape[1]
    nq = AT_Q // CHUNK
    nk = nq + N_PAST_CHUNKS
    n_tb = 3
    wb = (nk + nq - 1) * CHUNK

    @pl.when(pl.program_id(1) == 0)
    def _build_bias():
        grow = gvec_ref[0] * LOG2E
        xb = jnp.broadcast_to(grow, (CHUNK, 2 * LANES))
        r = lax.broadcasted_iota(jnp.int32, (CHUNK, 2 * LANES), 0)
        for bit in range(6):
            xb = jnp.where(((r >> bit) & 1) == 1, pltpu.roll(xb, 1 << bit, axis=1), xb)
        const = grow[:, 0:1]

        n_const = nk - nq + 1 - n_tb
        lo = (nq - 1) * CHUNK
        band_ref[:, 0:lo] = jnp.full((CHUNK, lo), MASK_VALUE, F32)
        band_ref[:, lo:lo + n_const * CHUNK] = jnp.broadcast_to(const, (CHUNK, n_const * CHUNK))
        band_ref[:, lo + n_const * CHUNK:lo + (n_const + n_tb) * CHUNK] = xb[:, CHUNK:]
        band_ref[:, wb - lo:wb] = jnp.full((CHUNK, lo), MASK_VALUE, F32)
        for qi in range(nq):
            off = (nq - 1 - qi) * CHUNK
            bias_ref[qi * CHUNK:(qi + 1) * CHUNK, :] = band_ref[:, off:off + nk * CHUNK]

    vx_ref[:, 0:LANES] = v_ref[0]
    vx_ref[:, LANES:2 * LANES] = jnp.ones((t, LANES), BF16)

    def window(g):
        q0 = g * AT_Q
        ks = max(0, q0 - N_PAST_CHUNKS * CHUNK)
        return q0, ks, q0 + AT_Q - ks

    def scores(g):
        q0, ks, kw = window(g)
        return lax.dot_general(q_ref[0, q0:q0 + AT_Q, :], k_ref[0, ks:ks + kw, :], NT_DIMS,
                               preferred_element_type=F32)

    n_groups = t // AT_Q
    s_next = scores(0)
    for g in range(n_groups):
        q0, ks, kw = window(g)
        s = s_next + bias_ref[:, nk * CHUNK - kw:nk * CHUNK]
        if g + 1 < n_groups:
            s_next = scores(g + 1)
        p = jnp.exp2(s - jnp.max(s, axis=-1, keepdims=True))
        ox = jnp.dot(p.astype(BF16), vx_ref[ks:ks + kw, :], preferred_element_type=F32)
        out_ref[0, q0:q0 + AT_Q, :] = (ox[:, 0:LANES] / ox[:, LANES:2 * LANES]).astype(BF16)


def _attn(proj3, gvec, cast_srcs, *, col0, n_heads):
    bsz, t, _ = proj3.shape
    d_b = n_heads * LANES
    nq = AT_Q // CHUNK
    nk = nq + N_PAST_CHUNKS
    blk0 = col0 // LANES
    grid = (n_heads, bsz)
    cast_specs = [_cast_spec(a.shape, grid) for a in cast_srcs]

    def sec(s):
        return pl.BlockSpec((1, t, LANES), lambda h, b, s=s: (b, 0, blk0 + s * n_heads + h))

    return pl.pallas_call(
        functools.partial(_attn_kernel, n_cast=len(cast_srcs)),
        out_shape=(jax.ShapeDtypeStruct((bsz, t, d_b), BF16),
                   *[jax.ShapeDtypeStruct(a.shape, BF16) for a in cast_srcs]),
        grid=grid,
        in_specs=[sec(0), sec(1), sec(2),
                  pl.BlockSpec((1, 1, 2 * LANES), lambda h, b: (h, 0, 0)),
                  *cast_specs],
        out_specs=(pl.BlockSpec((1, t, LANES), lambda h, b: (b, 0, h)), *cast_specs),
        scratch_shapes=[pltpu.VMEM((CHUNK, (nk + nq - 1) * CHUNK), F32),
                        pltpu.VMEM((AT_Q, nk * CHUNK), F32),
                        pltpu.VMEM((t, 2 * LANES), BF16)],
        compiler_params=pltpu.CompilerParams(
            dimension_semantics=("arbitrary", "arbitrary")),
        name="attn",
    )(proj3, proj3, proj3, gvec, *cast_srcs)


def _merge_kernel(ya_ref, yb_ref, ga0_ref, ga1_ref, gb0_ref, gb1_ref, x_ref,
                  wa_ref, wb_ref, wo_ref, gain_ref, x1_ref, h2_ref):
    pa = jnp.dot(ya_ref[...], wa_ref[...], preferred_element_type=F32)
    pb = jnp.dot(yb_ref[...], wb_ref[...], preferred_element_type=F32)
    ga = jnp.concatenate([ga0_ref[...], ga1_ref[...]], axis=1).astype(F32)
    gb = jnp.concatenate([gb0_ref[...], gb1_ref[...]], axis=1).astype(F32)
    merged = (ga * pa + gb * pb).astype(BF16)
    x1 = x_ref[...] + jnp.dot(merged, wo_ref[...], preferred_element_type=F32)
    x1_ref[...] = x1
    ms = jnp.mean(x1 * x1, axis=-1, keepdims=True)
    h2_ref[...] = (x1 * lax.rsqrt(ms + EPS) * gain_ref[...]).astype(BF16)


def _merge(ya, yb, proj, x2, wa, wb, wo, gain, *, gate_col0):
    m, d = x2.shape
    da, db = ya.shape[1], yb.shape[1]
    tm = MG_TM
    gblk = gate_col0 // IN_TN

    def gate(s):
        return pl.BlockSpec((tm, IN_TN), lambda i, s=s: (i, gblk + s))

    def whole(a):
        return pl.BlockSpec(a.shape, lambda i: (0, 0), pipeline_mode=pl.Buffered(1))

    return pl.pallas_call(
        _merge_kernel,
        out_shape=(jax.ShapeDtypeStruct((m, d), F32),
                   jax.ShapeDtypeStruct((m, d), BF16)),
        grid=(m // tm,),
        in_specs=[pl.BlockSpec((tm, da), lambda i: (i, 0)),
                  pl.BlockSpec((tm, db), lambda i: (i, 0)),
                  gate(0), gate(1), gate(2), gate(3),
                  pl.BlockSpec((tm, d), lambda i: (i, 0)),
                  whole(wa), whole(wb), whole(wo), whole(gain)],
        out_specs=(pl.BlockSpec((tm, d), lambda i: (i, 0)),
                   pl.BlockSpec((tm, d), lambda i: (i, 0))),
        compiler_params=pltpu.CompilerParams(
            dimension_semantics=("arbitrary",),
            vmem_limit_bytes=VMEM_LIMIT_BYTES),
        name="merge",
    )(ya, yb, proj, proj, proj, proj, x2, wa, wb, wo, gain)


def _ffn_kernel(h_ref, wg_ref, wu_ref, wd_ref, x1_hbm, out_ref, x1_ref, sem):
    i, f = pl.program_id(0), pl.program_id(1)
    tm = out_ref.shape[0]

    def residual_copy():
        return pltpu.make_async_copy(x1_hbm.at[pl.ds(i * tm, tm), :], x1_ref, sem)

    def hidden_tile():
        h = h_ref[...]
        gate = jnp.dot(h, wg_ref[...], preferred_element_type=F32)
        up = jnp.dot(h, wu_ref[...], preferred_element_type=F32)
        act = (gate * _sigmoid(gate) * up).astype(BF16)
        return jnp.dot(act, wd_ref[...], preferred_element_type=F32)

    last = pl.num_programs(1) - 1

    @pl.when(f == 0)
    def _first():
        residual_copy().start()
        out_ref[...] = hidden_tile()

    @pl.when((f > 0) & (f < last))
    def _middle():
        out_ref[...] += hidden_tile()

    @pl.when(f == last)
    def _last():
        residual_copy().wait()
        out_ref[...] += hidden_tile() + x1_ref[...]


def _ffn(h2, x1, w_in, w_out):
    m, d = h2.shape
    dff = w_out.shape[0]
    tm, tf = FF_TM, FF_TF
    nf = dff // tf
    return pl.pallas_call(
        _ffn_kernel,
        out_shape=jax.ShapeDtypeStruct((m, d), F32),
        grid=(m // tm, nf),
        in_specs=[pl.BlockSpec((tm, d), lambda i, f: (i, 0)),
                  pl.BlockSpec((d, tf), lambda i, f: (0, f)),
                  pl.BlockSpec((d, tf), lambda i, f: (0, nf + f)),
                  pl.BlockSpec((tf, d), lambda i, f: (f, 0)),
                  pl.BlockSpec(memory_space=pl.ANY)],
        out_specs=pl.BlockSpec((tm, d), lambda i, f: (i, 0)),
        scratch_shapes=[pltpu.VMEM((tm, d), F32), pltpu.SemaphoreType.DMA(())],
        compiler_params=pltpu.CompilerParams(
            dimension_semantics=("arbitrary", "arbitrary"),
            vmem_limit_bytes=VMEM_LIMIT_BYTES),
        name="ffn",
    )(h2, w_in, w_in, w_out, x1)


def _bias_vector(rel_bias):
    rev = rel_bias[:, ::-1]
    pad = 2 * LANES - N_REL
    return jnp.pad(rev, ((0, 0), (pad, 0)), mode="edge")[:, None, :]


def kernel(x, w_in, b_gate, norm_mix, norm_ffn, hgrn_lb_logits, hgrn_out_gain,
           q_gain, k_gain, rel_bias, w_proj_a, w_proj_b, w_out, w_ffn_in, w_ffn_out):
    bsz, t, d = x.shape
    depth = w_in.shape[0]
    d_a = hgrn_out_gain.shape[1]
    dh = q_gain.shape[1]
    n_heads_b = rel_bias.shape[1]
    d_b = n_heads_b * dh
    n_in = w_in.shape[2]
    gate_col0 = 4 * d_a + 3 * d_b
    assert dh == LANES and d_a == IN_TN and d_b == IN_TN and depth == 1
    assert hgrn_lb_logits.shape[0] == depth + 1
    assert n_in == gate_col0 + 2 * d and t % AT_Q == 0

    m = bsz * t
    x2 = x.reshape(m, d)
    for l in range(depth):
        proj, logf = _in_proj(
            x2, norm_mix[l][None, :], w_in[l], b_gate[l][None, :], hgrn_lb_logits,
            q_gain[l][None, :], k_gain[l][None, :], (), scale=dh ** -0.5 * LOG2E)
        proj3 = proj.reshape(bsz, t, n_in)
        y_a, w_ffn_in_bf, w_ffn_out_bf = _hgrn(
            proj3, logf.reshape(bsz, t, LOGF_PIECES * d_a), hgrn_out_gain[l][None, :],
            (w_ffn_in[l], w_ffn_out[l]), d_a=d_a)
        y_b, wa_bf, wb_bf, wo_bf = _attn(
            proj3, _bias_vector(rel_bias[l]), (w_proj_a[l], w_proj_b[l], w_out[l]),
            col0=4 * d_a, n_heads=n_heads_b)
        x1, h2 = _merge(y_a.reshape(m, d_a), y_b.reshape(m, d_b), proj, x2,
                        wa_bf, wb_bf, wo_bf, norm_ffn[l][None, :], gate_col0=gate_col0)
        x2 = _ffn(h2, x1, w_ffn_in_bf, w_ffn_out_bf)
    return x2.reshape(bsz, t, d)
```

```python
import functools

import jax
import jax.numpy as jnp
from jax import lax
from jax.experimental import pallas as pl
from jax.experimental.pallas import tpu as pltpu

F32 = jnp.float32
BF16 = jnp.bfloat16

EPS = 1e-6
LANES = 128
BF16_SUBLANES = 16
VMEM_LIMIT_BYTES = 56 * 1024 * 1024
CHUNK = 64
N_PAST_CHUNKS = 8
REL_FUTURE = CHUNK - 1
REL_PAST = 2 * CHUNK - 1
N_REL = REL_FUTURE + REL_PAST + 1
MASK_VALUE = -1e30
BIG_EXPONENT = 1e30
LOGF_PIECES = 3
LOG2E = 1.4426950408889634

IN_TM = 1024
IN_TN = 1024
IN_SUB = 256
IN_FIRST_GATE_SECTION = 7
HG_L = 128
HG_CHUNKS = 2
HG_FAST_RANGE = 100.0
HG_SUB = 32
AT_Q = 4 * CHUNK
MG_TM = 512
FF_TM = 1024
FF_TF = 512

NT_DIMS = (((1,), (1,)), ((), ()))


def _sigmoid(x):
    return 0.5 * jnp.tanh(0.5 * x) + 0.5


def _cast_plan(srcs, grid):
    steps, strides = 1, []
    for g in reversed(grid):
        strides.insert(0, steps)
        steps *= g

    def row_block(*ids):
        return sum(i * s for i, s in zip(ids, strides))

    arrays, in_specs, out_specs, out_shapes = [], [], [], []
    for src in srcs:
        a, part, n_parts = src if isinstance(src, tuple) else (src, 0, 1)
        rows, cols = a.shape[0], a.shape[1] // n_parts
        assert rows % (steps * BF16_SUBLANES) == 0 and cols % LANES == 0, (a.shape, grid)
        arrays.append(a)
        in_specs.append(pl.BlockSpec((rows // steps, cols),
                                     lambda *ids, part=part: (row_block(*ids), part)))
        out_specs.append(pl.BlockSpec((rows // steps, cols),
                                      lambda *ids: (row_block(*ids), 0)))
        out_shapes.append(jax.ShapeDtypeStruct((rows, cols), BF16))
    return arrays, in_specs, out_specs, out_shapes


def _cast_blocks(src_refs, dst_refs):
    for src, dst in zip(src_refs, dst_refs):
        dst[...] = src[...].astype(BF16)


def _in_proj_kernel(x_hbm, gain_ref, w_ref, bias_ref, lbl_ref, qg_ref, kg_ref, *rest,
                    scale):
    out_ref, logf_ref, h_ref, x_ref, x_sem = rest
    i, j = pl.program_id(0), pl.program_id(1)
    tm = x_ref.shape[0]

    def x_copy(tile):
        return pltpu.make_async_copy(x_hbm.at[pl.ds(tile * tm, tm), :], x_ref, x_sem)

    @pl.when((j == 0) & (i == 0))
    def _first_fetch():
        x_copy(0).start()

    @pl.when(j == 0)
    def _norm():
        x_copy(i).wait()
        x = x_ref[...]
        ms = jnp.mean(x * x, axis=-1, keepdims=True)
        h_ref[...] = (x * lax.rsqrt(ms + EPS) * gain_ref[...]).astype(BF16)

    @pl.when((j == 1) & (i + 1 < pl.num_programs(0)))
    def _next_fetch():
        x_copy(i + 1).start()

    def section(epilogue):
        rs = slice(None)
        for c in range(w_ref.shape[1] // IN_SUB):
            cs = slice(c * IN_SUB, (c + 1) * IN_SUB)
            epilogue(jnp.dot(h_ref[...], w_ref[:, cs].astype(BF16),
                             preferred_element_type=F32), rs, cs)

    @pl.when((j == 0) | (j == 3))
    def _silu():
        def epilogue(acc, rs, cs):
            out_ref[rs, cs] = (acc * _sigmoid(acc)).astype(BF16)
        section(epilogue)

    @pl.when(j == 1)
    def _forget():
        l = lbl_ref[...]
        e = jnp.exp(l - jnp.max(l, axis=0, keepdims=True))
        lb_row = e[0:1] / jnp.sum(e, axis=0, keepdims=True)

        def epilogue(acc, rs, cs):
            lb = lb_row[:, cs]
            sig = _sigmoid(acc)
            lf = jnp.log2(lb + (1.0 - lb) * sig)
            hi = lf.astype(BF16)
            r1 = lf - hi.astype(F32)
            mid = r1.astype(BF16)
            tn = w_ref.shape[1]
            logf_ref[rs, cs] = hi
            logf_ref[rs, slice(tn + cs.start, tn + cs.stop)] = mid
            logf_ref[rs, slice(2 * tn + cs.start, 2 * tn + cs.stop)] = (
                r1 - mid.astype(F32)).astype(BF16)
            out_ref[rs, cs] = ((1.0 - lb) * (1.0 - sig)).astype(BF16)
        section(epilogue)

    @pl.when((j == 2) | (j == 6))
    def _plain():
        def epilogue(acc, rs, cs):
            out_ref[rs, cs] = acc.astype(BF16)
        section(epilogue)

    def head_norm(gain_ref_, mult):
        def epilogue(acc, rs, cs):
            for h in range(IN_SUB // LANES):
                t = acc[:, h * LANES:(h + 1) * LANES]
                sl = slice(cs.start + h * LANES, cs.start + (h + 1) * LANES)
                ms = jnp.mean(t * t, axis=-1, keepdims=True)
                out_ref[rs, sl] = (t * lax.rsqrt(ms + EPS) * (gain_ref_[...] * mult)).astype(BF16)
        section(epilogue)

    @pl.when(j == 4)
    def _qnorm():
        head_norm(qg_ref, scale)

    @pl.when(j == 5)
    def _knorm():
        head_norm(kg_ref, 1.0)

    @pl.when(j >= IN_FIRST_GATE_SECTION)
    def _gates():
        def epilogue(acc, rs, cs):
            out_ref[rs, cs] = _sigmoid(acc + bias_ref[:, cs]).astype(BF16)
        section(epilogue)


def _in_proj(x2, gain, w, b_gate, lb_logits, qg, kg, *, scale):
    m, d = x2.shape
    n = w.shape[1]
    tm, tn = IN_TM, IN_TN
    grid = (m // tm, n // tn)
    first_gate = (n - b_gate.shape[1]) // tn
    assert first_gate == IN_FIRST_GATE_SECTION
    return pl.pallas_call(
        functools.partial(_in_proj_kernel, scale=scale),
        out_shape=(jax.ShapeDtypeStruct((m, n), BF16),
                   jax.ShapeDtypeStruct((m, LOGF_PIECES * tn), BF16)),
        grid=grid,
        in_specs=[
            pl.BlockSpec(memory_space=pl.ANY),
            pl.BlockSpec((1, d), lambda i, j: (0, 0)),
            pl.BlockSpec((d, tn), lambda i, j: (0, j)),
            pl.BlockSpec((1, tn), lambda i, j: (0, jnp.maximum(j - first_gate, 0))),
            pl.BlockSpec(lb_logits.shape, lambda i, j: (0, 0)),
            pl.BlockSpec((1, LANES), lambda i, j: (0, 0)),
            pl.BlockSpec((1, LANES), lambda i, j: (0, 0)),
        ],
        out_specs=(pl.BlockSpec((tm, tn), lambda i, j: (i, j)),
                   pl.BlockSpec((tm, LOGF_PIECES * tn), lambda i, j: (i, 0))),
        scratch_shapes=[pltpu.VMEM((tm, d), BF16), pltpu.VMEM((tm, d), F32),
                        pltpu.SemaphoreType.DMA(())],
        compiler_params=pltpu.CompilerParams(
            dimension_semantics=("arbitrary", "arbitrary"),
            vmem_limit_bytes=VMEM_LIMIT_BYTES),
        name="in_proj",
    )(x2, gain, w, b_gate, lb_logits, qg, kg)


def _hgrn_kernel(q_ref, k_ref, v_ref, g_ref, lf_ref, gain_ref, *rest, n_cast):
    cast_in = rest[:n_cast]
    out_ref = rest[n_cast]
    cast_out = rest[n_cast + 1:2 * n_cast + 1]
    st_ref, cb_ref, sel_ref, tri_ref, trib_ref = rest[-5:]
    _cast_blocks(cast_in, cast_out)
    n_g = st_ref.shape[0]
    L, SUB = HG_L, HG_SUB

    @pl.when(pl.program_id(1) == 0)
    def _init():
        st_ref[...] = jnp.zeros_like(st_ref)
        cb_ref[...] = jnp.full(cb_ref.shape, BIG_EXPONENT, F32)
        r = lax.broadcasted_iota(jnp.int32, sel_ref.shape, 0)
        c = lax.broadcasted_iota(jnp.int32, sel_ref.shape, 1)
        sel_ref[...] = jnp.where(c == SUB - 1 - r // LANES, 1.0, 0.0).astype(BF16)
        row = lax.broadcasted_iota(jnp.int32, (L, L), 0)
        col = lax.broadcasted_iota(jnp.int32, (L, L), 1)
        tri_ref[...] = jnp.where(row >= col, 1.0, 0.0).astype(BF16)
        trib_ref[...] = jnp.where((row >= col) & (col > (row // SUB) * SUB), 1.0, 0.0
                                  ).astype(BF16)

    heads = [slice(g * LANES, (g + 1) * LANES) for g in range(n_g)]
    gw = n_g * LANES
    state = [st_ref[g] for g in range(n_g)]
    chunks = []
    for ch in range(HG_CHUNKS):
        rows = slice(ch * L, (ch + 1) * L)
        state, ctx = _hgrn_front(rows, state, heads, gw, q_ref, k_ref, v_ref, lf_ref,
                                 tri_ref, trib_ref)
        chunks.append((rows, ch, ctx))
    for g in range(n_g):
        st_ref[g] = state[g]

    d_min = functools.reduce(jnp.minimum, [jnp.min(ctx[3]) for _, _, ctx in chunks])
    fast = d_min >= -HG_FAST_RANGE

    @pl.when(fast)
    def _fast():
        _hgrn_back_fast(chunks, heads, g_ref, gain_ref, out_ref)

    @pl.when(jnp.logical_not(fast))
    def _exact():
        for rows, ch, ctx in chunks:
            _hgrn_back_exact(rows, ch, ctx, heads, g_ref, gain_ref, out_ref, cb_ref, sel_ref)


def _hgrn_front(rows, state, heads, gw, q_ref, k_ref, v_ref, lf_ref, tri_ref, trib_ref):
    L, SUB = HG_L, HG_SUB

    pieces = [lf_ref[0, rows, p * gw:(p + 1) * gw] for p in range(LOGF_PIECES)]
    b = sum(jnp.dot(tri_ref[...], p, preferred_element_type=F32) for p in pieces)
    d = sum(jnp.dot(trib_ref[...], p, preferred_element_type=F32) for p in pieces)
    q = q_ref[0, rows, :].astype(F32)
    k = k_ref[0, rows, :].astype(F32)
    v_bf = v_ref[0, rows, :]
    v = v_bf.astype(F32)

    qe = (q * jnp.exp2(b)).astype(BF16)
    o_blocks = [[None] * (L // SUB) for _ in heads]
    for g, sl in enumerate(heads):
        o = lax.dot_general(qe[:, sl], state[g].astype(BF16), NT_DIMS,
                            preferred_element_type=F32)
        for i in range(L // SUB):
            o_blocks[g][i] = o[i * SUB:(i + 1) * SUB]

    s_far = {}
    for i in range(1, L // SUB):
        r0 = i * SUB
        bref = b[r0:r0 + 1]
        qt = (q[r0:r0 + SUB] * jnp.exp2(b[r0:r0 + SUB] - bref)).astype(BF16)
        kt = (k[0:r0] * jnp.exp2(bref - b[0:r0])).astype(BF16)
        for g, sl in enumerate(heads):
            s_far[i, g] = lax.dot_general(qt[:, sl], kt[:, sl], NT_DIMS,
                                          preferred_element_type=F32)

    b_last = b[L - 1:L]
    kdec = (k * jnp.exp2(b_last - b)).astype(BF16)
    eb_last = jnp.exp2(b_last)
    new_state = []
    for g, sl in enumerate(heads):
        upd = jnp.dot(jnp.transpose(v[:, sl]).astype(BF16), kdec[:, sl],
                      preferred_element_type=F32)
        new_state.append(state[g] * eb_last[:, sl] + upd)
    return new_state, (q, k, b, d, v_bf, o_blocks, s_far)


def _hgrn_far(o_blocks, s_far, v_bf, heads, min_distance):
    L, SUB = HG_L, HG_SUB
    out = [list(blocks) for blocks in o_blocks]
    for i in range(1, L // SUB):
        r0 = i * SUB
        keep = (lax.broadcasted_iota(jnp.int32, (SUB, r0), 1) + min_distance
                <= lax.broadcasted_iota(jnp.int32, (SUB, r0), 0) + r0)
        for g, sl in enumerate(heads):
            s = s_far[i, g] if min_distance <= 1 else jnp.where(keep, s_far[i, g], 0.0)
            oi = jnp.dot(s.astype(BF16), v_bf[0:r0, sl], preferred_element_type=F32)
            out[g][i] = out[g][i] + oi
    return out


def _hgrn_finish(o, rows, sl, g_ref, gain_ref, out_ref):
    ms = jnp.mean(o * o, axis=-1, keepdims=True)
    y = o * lax.rsqrt(ms + EPS) * gain_ref[:, sl] * g_ref[0, rows, sl].astype(F32)
    out_ref[0, rows, sl] = y.astype(BF16)


def _hgrn_back_fast(chunks, heads, g_ref, gain_ref, out_ref):
    L, SUB = HG_L, HG_SUB
    row = lax.broadcasted_iota(jnp.int32, (L, L), 0)
    col = lax.broadcasted_iota(jnp.int32, (L, L), 1)
    same = (col <= row) & (col >= (row // SUB) * SUB)
    s_near = []
    for _, _, (q, k, b, d, v_bf, o_blocks, s_far) in chunks:
        qn = (q * jnp.exp2(d)).astype(BF16)
        kn = (k * jnp.exp2(-d)).astype(BF16)
        s_near.append([lax.dot_general(qn[:, sl], kn[:, sl], NT_DIMS,
                                       preferred_element_type=F32) for sl in heads])
    far = [_hgrn_far(o_blocks, s_far, v_bf, heads, min_distance=1)
           for _, _, (q, k, b, d, v_bf, o_blocks, s_far) in chunks]
    near = [[jnp.dot(jnp.where(same, s_near[n][g], 0.0).astype(BF16), ctx[4][:, sl],
                     preferred_element_type=F32) for g, sl in enumerate(heads)]
            for n, (_, _, ctx) in enumerate(chunks)]
    for n, (rows, _, _) in enumerate(chunks):
        for g, sl in enumerate(heads):
            _hgrn_finish(jnp.concatenate(far[n][g], axis=0) + near[n][g], rows, sl,
                         g_ref, gain_ref, out_ref)


def _hgrn_back_exact(rows, ch, ctx, heads, g_ref, gain_ref, out_ref, cb_ref, sel_ref):
    L, SUB = HG_L, HG_SUB
    q, k, b, d, v_bf, o_blocks, s_far = ctx
    o_blocks = _hgrn_far(o_blocks, s_far, v_bf, heads, min_distance=SUB)

    c = b - jnp.log2(k)
    for g, sl in enumerate(heads):
        cb_ref[ch, g, SUB:SUB + L, :] = c[:, sl]

    near = []
    for g, sl in enumerate(heads):
        qg, bg = q[:, sl], b[:, sl]
        e = [(qg * jnp.exp2(bg - cb_ref[ch, g, SUB - off:SUB - off + L, :])).astype(BF16)
             for off in range(SUB)]
        near.append(jnp.dot(jnp.concatenate(e, axis=1), sel_ref[...],
                            preferred_element_type=F32))

    for g, sl in enumerate(heads):
        s = pltpu.roll(near[g], LANES - (SUB - 1), axis=1, stride=1, stride_axis=0)
        o = jnp.concatenate(o_blocks[g], axis=0) + jnp.dot(
            s[:, 0:L].astype(BF16), v_bf[:, sl], preferred_element_type=F32)
        _hgrn_finish(o, rows, sl, g_ref, gain_ref, out_ref)


def _hgrn(proj3, logf3, gain, cast_srcs, *, d_a):
    bsz, t, _ = proj3.shape
    n_g = d_a // LANES
    rows = HG_CHUNKS * HG_L
    grid = (bsz, t // rows)
    cast_arrays, cast_in_specs, cast_out_specs, cast_out_shapes = _cast_plan(cast_srcs, grid)

    def sec(s):
        return pl.BlockSpec((1, rows, d_a), lambda b, c, s=s: (b, c, s))

    return pl.pallas_call(
        functools.partial(_hgrn_kernel, n_cast=len(cast_srcs)),
        out_shape=(jax.ShapeDtypeStruct((bsz, t, d_a), BF16), *cast_out_shapes),
        grid=grid,
        in_specs=[sec(0), sec(1), sec(2), sec(3),
                  pl.BlockSpec((1, rows, LOGF_PIECES * d_a), lambda b, c: (b, c, 0)),
                  pl.BlockSpec((1, d_a), lambda b, c: (0, 0)),
                  *cast_in_specs],
        out_specs=(pl.BlockSpec((1, rows, d_a), lambda b, c: (b, c, 0)), *cast_out_specs),
        scratch_shapes=[pltpu.VMEM((n_g, LANES, LANES), F32),
                        pltpu.VMEM((HG_CHUNKS, n_g, HG_L + HG_SUB, LANES), F32),
                        pltpu.VMEM((HG_SUB * LANES, LANES), BF16),
                        pltpu.VMEM((HG_L, HG_L), BF16),
                        pltpu.VMEM((HG_L, HG_L), BF16)],
        compiler_params=pltpu.CompilerParams(
            dimension_semantics=("arbitrary", "arbitrary")),
        name="hgrn",
    )(proj3, proj3, proj3, proj3, logf3, gain, *cast_arrays)


def _attn_kernel(q_ref, k_ref, v_ref, gvec_ref, *rest, n_cast):
    cast_in = rest[:n_cast]
    out_ref = rest[n_cast]
    cast_out = rest[n_cast + 1:2 * n_cast + 1]
    band_ref, bias_ref, vx_ref = rest[-3:]
    _cast_blocks(cast_in, cast_out)
    t = q_ref.shape[1]
    nq = AT_Q // CHUNK
    nk = nq + N_PAST_CHUNKS
    n_tb = 3
    wb = (nk + nq - 1) * CHUNK

    @pl.when(pl.program_id(1) == 0)
    def _build_bias():
        grow = gvec_ref[0] * LOG2E
        xb = jnp.broadcast_to(grow, (CHUNK, 2 * LANES))
        r = lax.broadcasted_iota(jnp.int32, (CHUNK, 2 * LANES), 0)
        for bit in range(6):
            xb = jnp.where(((r >> bit) & 1) == 1, pltpu.roll(xb, 1 << bit, axis=1), xb)
        const = grow[:, 0:1]

        n_const = nk - nq + 1 - n_tb
        lo = (nq - 1) * CHUNK
        band_ref[:, 0:lo] = jnp.full((CHUNK, lo), MASK_VALUE, F32)
        band_ref[:, lo:lo + n_const * CHUNK] = jnp.broadcast_to(const, (CHUNK, n_const * CHUNK))
        band_ref[:, lo + n_const * CHUNK:lo + (n_const + n_tb) * CHUNK] = xb[:, CHUNK:]
        band_ref[:, wb - lo:wb] = jnp.full((CHUNK, lo), MASK_VALUE, F32)
        for qi in range(nq):
            off = (nq - 1 - qi) * CHUNK
            bias_ref[qi * CHUNK:(qi + 1) * CHUNK, :] = band_ref[:, off:off + nk * CHUNK]

    vx_ref[:, 0:LANES] = v_ref[0]
    vx_ref[:, LANES:2 * LANES] = jnp.ones((t, LANES), BF16)

    def window(g):
        q0 = g * AT_Q
        ks = max(0, q0 - N_PAST_CHUNKS * CHUNK)
        return q0, ks, q0 + AT_Q - ks

    def scores(g):
        q0, ks, kw = window(g)
        return lax.dot_general(q_ref[0, q0:q0 + AT_Q, :], k_ref[0, ks:ks + kw, :], NT_DIMS,
                               preferred_element_type=F32)

    n_groups = t // AT_Q
    s_next = scores(0)
    for g in range(n_groups):
        q0, ks, kw = window(g)
        s = s_next + bias_ref[:, nk * CHUNK - kw:nk * CHUNK]
        if g + 1 < n_groups:
            s_next = scores(g + 1)
        p = jnp.exp2(s - jnp.max(s, axis=-1, keepdims=True))
        ox = jnp.dot(p.astype(BF16), vx_ref[ks:ks + kw, :], preferred_element_type=F32)
        out_ref[0, q0:q0 + AT_Q, :] = (ox[:, 0:LANES] / ox[:, LANES:2 * LANES]).astype(BF16)


def _attn(proj3, gvec, cast_srcs, *, col0, n_heads):
    bsz, t, _ = proj3.shape
    d_b = n_heads * LANES
    nq = AT_Q // CHUNK
    nk = nq + N_PAST_CHUNKS
    blk0 = col0 // LANES
    grid = (n_heads, bsz)
    cast_arrays, cast_in_specs, cast_out_specs, cast_out_shapes = _cast_plan(cast_srcs, grid)

    def sec(s):
        return pl.BlockSpec((1, t, LANES), lambda h, b, s=s: (b, 0, blk0 + s * n_heads + h))

    return pl.pallas_call(
        functools.partial(_attn_kernel, n_cast=len(cast_srcs)),
        out_shape=(jax.ShapeDtypeStruct((bsz, t, d_b), BF16), *cast_out_shapes),
        grid=grid,
        in_specs=[sec(0), sec(1), sec(2),
                  pl.BlockSpec((1, 1, 2 * LANES), lambda h, b: (h, 0, 0)),
                  *cast_in_specs],
        out_specs=(pl.BlockSpec((1, t, LANES), lambda h, b: (b, 0, h)), *cast_out_specs),
        scratch_shapes=[pltpu.VMEM((CHUNK, (nk + nq - 1) * CHUNK), F32),
                        pltpu.VMEM((AT_Q, nk * CHUNK), F32),
                        pltpu.VMEM((t, 2 * LANES), BF16)],
        compiler_params=pltpu.CompilerParams(
            dimension_semantics=("arbitrary", "arbitrary")),
        name="attn",
    )(proj3, proj3, proj3, gvec, *cast_arrays)


def _merge_kernel(ya_ref, yb_ref, ga0_ref, ga1_ref, gb0_ref, gb1_ref, x_ref,
                  wa_ref, wb_ref, wo_ref, gain_ref, x1_ref, h2_ref):
    pa = jnp.dot(ya_ref[...], wa_ref[...], preferred_element_type=F32)
    pb = jnp.dot(yb_ref[...], wb_ref[...], preferred_element_type=F32)
    ga = jnp.concatenate([ga0_ref[...], ga1_ref[...]], axis=1).astype(F32)
    gb = jnp.concatenate([gb0_ref[...], gb1_ref[...]], axis=1).astype(F32)
    merged = (ga * pa + gb * pb).astype(BF16)
    x1 = x_ref[...] + jnp.dot(merged, wo_ref[...], preferred_element_type=F32)
    x1_ref[...] = x1
    ms = jnp.mean(x1 * x1, axis=-1, keepdims=True)
    h2_ref[...] = (x1 * lax.rsqrt(ms + EPS) * gain_ref[...]).astype(BF16)


def _merge(ya, yb, proj, x2, wa, wb, wo, gain, *, gate_col0):
    m, d = x2.shape
    da, db = ya.shape[1], yb.shape[1]
    tm = MG_TM
    gblk = gate_col0 // IN_TN

    def gate(s):
        return pl.BlockSpec((tm, IN_TN), lambda i, s=s: (i, gblk + s))

    def whole(a):
        return pl.BlockSpec(a.shape, lambda i: (0, 0), pipeline_mode=pl.Buffered(1))

    return pl.pallas_call(
        _merge_kernel,
        out_shape=(jax.ShapeDtypeStruct((m, d), F32),
                   jax.ShapeDtypeStruct((m, d), BF16)),
        grid=(m // tm,),
        in_specs=[pl.BlockSpec((tm, da), lambda i: (i, 0)),
                  pl.BlockSpec((tm, db), lambda i: (i, 0)),
                  gate(0), gate(1), gate(2), gate(3),
                  pl.BlockSpec((tm, d), lambda i: (i, 0)),
                  whole(wa), whole(wb), whole(wo), whole(gain)],
        out_specs=(pl.BlockSpec((tm, d), lambda i: (i, 0)),
                   pl.BlockSpec((tm, d), lambda i: (i, 0))),
        compiler_params=pltpu.CompilerParams(
            dimension_semantics=("arbitrary",),
            vmem_limit_bytes=VMEM_LIMIT_BYTES),
        name="merge",
    )(ya, yb, proj, proj, proj, proj, x2, wa, wb, wo, gain)


def _ffn_kernel(h_ref, wg_ref, wu_ref, wd_ref, x1_hbm, out_ref, x1_ref, sem):
    i, f = pl.program_id(0), pl.program_id(1)
    tm = out_ref.shape[0]

    def residual_copy():
        return pltpu.make_async_copy(x1_hbm.at[pl.ds(i * tm, tm), :], x1_ref, sem)

    def hidden_tile():
        h = h_ref[...]
        gate = jnp.dot(h, wg_ref[...], preferred_element_type=F32)
        up = jnp.dot(h, wu_ref[...], preferred_element_type=F32)
        act = (gate * _sigmoid(gate) * up).astype(BF16)
        return jnp.dot(act, wd_ref[...].astype(BF16), preferred_element_type=F32)

    last = pl.num_programs(1) - 1

    @pl.when(f == 0)
    def _first():
        residual_copy().start()
        out_ref[...] = hidden_tile()

    @pl.when((f > 0) & (f < last))
    def _middle():
        out_ref[...] += hidden_tile()

    @pl.when(f == last)
    def _last():
        residual_copy().wait()
        out_ref[...] += hidden_tile() + x1_ref[...]


def _ffn(h2, x1, w_gate, w_up, w_down):
    m, d = h2.shape
    dff = w_down.shape[0]
    tm, tf = FF_TM, FF_TF
    nf = dff // tf
    return pl.pallas_call(
        _ffn_kernel,
        out_shape=jax.ShapeDtypeStruct((m, d), F32),
        grid=(m // tm, nf),
        in_specs=[pl.BlockSpec((tm, d), lambda i, f: (i, 0)),
                  pl.BlockSpec((d, tf), lambda i, f: (0, f)),
                  pl.BlockSpec((d, tf), lambda i, f: (0, f)),
                  pl.BlockSpec((tf, d), lambda i, f: (f, 0)),
                  pl.BlockSpec(memory_space=pl.ANY)],
        out_specs=pl.BlockSpec((tm, d), lambda i, f: (i, 0)),
        scratch_shapes=[pltpu.VMEM((tm, d), F32), pltpu.SemaphoreType.DMA(())],
        compiler_params=pltpu.CompilerParams(
            dimension_semantics=("arbitrary", "arbitrary"),
            vmem_limit_bytes=VMEM_LIMIT_BYTES),
        name="ffn",
    )(h2, w_gate, w_up, w_down, x1)


def _bias_vector(rel_bias):
    rev = rel_bias[:, ::-1]
    pad = 2 * LANES - N_REL
    return jnp.pad(rev, ((0, 0), (pad, 0)), mode="edge")[:, None, :]


def kernel(x, w_in, b_gate, norm_mix, norm_ffn, hgrn_lb_logits, hgrn_out_gain,
           q_gain, k_gain, rel_bias, w_proj_a, w_proj_b, w_out, w_ffn_in, w_ffn_out):
    bsz, t, d = x.shape
    depth = w_in.shape[0]
    d_a = hgrn_out_gain.shape[1]
    dh = q_gain.shape[1]
    n_heads_b = rel_bias.shape[1]
    d_b = n_heads_b * dh
    n_in = w_in.shape[2]
    gate_col0 = 4 * d_a + 3 * d_b
    assert dh == LANES and d_a == IN_TN and d_b == IN_TN and depth == 1
    assert hgrn_lb_logits.shape[0] == depth + 1
    assert n_in == gate_col0 + 2 * d and t % AT_Q == 0

    m = bsz * t
    x2 = x.reshape(m, d)
    for l in range(depth):
        proj, logf = _in_proj(
            x2, norm_mix[l][None, :], w_in[l], b_gate[l][None, :], hgrn_lb_logits,
            q_gain[l][None, :], k_gain[l][None, :], scale=dh ** -0.5 * LOG2E)
        proj3 = proj.reshape(bsz, t, n_in)
        y_a, w_gate_bf = _hgrn(
            proj3, logf.reshape(bsz, t, LOGF_PIECES * d_a), hgrn_out_gain[l][None, :],
            ((w_ffn_in[l], 0, 2),), d_a=d_a)
        y_b, wa_bf, wb_bf, wo_bf, w_up_bf = _attn(
            proj3, _bias_vector(rel_bias[l]),
            (w_proj_a[l], w_proj_b[l], w_out[l], (w_ffn_in[l], 1, 2)),
            col0=4 * d_a, n_heads=n_heads_b)
        x1, h2 = _merge(y_a.reshape(m, d_a), y_b.reshape(m, d_b), proj, x2,
                        wa_bf, wb_bf, wo_bf, norm_ffn[l][None, :], gate_col0=gate_col0)
        x2 = _ffn(h2, x1, w_gate_bf, w_up_bf, w_ffn_out[l])
    return x2.reshape(bsz, t, d)
```

```python
import collections
import functools

import jax
import jax.numpy as jnp
from jax import lax
from jax.experimental import pallas as pl
from jax.experimental.pallas import tpu as pltpu

F32 = jnp.float32
BF16 = jnp.bfloat16

EPS = 1e-6
LANES = 128
BF16_SUBLANES = 16
VMEM_LIMIT_BYTES = 56 * 1024 * 1024
CHUNK = 64
N_PAST_CHUNKS = 8
REL_FUTURE = CHUNK - 1
REL_PAST = 2 * CHUNK - 1
N_REL = REL_FUTURE + REL_PAST + 1
MASK_VALUE = -1e30
BIG_EXPONENT = 1e30
LOGF_PIECES = 3
LOG2E = 1.4426950408889634

IN_TM = 1024
IN_TN = 1024
IN_SUB = 256
IN_FIRST_GATE_SECTION = 7
HG_L = 128
HG_CHUNKS = 2
HG_FAST_RANGE = 100.0
HG_SUB = 32
AT_Q = 4 * CHUNK
MG_TM = 512
FF_TM = 1024
FF_TF = 512

NT_DIMS = (((1,), (1,)), ((), ()))


def _sigmoid(x):
    return 0.5 * jnp.tanh(0.5 * x) + 0.5


def _cast_plan(srcs, grid):
    steps, strides = 1, []
    for g in reversed(grid):
        strides.insert(0, steps)
        steps *= g

    def row_block(*ids):
        return sum(i * s for i, s in zip(ids, strides))

    arrays, in_specs, out_specs, out_shapes = [], [], [], []
    for src in srcs:
        a, part, n_parts = src if isinstance(src, tuple) else (src, 0, 1)
        rows, cols = a.shape[0], a.shape[1] // n_parts
        assert rows % (steps * BF16_SUBLANES) == 0 and cols % LANES == 0, (a.shape, grid)
        arrays.append(a)
        in_specs.append(pl.BlockSpec((rows // steps, cols),
                                     lambda *ids, part=part: (row_block(*ids), part)))
        out_specs.append(pl.BlockSpec((rows // steps, cols),
                                      lambda *ids: (row_block(*ids), 0)))
        out_shapes.append(jax.ShapeDtypeStruct((rows, cols), BF16))
    return arrays, in_specs, out_specs, out_shapes


def _cast_blocks(src_refs, dst_refs):
    for src, dst in zip(src_refs, dst_refs):
        dst[...] = src[...].astype(BF16)


def _in_proj_kernel(x_hbm, gain_ref, w_ref, bias_ref, lbl_ref, qg_ref, kg_ref, *rest,
                    scale):
    out_ref, logf_ref, h_ref, x_ref, x_sem = rest
    i, j = pl.program_id(0), pl.program_id(1)
    tm = x_ref.shape[0]

    def x_copy(tile):
        return pltpu.make_async_copy(x_hbm.at[pl.ds(tile * tm, tm), :], x_ref, x_sem)

    @pl.when((j == 0) & (i == 0))
    def _first_fetch():
        x_copy(0).start()

    @pl.when(j == 0)
    def _norm():
        x_copy(i).wait()
        x = x_ref[...]
        ms = jnp.mean(x * x, axis=-1, keepdims=True)
        h_ref[...] = (x * lax.rsqrt(ms + EPS) * gain_ref[...]).astype(BF16)

    @pl.when((j == 1) & (i + 1 < pl.num_programs(0)))
    def _next_fetch():
        x_copy(i + 1).start()

    def section(epilogue):
        for c in range(w_ref.shape[1] // IN_SUB):
            cs = slice(c * IN_SUB, (c + 1) * IN_SUB)
            epilogue(jnp.dot(h_ref[...], w_ref[:, cs].astype(BF16),
                             preferred_element_type=F32), cs)

    @pl.when((j == 0) | (j == 3))
    def _silu():
        def epilogue(acc, cs):
            out_ref[:, cs] = (acc * _sigmoid(acc)).astype(BF16)
        section(epilogue)

    @pl.when(j == 1)
    def _forget():
        l = lbl_ref[...]
        e = jnp.exp(l - jnp.max(l, axis=0, keepdims=True))
        lb_row = e[0:1] / jnp.sum(e, axis=0, keepdims=True)

        def epilogue(acc, cs):
            lb = lb_row[:, cs]
            sig = _sigmoid(acc)
            lf = jnp.log2(lb + (1.0 - lb) * sig)
            hi = lf.astype(BF16)
            r1 = lf - hi.astype(F32)
            mid = r1.astype(BF16)
            tn = w_ref.shape[1]
            logf_ref[:, cs] = hi
            logf_ref[:, slice(tn + cs.start, tn + cs.stop)] = mid
            logf_ref[:, slice(2 * tn + cs.start, 2 * tn + cs.stop)] = (
                r1 - mid.astype(F32)).astype(BF16)
            out_ref[:, cs] = ((1.0 - lb) * (1.0 - sig)).astype(BF16)
        section(epilogue)

    @pl.when((j == 2) | (j == 6))
    def _plain():
        def epilogue(acc, cs):
            out_ref[:, cs] = acc.astype(BF16)
        section(epilogue)

    def head_norm(gain_ref_, mult):
        def epilogue(acc, cs):
            for h in range(IN_SUB // LANES):
                t = acc[:, h * LANES:(h + 1) * LANES]
                sl = slice(cs.start + h * LANES, cs.start + (h + 1) * LANES)
                ms = jnp.mean(t * t, axis=-1, keepdims=True)
                out_ref[:, sl] = (t * lax.rsqrt(ms + EPS) * (gain_ref_[...] * mult)).astype(BF16)
        section(epilogue)

    @pl.when(j == 4)
    def _qnorm():
        head_norm(qg_ref, scale)

    @pl.when(j == 5)
    def _knorm():
        head_norm(kg_ref, 1.0)

    @pl.when(j >= IN_FIRST_GATE_SECTION)
    def _gates():
        def epilogue(acc, cs):
            out_ref[:, cs] = _sigmoid(acc + bias_ref[:, cs]).astype(BF16)
        section(epilogue)


def _in_proj(x2, gain, w, b_gate, lb_logits, qg, kg, *, scale):
    m, d = x2.shape
    n = w.shape[1]
    tm, tn = IN_TM, IN_TN
    grid = (m // tm, n // tn)
    first_gate = (n - b_gate.shape[1]) // tn
    assert first_gate == IN_FIRST_GATE_SECTION
    return pl.pallas_call(
        functools.partial(_in_proj_kernel, scale=scale),
        out_shape=(jax.ShapeDtypeStruct((m, n), BF16),
                   jax.ShapeDtypeStruct((m, LOGF_PIECES * tn), BF16)),
        grid=grid,
        in_specs=[
            pl.BlockSpec(memory_space=pl.ANY),
            pl.BlockSpec((1, d), lambda i, j: (0, 0)),
            pl.BlockSpec((d, tn), lambda i, j: (0, j)),
            pl.BlockSpec((1, tn), lambda i, j: (0, jnp.maximum(j - first_gate, 0))),
            pl.BlockSpec(lb_logits.shape, lambda i, j: (0, 0)),
            pl.BlockSpec((1, LANES), lambda i, j: (0, 0)),
            pl.BlockSpec((1, LANES), lambda i, j: (0, 0)),
        ],
        out_specs=(pl.BlockSpec((tm, tn), lambda i, j: (i, j)),
                   pl.BlockSpec((tm, LOGF_PIECES * tn), lambda i, j: (i, 0))),
        scratch_shapes=[pltpu.VMEM((tm, d), BF16), pltpu.VMEM((tm, d), F32),
                        pltpu.SemaphoreType.DMA(())],
        compiler_params=pltpu.CompilerParams(
            dimension_semantics=("arbitrary", "arbitrary"),
            vmem_limit_bytes=VMEM_LIMIT_BYTES),
        name="in_proj",
    )(x2, gain, w, b_gate, lb_logits, qg, kg)


def _hgrn_kernel(q_ref, k_ref, v_ref, g_ref, lf_ref, gain_ref, *rest, n_cast):
    cast_in = rest[:n_cast]
    out_ref = rest[n_cast]
    cast_out = rest[n_cast + 1:2 * n_cast + 1]
    st_ref, cb_ref, sel_ref, tri_ref = rest[-4:]
    _cast_blocks(cast_in, cast_out)
    n_g = st_ref.shape[0]
    L, SUB = HG_L, HG_SUB

    @pl.when(pl.program_id(1) == 0)
    def _init():
        st_ref[...] = jnp.zeros_like(st_ref)
        cb_ref[...] = jnp.full(cb_ref.shape, BIG_EXPONENT, F32)
        r = lax.broadcasted_iota(jnp.int32, sel_ref.shape, 0)
        c = lax.broadcasted_iota(jnp.int32, sel_ref.shape, 1)
        sel_ref[...] = jnp.where(c == SUB - 1 - r // LANES, 1.0, 0.0).astype(BF16)
        row = lax.broadcasted_iota(jnp.int32, (L, L), 0)
        col = lax.broadcasted_iota(jnp.int32, (L, L), 1)
        tri_ref[...] = jnp.where(row >= col, 1.0, 0.0).astype(BF16)

    heads = [slice(g * LANES, (g + 1) * LANES) for g in range(n_g)]
    gw = n_g * LANES
    state = [st_ref[g] for g in range(n_g)]
    chunks = []
    for ch in range(HG_CHUNKS):
        rows = slice(ch * L, (ch + 1) * L)
        state, ctx = _hgrn_front(rows, state, heads, gw, q_ref, k_ref, v_ref, lf_ref,
                                 tri_ref)
        chunks.append((rows, ch, ctx))
    for g in range(n_g):
        st_ref[g] = state[g]

    d_min = functools.reduce(jnp.minimum, [jnp.min(ctx.d) for _, _, ctx in chunks])
    fast = d_min >= -HG_FAST_RANGE

    @pl.when(fast)
    def _fast():
        _hgrn_back_fast(chunks, heads, g_ref, gain_ref, out_ref)

    @pl.when(jnp.logical_not(fast))
    def _exact():
        for rows, ch, ctx in chunks:
            _hgrn_back_exact(rows, ch, ctx, heads, g_ref, gain_ref, out_ref, cb_ref, sel_ref)


_HgrnChunk = collections.namedtuple("_HgrnChunk", "q k b d v_bf o_blocks s_far qn")


def _hgrn_front(rows, state, heads, gw, q_ref, k_ref, v_ref, lf_ref, tri_ref):
    L, SUB = HG_L, HG_SUB

    b = sum(jnp.dot(tri_ref[...], lf_ref[0, rows, p * gw:(p + 1) * gw],
                    preferred_element_type=F32) for p in range(LOGF_PIECES))
    d = b - jnp.concatenate([jnp.broadcast_to(b[r0:r0 + 1], (SUB, gw))
                             for r0 in range(0, L, SUB)], axis=0)
    q = q_ref[0, rows, :].astype(F32)
    k = k_ref[0, rows, :].astype(F32)
    v_bf = v_ref[0, rows, :]
    v = v_bf.astype(F32)

    qe = (q * jnp.exp2(b)).astype(BF16)
    o_blocks = [[None] * (L // SUB) for _ in heads]
    for g, sl in enumerate(heads):
        o = lax.dot_general(qe[:, sl], state[g].astype(BF16), NT_DIMS,
                            preferred_element_type=F32)
        for i in range(L // SUB):
            o_blocks[g][i] = o[i * SUB:(i + 1) * SUB]

    qn = (q * jnp.exp2(d)).astype(BF16)
    s_far = {}
    for i in range(1, L // SUB):
        r0 = i * SUB
        kt = (k[0:r0] * jnp.exp2(b[r0:r0 + 1] - b[0:r0])).astype(BF16)
        for g, sl in enumerate(heads):
            s_far[i, g] = lax.dot_general(qn[r0:r0 + SUB, sl], kt[:, sl], NT_DIMS,
                                          preferred_element_type=F32)

    b_last = b[L - 1:L]
    kdec = (k * jnp.exp2(b_last - b)).astype(BF16)
    eb_last = jnp.exp2(b_last)
    new_state = []
    for g, sl in enumerate(heads):
        upd = jnp.dot(jnp.transpose(v[:, sl]).astype(BF16), kdec[:, sl],
                      preferred_element_type=F32)
        new_state.append(state[g] * eb_last[:, sl] + upd)
    return new_state, _HgrnChunk(q, k, b, d, v_bf, o_blocks, s_far, qn)


def _hgrn_far(o_blocks, s_far, v_bf, heads, min_distance):
    L, SUB = HG_L, HG_SUB
    out = [list(blocks) for blocks in o_blocks]
    for i in range(1, L // SUB):
        r0 = i * SUB
        keep = (lax.broadcasted_iota(jnp.int32, (SUB, r0), 1) + min_distance
                <= lax.broadcasted_iota(jnp.int32, (SUB, r0), 0) + r0)
        for g, sl in enumerate(heads):
            s = s_far[i, g] if min_distance <= 1 else jnp.where(keep, s_far[i, g], 0.0)
            oi = jnp.dot(s.astype(BF16), v_bf[0:r0, sl], preferred_element_type=F32)
            out[g][i] = out[g][i] + oi
    return out


def _hgrn_finish(o, rows, sl, g_ref, gain_ref, out_ref):
    ms = jnp.mean(o * o, axis=-1, keepdims=True)
    y = o * lax.rsqrt(ms + EPS) * gain_ref[:, sl] * g_ref[0, rows, sl].astype(F32)
    out_ref[0, rows, sl] = y.astype(BF16)


def _hgrn_back_fast(chunks, heads, g_ref, gain_ref, out_ref):
    L, SUB = HG_L, HG_SUB
    row = lax.broadcasted_iota(jnp.int32, (L, L), 0)
    col = lax.broadcasted_iota(jnp.int32, (L, L), 1)
    same = (col <= row) & (col >= (row // SUB) * SUB)
    s_near = []
    for _, _, ctx in chunks:
        kn = (ctx.k * jnp.exp2(-ctx.d)).astype(BF16)
        s_near.append([lax.dot_general(ctx.qn[:, sl], kn[:, sl], NT_DIMS,
                                       preferred_element_type=F32) for sl in heads])
    far = [_hgrn_far(ctx.o_blocks, ctx.s_far, ctx.v_bf, heads, min_distance=1)
           for _, _, ctx in chunks]
    near = [[jnp.dot(jnp.where(same, s_near[n][g], 0.0).astype(BF16), ctx.v_bf[:, sl],
                     preferred_element_type=F32) for g, sl in enumerate(heads)]
            for n, (_, _, ctx) in enumerate(chunks)]
    for n, (rows, _, _) in enumerate(chunks):
        for g, sl in enumerate(heads):
            _hgrn_finish(jnp.concatenate(far[n][g], axis=0) + near[n][g], rows, sl,
                         g_ref, gain_ref, out_ref)


def _hgrn_back_exact(rows, ch, ctx, heads, g_ref, gain_ref, out_ref, cb_ref, sel_ref):
    L, SUB = HG_L, HG_SUB
    q, k, b, v_bf = ctx.q, ctx.k, ctx.b, ctx.v_bf
    o_blocks = _hgrn_far(ctx.o_blocks, ctx.s_far, v_bf, heads, min_distance=SUB)

    c = b - jnp.log2(k)
    for g, sl in enumerate(heads):
        cb_ref[ch, g, SUB:SUB + L, :] = c[:, sl]

    near = []
    for g, sl in enumerate(heads):
        qg, bg = q[:, sl], b[:, sl]
        e = [(qg * jnp.exp2(bg - cb_ref[ch, g, SUB - off:SUB - off + L, :])).astype(BF16)
             for off in range(SUB)]
        near.append(jnp.dot(jnp.concatenate(e, axis=1), sel_ref[...],
                            preferred_element_type=F32))

    for g, sl in enumerate(heads):
        s = pltpu.roll(near[g], LANES - (SUB - 1), axis=1, stride=1, stride_axis=0)
        o = jnp.concatenate(o_blocks[g], axis=0) + jnp.dot(
            s[:, 0:L].astype(BF16), v_bf[:, sl], preferred_element_type=F32)
        _hgrn_finish(o, rows, sl, g_ref, gain_ref, out_ref)


def _hgrn(proj3, logf3, gain, cast_srcs, *, d_a):
    bsz, t, _ = proj3.shape
    n_g = d_a // LANES
    rows = HG_CHUNKS * HG_L
    grid = (bsz, t // rows)
    cast_arrays, cast_in_specs, cast_out_specs, cast_out_shapes = _cast_plan(cast_srcs, grid)

    def sec(s):
        return pl.BlockSpec((1, rows, d_a), lambda b, c, s=s: (b, c, s))

    return pl.pallas_call(
        functools.partial(_hgrn_kernel, n_cast=len(cast_srcs)),
        out_shape=(jax.ShapeDtypeStruct((bsz, t, d_a), BF16), *cast_out_shapes),
        grid=grid,
        in_specs=[sec(0), sec(1), sec(2), sec(3),
                  pl.BlockSpec((1, rows, LOGF_PIECES * d_a), lambda b, c: (b, c, 0)),
                  pl.BlockSpec((1, d_a), lambda b, c: (0, 0)),
                  *cast_in_specs],
        out_specs=(pl.BlockSpec((1, rows, d_a), lambda b, c: (b, c, 0)), *cast_out_specs),
        scratch_shapes=[pltpu.VMEM((n_g, LANES, LANES), F32),
                        pltpu.VMEM((HG_CHUNKS, n_g, HG_L + HG_SUB, LANES), F32),
                        pltpu.VMEM((HG_SUB * LANES, LANES), BF16),
                        pltpu.VMEM((HG_L, HG_L), BF16)],
        compiler_params=pltpu.CompilerParams(
            dimension_semantics=("arbitrary", "arbitrary")),
        name="hgrn",
    )(proj3, proj3, proj3, proj3, logf3, gain, *cast_arrays)


def _attn_kernel(q_ref, k_ref, v_ref, gvec_ref, *rest, n_cast):
    cast_in = rest[:n_cast]
    out_ref = rest[n_cast]
    cast_out = rest[n_cast + 1:2 * n_cast + 1]
    band_ref, bias_ref, vx_ref = rest[-3:]
    _cast_blocks(cast_in, cast_out)
    t = q_ref.shape[1]
    nq = AT_Q // CHUNK
    nk = nq + N_PAST_CHUNKS
    n_tb = REL_PAST // CHUNK + 2
    wb = (nk + nq - 1) * CHUNK

    @pl.when(pl.program_id(1) == 0)
    def _build_bias():
        grow = gvec_ref[0] * LOG2E
        xb = jnp.broadcast_to(grow, (CHUNK, 2 * LANES))
        r = lax.broadcasted_iota(jnp.int32, (CHUNK, 2 * LANES), 0)
        for bit in range(6):
            xb = jnp.where(((r >> bit) & 1) == 1, pltpu.roll(xb, 1 << bit, axis=1), xb)
        const = grow[:, 0:1]

        n_const = nk - nq + 1 - n_tb
        lo = (nq - 1) * CHUNK
        band_ref[:, 0:lo] = jnp.full((CHUNK, lo), MASK_VALUE, F32)
        band_ref[:, lo:lo + n_const * CHUNK] = jnp.broadcast_to(const, (CHUNK, n_const * CHUNK))
        band_ref[:, lo + n_const * CHUNK:lo + (n_const + n_tb) * CHUNK] = xb[:, CHUNK:]
        band_ref[:, wb - lo:wb] = jnp.full((CHUNK, lo), MASK_VALUE, F32)
        for qi in range(nq):
            off = (nq - 1 - qi) * CHUNK
            bias_ref[qi * CHUNK:(qi + 1) * CHUNK, :] = band_ref[:, off:off + nk * CHUNK]

    vx_ref[:, 0:LANES] = v_ref[0]
    vx_ref[:, LANES:2 * LANES] = jnp.ones((t, LANES), BF16)

    def window(g):
        q0 = g * AT_Q
        ks = max(0, q0 - N_PAST_CHUNKS * CHUNK)
        return q0, ks, q0 + AT_Q - ks

    def scores(g):
        q0, ks, kw = window(g)
        return lax.dot_general(q_ref[0, q0:q0 + AT_Q, :], k_ref[0, ks:ks + kw, :], NT_DIMS,
                               preferred_element_type=F32)

    n_groups = t // AT_Q
    s_next = scores(0)
    for g in range(n_groups):
        q0, ks, kw = window(g)
        s = s_next + bias_ref[:, nk * CHUNK - kw:nk * CHUNK]
        if g + 1 < n_groups:
            s_next = scores(g + 1)
        p = jnp.exp2(s - jnp.max(s, axis=-1, keepdims=True))
        ox = jnp.dot(p.astype(BF16), vx_ref[ks:ks + kw, :], preferred_element_type=F32)
        out_ref[0, q0:q0 + AT_Q, :] = (ox[:, 0:LANES] / ox[:, LANES:2 * LANES]).astype(BF16)


def _attn(proj3, gvec, cast_srcs, *, col0, n_heads):
    bsz, t, _ = proj3.shape
    d_b = n_heads * LANES
    nq = AT_Q // CHUNK
    nk = nq + N_PAST_CHUNKS
    blk0 = col0 // LANES
    grid = (n_heads, bsz)
    cast_arrays, cast_in_specs, cast_out_specs, cast_out_shapes = _cast_plan(cast_srcs, grid)

    def sec(s):
        return pl.BlockSpec((1, t, LANES), lambda h, b, s=s: (b, 0, blk0 + s * n_heads + h))

    return pl.pallas_call(
        functools.partial(_attn_kernel, n_cast=len(cast_srcs)),
        out_shape=(jax.ShapeDtypeStruct((bsz, t, d_b), BF16), *cast_out_shapes),
        grid=grid,
        in_specs=[sec(0), sec(1), sec(2),
                  pl.BlockSpec((1, 1, 2 * LANES), lambda h, b: (h, 0, 0)),
                  *cast_in_specs],
        out_specs=(pl.BlockSpec((1, t, LANES), lambda h, b: (b, 0, h)), *cast_out_specs),
        scratch_shapes=[pltpu.VMEM((CHUNK, (nk + nq - 1) * CHUNK), F32),
                        pltpu.VMEM((AT_Q, nk * CHUNK), F32),
                        pltpu.VMEM((t, 2 * LANES), BF16)],
        compiler_params=pltpu.CompilerParams(
            dimension_semantics=("arbitrary", "arbitrary")),
        name="attn",
    )(proj3, proj3, proj3, gvec, *cast_arrays)


def _merge_kernel(ya_ref, yb_ref, ga0_ref, ga1_ref, gb0_ref, gb1_ref, x_ref,
                  wa_ref, wb_ref, wo_ref, gain_ref, x1_ref, h2_ref):
    pa = jnp.dot(ya_ref[...], wa_ref[...], preferred_element_type=F32)
    pb = jnp.dot(yb_ref[...], wb_ref[...], preferred_element_type=F32)
    ga = jnp.concatenate([ga0_ref[...], ga1_ref[...]], axis=1).astype(F32)
    gb = jnp.concatenate([gb0_ref[...], gb1_ref[...]], axis=1).astype(F32)
    merged = (ga * pa + gb * pb).astype(BF16)
    x1 = x_ref[...] + jnp.dot(merged, wo_ref[...], preferred_element_type=F32)
    x1_ref[...] = x1
    ms = jnp.mean(x1 * x1, axis=-1, keepdims=True)
    h2_ref[...] = (x1 * lax.rsqrt(ms + EPS) * gain_ref[...]).astype(BF16)


def _merge(ya, yb, proj, x2, wa, wb, wo, gain, *, gate_col0):
    m, d = x2.shape
    da, db = ya.shape[1], yb.shape[1]
    tm = MG_TM
    gblk = gate_col0 // IN_TN

    def gate(s):
        return pl.BlockSpec((tm, IN_TN), lambda i, s=s: (i, gblk + s))

    def whole(a):
        return pl.BlockSpec(a.shape, lambda i: (0, 0), pipeline_mode=pl.Buffered(1))

    return pl.pallas_call(
        _merge_kernel,
        out_shape=(jax.ShapeDtypeStruct((m, d), F32),
                   jax.ShapeDtypeStruct((m, d), BF16)),
        grid=(m // tm,),
        in_specs=[pl.BlockSpec((tm, da), lambda i: (i, 0)),
                  pl.BlockSpec((tm, db), lambda i: (i, 0)),
                  gate(0), gate(1), gate(2), gate(3),
                  pl.BlockSpec((tm, d), lambda i: (i, 0)),
                  whole(wa), whole(wb), whole(wo), whole(gain)],
        out_specs=(pl.BlockSpec((tm, d), lambda i: (i, 0)),
                   pl.BlockSpec((tm, d), lambda i: (i, 0))),
        compiler_params=pltpu.CompilerParams(
            dimension_semantics=("arbitrary",),
            vmem_limit_bytes=VMEM_LIMIT_BYTES),
        name="merge",
    )(ya, yb, proj, proj, proj, proj, x2, wa, wb, wo, gain)


def _ffn_kernel(h_ref, wg_ref, wu_ref, wd_ref, x1_hbm, out_ref, x1_ref, sem):
    i, f = pl.program_id(0), pl.program_id(1)
    tm = out_ref.shape[0]

    def residual_copy():
        return pltpu.make_async_copy(x1_hbm.at[pl.ds(i * tm, tm), :], x1_ref, sem)

    def hidden_tile():
        h = h_ref[...]
        gate = jnp.dot(h, wg_ref[...], preferred_element_type=F32)
        up = jnp.dot(h, wu_ref[...], preferred_element_type=F32)
        act = (gate * _sigmoid(gate) * up).astype(BF16)
        return jnp.dot(act, wd_ref[...].astype(BF16), preferred_element_type=F32)

    last = pl.num_programs(1) - 1

    @pl.when(f == 0)
    def _first():
        residual_copy().start()
        out_ref[...] = hidden_tile()

    @pl.when((f > 0) & (f < last))
    def _middle():
        out_ref[...] += hidden_tile()

    @pl.when(f == last)
    def _last():
        residual_copy().wait()
        out_ref[...] += hidden_tile() + x1_ref[...]


def _ffn(h2, x1, w_gate, w_up, w_down):
    m, d = h2.shape
    dff = w_down.shape[0]
    tm, tf = FF_TM, FF_TF
    nf = dff // tf
    return pl.pallas_call(
        _ffn_kernel,
        out_shape=jax.ShapeDtypeStruct((m, d), F32),
        grid=(m // tm, nf),
        in_specs=[pl.BlockSpec((tm, d), lambda i, f: (i, 0)),
                  pl.BlockSpec((d, tf), lambda i, f: (0, f)),
                  pl.BlockSpec((d, tf), lambda i, f: (0, f)),
                  pl.BlockSpec((tf, d), lambda i, f: (f, 0)),
                  pl.BlockSpec(memory_space=pl.ANY)],
        out_specs=pl.BlockSpec((tm, d), lambda i, f: (i, 0)),
        scratch_shapes=[pltpu.VMEM((tm, d), F32), pltpu.SemaphoreType.DMA(())],
        compiler_params=pltpu.CompilerParams(
            dimension_semantics=("arbitrary", "arbitrary"),
            vmem_limit_bytes=VMEM_LIMIT_BYTES),
        name="ffn",
    )(h2, w_gate, w_up, w_down, x1)


def _bias_vector(rel_bias):
    assert rel_bias.shape[1] == N_REL and N_REL + CHUNK <= 2 * LANES
    rev = rel_bias[:, ::-1]
    pad = 2 * LANES - N_REL
    return jnp.pad(rev, ((0, 0), (pad, 0)), mode="edge")[:, None, :]


def kernel(x, w_in, b_gate, norm_mix, norm_ffn, hgrn_lb_logits, hgrn_out_gain,
           q_gain, k_gain, rel_bias, w_proj_a, w_proj_b, w_out, w_ffn_in, w_ffn_out):
    bsz, t, d = x.shape
    depth = w_in.shape[0]
    d_a = hgrn_out_gain.shape[1]
    dh = q_gain.shape[1]
    n_heads_b = rel_bias.shape[1]
    d_b = n_heads_b * dh
    n_in = w_in.shape[2]
    gate_col0 = 4 * d_a + 3 * d_b
    assert dh == LANES and d_a == IN_TN and d_b == IN_TN and depth == 1
    assert hgrn_lb_logits.shape[0] == depth + 1
    assert n_in == gate_col0 + 2 * d and t % AT_Q == 0 and t % (HG_CHUNKS * HG_L) == 0
    assert (bsz * t) % IN_TM == 0 and (bsz * t) % FF_TM == 0 and (bsz * t) % MG_TM == 0

    m = bsz * t
    x2 = x.reshape(m, d)
    for l in range(depth):
        proj, logf = _in_proj(
            x2, norm_mix[l][None, :], w_in[l], b_gate[l][None, :], hgrn_lb_logits,
            q_gain[l][None, :], k_gain[l][None, :], scale=dh ** -0.5 * LOG2E)
        proj3 = proj.reshape(bsz, t, n_in)
        y_a, w_gate_bf = _hgrn(
            proj3, logf.reshape(bsz, t, LOGF_PIECES * d_a), hgrn_out_gain[l][None, :],
            ((w_ffn_in[l], 0, 2),), d_a=d_a)
        y_b, wa_bf, wb_bf, wo_bf, w_up_bf = _attn(
            proj3, _bias_vector(rel_bias[l]),
            (w_proj_a[l], w_proj_b[l], w_out[l], (w_ffn_in[l], 1, 2)),
            col0=4 * d_a, n_heads=n_heads_b)
        x1, h2 = _merge(y_a.reshape(m, d_a), y_b.reshape(m, d_b), proj, x2,
                        wa_bf, wb_bf, wo_bf, norm_ffn[l][None, :], gate_col0=gate_col0)
        x2 = _ffn(h2, x1, w_gate_bf, w_up_bf, w_ffn_out[l])
    return x2.reshape(bsz, t, d)
```

```python
import collections
import functools

import jax
import jax.numpy as jnp
from jax import lax
from jax.experimental import pallas as pl
from jax.experimental.pallas import tpu as pltpu

F32 = jnp.float32
BF16 = jnp.bfloat16

EPS = 1e-6
LANES = 128
BF16_SUBLANES = 16
VMEM_LIMIT_BYTES = 56 * 1024 * 1024
CHUNK = 64
N_PAST_CHUNKS = 8
REL_FUTURE = CHUNK - 1
REL_PAST = 2 * CHUNK - 1
N_REL = REL_FUTURE + REL_PAST + 1
MASK_VALUE = -1e30
BIG_EXPONENT = 1e30
LOGF_PIECES = 3
LOG2E = 1.4426950408889634

IN_TM = 1024
IN_TN = 1024
IN_SUB = 256
IN_W_RING = 3
IN_FIRST_GATE_SECTION = 7
HG_L = 128
HG_CHUNKS = 2
HG_FAST_RANGE = 100.0
HG_SUB = 32
AT_Q = 4 * CHUNK
MG_TM = 512
FF_TM = 1024
FF_TF = 512

NT_DIMS = (((1,), (1,)), ((), ()))


def _sigmoid(x):
    return 0.5 * jnp.tanh(0.5 * x) + 0.5


def _cast_plan(srcs, grid):
    steps, strides = 1, []
    for g in reversed(grid):
        strides.insert(0, steps)
        steps *= g

    def row_block(*ids):
        return sum(i * s for i, s in zip(ids, strides))

    arrays, in_specs, out_specs, out_shapes = [], [], [], []
    for src in srcs:
        a, part, n_parts = src if isinstance(src, tuple) else (src, 0, 1)
        rows, cols = a.shape[0], a.shape[1] // n_parts
        assert rows % (steps * BF16_SUBLANES) == 0 and cols % LANES == 0, (a.shape, grid)
        arrays.append(a)
        in_specs.append(pl.BlockSpec((rows // steps, cols),
                                     lambda *ids, part=part: (row_block(*ids), part)))
        out_specs.append(pl.BlockSpec((rows // steps, cols),
                                      lambda *ids: (row_block(*ids), 0)))
        out_shapes.append(jax.ShapeDtypeStruct((rows, cols), BF16))
    return arrays, in_specs, out_specs, out_shapes


def _cast_blocks(src_refs, dst_refs):
    for src, dst in zip(src_refs, dst_refs):
        dst[...] = src[...].astype(BF16)


def _in_proj_kernel(x_hbm, gain_ref, w_hbm, bias_ref, lbl_ref, qg_ref, kg_ref, *rest,
                    scale):
    out_ref, logf_ref, h_ref, x_ref, wbuf_ref, x_sem, w_sems = rest
    i, j = pl.program_id(0), pl.program_id(1)
    tm = x_ref.shape[0]
    tn = wbuf_ref.shape[2]

    n_j = pl.num_programs(1)
    step = i * n_j + j

    def w_copy(s):
        slot = lax.rem(s, IN_W_RING)
        col = pl.multiple_of(lax.rem(s, n_j) * tn, tn)
        return pltpu.make_async_copy(w_hbm.at[:, pl.ds(col, tn)], wbuf_ref.at[slot],
                                     w_sems.at[slot])

    @pl.when(step == 0)
    def _prime():
        for s in range(IN_W_RING - 1):
            w_copy(s).start()

    w_copy(step).wait()

    @pl.when(step + IN_W_RING - 1 < pl.num_programs(0) * n_j)
    def _request():
        w_copy(step + IN_W_RING - 1).start()

    w_ref = wbuf_ref.at[lax.rem(step, IN_W_RING)]

    def x_copy(tile):
        return pltpu.make_async_copy(x_hbm.at[pl.ds(tile * tm, tm), :], x_ref, x_sem)

    @pl.when((j == 0) & (i == 0))
    def _first_fetch():
        x_copy(0).start()

    @pl.when(j == 0)
    def _norm():
        x_copy(i).wait()
        x = x_ref[...]
        ms = jnp.mean(x * x, axis=-1, keepdims=True)
        h_ref[...] = (x * lax.rsqrt(ms + EPS) * gain_ref[...]).astype(BF16)

    @pl.when((j == 1) & (i + 1 < pl.num_programs(0)))
    def _next_fetch():
        x_copy(i + 1).start()

    def section(epilogue):
        for c in range(w_ref.shape[1] // IN_SUB):
            cs = slice(c * IN_SUB, (c + 1) * IN_SUB)
            epilogue(jnp.dot(h_ref[...], w_ref[:, cs].astype(BF16),
                             preferred_element_type=F32), cs)

    @pl.when((j == 0) | (j == 3))
    def _silu():
        def epilogue(acc, cs):
            out_ref[:, cs] = (acc * _sigmoid(acc)).astype(BF16)
        section(epilogue)

    @pl.when(j == 1)
    def _forget():
        l = lbl_ref[...]
        e = jnp.exp(l - jnp.max(l, axis=0, keepdims=True))
        lb_row = e[0:1] / jnp.sum(e, axis=0, keepdims=True)

        def epilogue(acc, cs):
            lb = lb_row[:, cs]
            sig = _sigmoid(acc)
            lf = jnp.log2(lb + (1.0 - lb) * sig)
            hi = lf.astype(BF16)
            r1 = lf - hi.astype(F32)
            mid = r1.astype(BF16)
            tn = w_ref.shape[1]
            logf_ref[:, cs] = hi
            logf_ref[:, slice(tn + cs.start, tn + cs.stop)] = mid
            logf_ref[:, slice(2 * tn + cs.start, 2 * tn + cs.stop)] = (
                r1 - mid.astype(F32)).astype(BF16)
            out_ref[:, cs] = ((1.0 - lb) * (1.0 - sig)).astype(BF16)
        section(epilogue)

    @pl.when((j == 2) | (j == 6))
    def _plain():
        def epilogue(acc, cs):
            out_ref[:, cs] = acc.astype(BF16)
        section(epilogue)

    def head_norm(gain_ref_, mult):
        def epilogue(acc, cs):
            for h in range(IN_SUB // LANES):
                t = acc[:, h * LANES:(h + 1) * LANES]
                sl = slice(cs.start + h * LANES, cs.start + (h + 1) * LANES)
                ms = jnp.mean(t * t, axis=-1, keepdims=True)
                out_ref[:, sl] = (t * lax.rsqrt(ms + EPS) * (gain_ref_[...] * mult)).astype(BF16)
        section(epilogue)

    @pl.when(j == 4)
    def _qnorm():
        head_norm(qg_ref, scale)

    @pl.when(j == 5)
    def _knorm():
        head_norm(kg_ref, 1.0)

    @pl.when(j >= IN_FIRST_GATE_SECTION)
    def _gates():
        def epilogue(acc, cs):
            out_ref[:, cs] = _sigmoid(acc + bias_ref[:, cs]).astype(BF16)
        section(epilogue)


def _in_proj(x2, gain, w, b_gate, lb_logits, qg, kg, *, scale):
    m, d = x2.shape
    n = w.shape[1]
    tm, tn = IN_TM, IN_TN
    grid = (m // tm, n // tn)
    first_gate = (n - b_gate.shape[1]) // tn
    assert first_gate == IN_FIRST_GATE_SECTION
    return pl.pallas_call(
        functools.partial(_in_proj_kernel, scale=scale),
        out_shape=(jax.ShapeDtypeStruct((m, n), BF16),
                   jax.ShapeDtypeStruct((m, LOGF_PIECES * tn), BF16)),
        grid=grid,
        in_specs=[
            pl.BlockSpec(memory_space=pl.ANY),
            pl.BlockSpec((1, d), lambda i, j: (0, 0)),
            pl.BlockSpec(memory_space=pl.ANY),
            pl.BlockSpec((1, tn), lambda i, j: (0, jnp.maximum(j - first_gate, 0))),
            pl.BlockSpec(lb_logits.shape, lambda i, j: (0, 0)),
            pl.BlockSpec((1, LANES), lambda i, j: (0, 0)),
            pl.BlockSpec((1, LANES), lambda i, j: (0, 0)),
        ],
        out_specs=(pl.BlockSpec((tm, tn), lambda i, j: (i, j)),
                   pl.BlockSpec((tm, LOGF_PIECES * tn), lambda i, j: (i, 0))),
        scratch_shapes=[pltpu.VMEM((tm, d), BF16), pltpu.VMEM((tm, d), F32),
                        pltpu.VMEM((IN_W_RING, d, tn), F32),
                        pltpu.SemaphoreType.DMA(()), pltpu.SemaphoreType.DMA((IN_W_RING,))],
        compiler_params=pltpu.CompilerParams(
            dimension_semantics=("arbitrary", "arbitrary"),
            vmem_limit_bytes=VMEM_LIMIT_BYTES),
        name="in_proj",
    )(x2, gain, w, b_gate, lb_logits, qg, kg)


def _hgrn_kernel(q_ref, k_ref, v_ref, g_ref, lf_ref, gain_ref, *rest, n_cast):
    cast_in = rest[:n_cast]
    out_ref = rest[n_cast]
    cast_out = rest[n_cast + 1:2 * n_cast + 1]
    st_ref, cb_ref, sel_ref, tri_ref = rest[-4:]
    _cast_blocks(cast_in, cast_out)
    n_g = st_ref.shape[0]
    L, SUB = HG_L, HG_SUB

    @pl.when(pl.program_id(1) == 0)
    def _init():
        st_ref[...] = jnp.zeros_like(st_ref)
        cb_ref[...] = jnp.full(cb_ref.shape, BIG_EXPONENT, F32)
        r = lax.broadcasted_iota(jnp.int32, sel_ref.shape, 0)
        c = lax.broadcasted_iota(jnp.int32, sel_ref.shape, 1)
        sel_ref[...] = jnp.where(c == SUB - 1 - r // LANES, 1.0, 0.0).astype(BF16)
        row = lax.broadcasted_iota(jnp.int32, (L, L), 0)
        col = lax.broadcasted_iota(jnp.int32, (L, L), 1)
        tri_ref[...] = jnp.where(row >= col, 1.0, 0.0).astype(BF16)

    heads = [slice(g * LANES, (g + 1) * LANES) for g in range(n_g)]
    gw = n_g * LANES
    state = [st_ref[g] for g in range(n_g)]
    chunks = []
    for ch in range(HG_CHUNKS):
        rows = slice(ch * L, (ch + 1) * L)
        state, ctx = _hgrn_front(rows, state, heads, gw, q_ref, k_ref, v_ref, lf_ref,
                                 tri_ref)
        chunks.append((rows, ch, ctx))
    for g in range(n_g):
        st_ref[g] = state[g]

    d_min = functools.reduce(jnp.minimum, [jnp.min(ctx.d) for _, _, ctx in chunks])
    fast = d_min >= -HG_FAST_RANGE

    @pl.when(fast)
    def _fast():
        _hgrn_back_fast(chunks, heads, g_ref, gain_ref, out_ref)

    @pl.when(jnp.logical_not(fast))
    def _exact():
        for rows, ch, ctx in chunks:
            _hgrn_back_exact(rows, ch, ctx, heads, g_ref, gain_ref, out_ref, cb_ref, sel_ref)


_HgrnChunk = collections.namedtuple("_HgrnChunk", "q k b d v_bf o_blocks s_far qn")


def _hgrn_front(rows, state, heads, gw, q_ref, k_ref, v_ref, lf_ref, tri_ref):
    L, SUB = HG_L, HG_SUB

    b = sum(jnp.dot(tri_ref[...], lf_ref[0, rows, p * gw:(p + 1) * gw],
                    preferred_element_type=F32) for p in range(LOGF_PIECES))
    d = b - jnp.concatenate([jnp.broadcast_to(b[r0:r0 + 1], (SUB, gw))
                             for r0 in range(0, L, SUB)], axis=0)
    q = q_ref[0, rows, :].astype(F32)
    k = k_ref[0, rows, :].astype(F32)
    v_bf = v_ref[0, rows, :]
    v = v_bf.astype(F32)

    qe = (q * jnp.exp2(b)).astype(BF16)
    o_blocks = [[None] * (L // SUB) for _ in heads]
    for g, sl in enumerate(heads):
        o = lax.dot_general(qe[:, sl], state[g].astype(BF16), NT_DIMS,
                            preferred_element_type=F32)
        for i in range(L // SUB):
            o_blocks[g][i] = o[i * SUB:(i + 1) * SUB]

    qn = (q * jnp.exp2(d)).astype(BF16)
    s_far = {}
    for i in range(1, L // SUB):
        r0 = i * SUB
        kt = (k[0:r0] * jnp.exp2(b[r0:r0 + 1] - b[0:r0])).astype(BF16)
        for g, sl in enumerate(heads):
            s_far[i, g] = lax.dot_general(qn[r0:r0 + SUB, sl], kt[:, sl], NT_DIMS,
                                          preferred_element_type=F32)

    b_last = b[L - 1:L]
    kdec = (k * jnp.exp2(b_last - b)).astype(BF16)
    eb_last = jnp.exp2(b_last)
    new_state = []
    for g, sl in enumerate(heads):
        upd = jnp.dot(jnp.transpose(v[:, sl]).astype(BF16), kdec[:, sl],
                      preferred_element_type=F32)
        new_state.append(state[g] * eb_last[:, sl] + upd)
    return new_state, _HgrnChunk(q, k, b, d, v_bf, o_blocks, s_far, qn)


def _hgrn_far(o_blocks, s_far, v_bf, heads, min_distance):
    L, SUB = HG_L, HG_SUB
    out = [list(blocks) for blocks in o_blocks]
    for i in range(1, L // SUB):
        r0 = i * SUB
        keep = (lax.broadcasted_iota(jnp.int32, (SUB, r0), 1) + min_distance
                <= lax.broadcasted_iota(jnp.int32, (SUB, r0), 0) + r0)
        for g, sl in enumerate(heads):
            s = s_far[i, g] if min_distance <= 1 else jnp.where(keep, s_far[i, g], 0.0)
            oi = jnp.dot(s.astype(BF16), v_bf[0:r0, sl], preferred_element_type=F32)
            out[g][i] = out[g][i] + oi
    return out


def _hgrn_finish(o, rows, sl, g_ref, gain_ref, out_ref):
    ms = jnp.mean(o * o, axis=-1, keepdims=True)
    y = o * lax.rsqrt(ms + EPS) * gain_ref[:, sl] * g_ref[0, rows, sl].astype(F32)
    out_ref[0, rows, sl] = y.astype(BF16)


def _hgrn_back_fast(chunks, heads, g_ref, gain_ref, out_ref):
    L, SUB = HG_L, HG_SUB
    row = lax.broadcasted_iota(jnp.int32, (L, L), 0)
    col = lax.broadcasted_iota(jnp.int32, (L, L), 1)
    same = (col <= row) & (col >= (row // SUB) * SUB)
    s_near = []
    for _, _, ctx in chunks:
        kn = (ctx.k * jnp.exp2(-ctx.d)).astype(BF16)
        s_near.append([lax.dot_general(ctx.qn[:, sl], kn[:, sl], NT_DIMS,
                                       preferred_element_type=F32) for sl in heads])
    far = [_hgrn_far(ctx.o_blocks, ctx.s_far, ctx.v_bf, heads, min_distance=1)
           for _, _, ctx in chunks]
    near = [[jnp.dot(jnp.where(same, s_near[n][g], 0.0).astype(BF16), ctx.v_bf[:, sl],
                     preferred_element_type=F32) for g, sl in enumerate(heads)]
            for n, (_, _, ctx) in enumerate(chunks)]
    for n, (rows, _, _) in enumerate(chunks):
        for g, sl in enumerate(heads):
            _hgrn_finish(jnp.concatenate(far[n][g], axis=0) + near[n][g], rows, sl,
                         g_ref, gain_ref, out_ref)


def _hgrn_back_exact(rows, ch, ctx, heads, g_ref, gain_ref, out_ref, cb_ref, sel_ref):
    L, SUB = HG_L, HG_SUB
    q, k, b, v_bf = ctx.q, ctx.k, ctx.b, ctx.v_bf
    o_blocks = _hgrn_far(ctx.o_blocks, ctx.s_far, v_bf, heads, min_distance=SUB)

    c = b - jnp.log2(k)
    for g, sl in enumerate(heads):
        cb_ref[ch, g, SUB:SUB + L, :] = c[:, sl]

    near = []
    for g, sl in enumerate(heads):
        qg, bg = q[:, sl], b[:, sl]
        e = [(qg * jnp.exp2(bg - cb_ref[ch, g, SUB - off:SUB - off + L, :])).astype(BF16)
             for off in range(SUB)]
        near.append(jnp.dot(jnp.concatenate(e, axis=1), sel_ref[...],
                            preferred_element_type=F32))

    for g, sl in enumerate(heads):
        s = pltpu.roll(near[g], LANES - (SUB - 1), axis=1, stride=1, stride_axis=0)
        o = jnp.concatenate(o_blocks[g], axis=0) + jnp.dot(
            s[:, 0:L].astype(BF16), v_bf[:, sl], preferred_element_type=F32)
        _hgrn_finish(o, rows, sl, g_ref, gain_ref, out_ref)


def _hgrn(proj3, logf3, gain, cast_srcs, *, d_a):
    bsz, t, _ = proj3.shape
    n_g = d_a // LANES
    rows = HG_CHUNKS * HG_L
    grid = (bsz, t // rows)
    cast_arrays, cast_in_specs, cast_out_specs, cast_out_shapes = _cast_plan(cast_srcs, grid)

    def sec(s):
        return pl.BlockSpec((1, rows, d_a), lambda b, c, s=s: (b, c, s))

    return pl.pallas_call(
        functools.partial(_hgrn_kernel, n_cast=len(cast_srcs)),
        out_shape=(jax.ShapeDtypeStruct((bsz, t, d_a), BF16), *cast_out_shapes),
        grid=grid,
        in_specs=[sec(0), sec(1), sec(2), sec(3),
                  pl.BlockSpec((1, rows, LOGF_PIECES * d_a), lambda b, c: (b, c, 0)),
                  pl.BlockSpec((1, d_a), lambda b, c: (0, 0)),
                  *cast_in_specs],
        out_specs=(pl.BlockSpec((1, rows, d_a), lambda b, c: (b, c, 0)), *cast_out_specs),
        scratch_shapes=[pltpu.VMEM((n_g, LANES, LANES), F32),
                        pltpu.VMEM((HG_CHUNKS, n_g, HG_L + HG_SUB, LANES), F32),
                        pltpu.VMEM((HG_SUB * LANES, LANES), BF16),
                        pltpu.VMEM((HG_L, HG_L), BF16)],
        compiler_params=pltpu.CompilerParams(
            dimension_semantics=("arbitrary", "arbitrary")),
        name="hgrn",
    )(proj3, proj3, proj3, proj3, logf3, gain, *cast_arrays)


def _attn_kernel(q_ref, k_ref, v_ref, gvec_ref, *rest, n_cast):
    cast_in = rest[:n_cast]
    out_ref = rest[n_cast]
    cast_out = rest[n_cast + 1:2 * n_cast + 1]
    band_ref, bias_ref, vx_ref = rest[-3:]
    _cast_blocks(cast_in, cast_out)
    t = q_ref.shape[1]
    nq = AT_Q // CHUNK
    nk = nq + N_PAST_CHUNKS
    n_tb = REL_PAST // CHUNK + 2
    wb = (nk + nq - 1) * CHUNK

    @pl.when(pl.program_id(1) == 0)
    def _build_bias():
        grow = gvec_ref[0] * LOG2E
        xb = jnp.broadcast_to(grow, (CHUNK, 2 * LANES))
        r = lax.broadcasted_iota(jnp.int32, (CHUNK, 2 * LANES), 0)
        for bit in range(6):
            xb = jnp.where(((r >> bit) & 1) == 1, pltpu.roll(xb, 1 << bit, axis=1), xb)
        const = grow[:, 0:1]

        n_const = nk - nq + 1 - n_tb
        lo = (nq - 1) * CHUNK
        band_ref[:, 0:lo] = jnp.full((CHUNK, lo), MASK_VALUE, F32)
        band_ref[:, lo:lo + n_const * CHUNK] = jnp.broadcast_to(const, (CHUNK, n_const * CHUNK))
        band_ref[:, lo + n_const * CHUNK:lo + (n_const + n_tb) * CHUNK] = xb[:, CHUNK:]
        band_ref[:, wb - lo:wb] = jnp.full((CHUNK, lo), MASK_VALUE, F32)
        for qi in range(nq):
            off = (nq - 1 - qi) * CHUNK
            bias_ref[qi * CHUNK:(qi + 1) * CHUNK, :] = band_ref[:, off:off + nk * CHUNK]

    vx_ref[:, 0:LANES] = v_ref[0]
    vx_ref[:, LANES:2 * LANES] = jnp.ones((t, LANES), BF16)

    def window(g):
        q0 = g * AT_Q
        ks = max(0, q0 - N_PAST_CHUNKS * CHUNK)
        return q0, ks, q0 + AT_Q - ks

    def scores(g):
        q0, ks, kw = window(g)
        return lax.dot_general(q_ref[0, q0:q0 + AT_Q, :], k_ref[0, ks:ks + kw, :], NT_DIMS,
                               preferred_element_type=F32)

    n_groups = t // AT_Q
    s_next = scores(0)
    for g in range(n_groups):
        q0, ks, kw = window(g)
        s = s_next + bias_ref[:, nk * CHUNK - kw:nk * CHUNK]
        if g + 1 < n_groups:
            s_next = scores(g + 1)
        p = jnp.exp2(s - jnp.max(s, axis=-1, keepdims=True))
        ox = jnp.dot(p.astype(BF16), vx_ref[ks:ks + kw, :], preferred_element_type=F32)
        out_ref[0, q0:q0 + AT_Q, :] = (ox[:, 0:LANES] / ox[:, LANES:2 * LANES]).astype(BF16)


def _attn(proj3, gvec, cast_srcs, *, col0, n_heads):
    bsz, t, _ = proj3.shape
    d_b = n_heads * LANES
    nq = AT_Q // CHUNK
    nk = nq + N_PAST_CHUNKS
    blk0 = col0 // LANES
    grid = (n_heads, bsz)
    cast_arrays, cast_in_specs, cast_out_specs, cast_out_shapes = _cast_plan(cast_srcs, grid)

    def sec(s):
        return pl.BlockSpec((1, t, LANES), lambda h, b, s=s: (b, 0, blk0 + s * n_heads + h))

    return pl.pallas_call(
        functools.partial(_attn_kernel, n_cast=len(cast_srcs)),
        out_shape=(jax.ShapeDtypeStruct((bsz, t, d_b), BF16), *cast_out_shapes),
        grid=grid,
        in_specs=[sec(0), sec(1), sec(2),
                  pl.BlockSpec((1, 1, 2 * LANES), lambda h, b: (h, 0, 0)),
                  *cast_in_specs],
        out_specs=(pl.BlockSpec((1, t, LANES), lambda h, b: (b, 0, h)), *cast_out_specs),
        scratch_shapes=[pltpu.VMEM((CHUNK, (nk + nq - 1) * CHUNK), F32),
                        pltpu.VMEM((AT_Q, nk * CHUNK), F32),
                        pltpu.VMEM((t, 2 * LANES), BF16)],
        compiler_params=pltpu.CompilerParams(
            dimension_semantics=("arbitrary", "arbitrary")),
        name="attn",
    )(proj3, proj3, proj3, gvec, *cast_arrays)


def _merge_kernel(ya_ref, yb_ref, ga0_ref, ga1_ref, gb0_ref, gb1_ref, x_ref,
                  wa_ref, wb_ref, wo_ref, gain_ref, x1_ref, h2_ref):
    pa = jnp.dot(ya_ref[...], wa_ref[...], preferred_element_type=F32)
    pb = jnp.dot(yb_ref[...], wb_ref[...], preferred_element_type=F32)
    ga = jnp.concatenate([ga0_ref[...], ga1_ref[...]], axis=1).astype(F32)
    gb = jnp.concatenate([gb0_ref[...], gb1_ref[...]], axis=1).astype(F32)
    merged = (ga * pa + gb * pb).astype(BF16)
    x1 = x_ref[...] + jnp.dot(merged, wo_ref[...], preferred_element_type=F32)
    x1_ref[...] = x1
    ms = jnp.mean(x1 * x1, axis=-1, keepdims=True)
    h2_ref[...] = (x1 * lax.rsqrt(ms + EPS) * gain_ref[...]).astype(BF16)


def _merge(ya, yb, proj, x2, wa, wb, wo, gain, *, gate_col0):
    m, d = x2.shape
    da, db = ya.shape[1], yb.shape[1]
    tm = MG_TM
    gblk = gate_col0 // IN_TN

    def gate(s):
        return pl.BlockSpec((tm, IN_TN), lambda i, s=s: (i, gblk + s))

    def whole(a):
        return pl.BlockSpec(a.shape, lambda i: (0, 0), pipeline_mode=pl.Buffered(1))

    return pl.pallas_call(
        _merge_kernel,
        out_shape=(jax.ShapeDtypeStruct((m, d), F32),
                   jax.ShapeDtypeStruct((m, d), BF16)),
        grid=(m // tm,),
        in_specs=[pl.BlockSpec((tm, da), lambda i: (i, 0)),
                  pl.BlockSpec((tm, db), lambda i: (i, 0)),
                  gate(0), gate(1), gate(2), gate(3),
                  pl.BlockSpec((tm, d), lambda i: (i, 0)),
                  whole(wa), whole(wb), whole(wo), whole(gain)],
        out_specs=(pl.BlockSpec((tm, d), lambda i: (i, 0)),
                   pl.BlockSpec((tm, d), lambda i: (i, 0))),
        compiler_params=pltpu.CompilerParams(
            dimension_semantics=("arbitrary",),
            vmem_limit_bytes=VMEM_LIMIT_BYTES),
        name="merge",
    )(ya, yb, proj, proj, proj, proj, x2, wa, wb, wo, gain)


def _ffn_kernel(h_ref, wg_ref, wu_ref, wd_ref, x1_hbm, out_ref, x1_ref, sem):
    i, f = pl.program_id(0), pl.program_id(1)
    tm = out_ref.shape[0]

    def residual_copy():
        return pltpu.make_async_copy(x1_hbm.at[pl.ds(i * tm, tm), :], x1_ref, sem)

    def hidden_tile():
        h = h_ref[...]
        gate = jnp.dot(h, wg_ref[...], preferred_element_type=F32)
        up = jnp.dot(h, wu_ref[...], preferred_element_type=F32)
        act = (gate * _sigmoid(gate) * up).astype(BF16)
        return jnp.dot(act, wd_ref[...].astype(BF16), preferred_element_type=F32)

    last = pl.num_programs(1) - 1

    @pl.when(f == 0)
    def _first():
        residual_copy().start()
        out_ref[...] = hidden_tile()

    @pl.when((f > 0) & (f < last))
    def _middle():
        out_ref[...] += hidden_tile()

    @pl.when(f == last)
    def _last():
        residual_copy().wait()
        out_ref[...] += hidden_tile() + x1_ref[...]


def _ffn(h2, x1, w_gate, w_up, w_down):
    m, d = h2.shape
    dff = w_down.shape[0]
    tm, tf = FF_TM, FF_TF
    nf = dff // tf
    return pl.pallas_call(
        _ffn_kernel,
        out_shape=jax.ShapeDtypeStruct((m, d), F32),
        grid=(m // tm, nf),
        in_specs=[pl.BlockSpec((tm, d), lambda i, f: (i, 0)),
                  pl.BlockSpec((d, tf), lambda i, f: (0, f)),
                  pl.BlockSpec((d, tf), lambda i, f: (0, f)),
                  pl.BlockSpec((tf, d), lambda i, f: (f, 0)),
                  pl.BlockSpec(memory_space=pl.ANY)],
        out_specs=pl.BlockSpec((tm, d), lambda i, f: (i, 0)),
        scratch_shapes=[pltpu.VMEM((tm, d), F32), pltpu.SemaphoreType.DMA(())],
        compiler_params=pltpu.CompilerParams(
            dimension_semantics=("arbitrary", "arbitrary"),
            vmem_limit_bytes=VMEM_LIMIT_BYTES),
        name="ffn",
    )(h2, w_gate, w_up, w_down, x1)


def _bias_vector(rel_bias):
    assert rel_bias.shape[1] == N_REL and N_REL + CHUNK <= 2 * LANES
    rev = rel_bias[:, ::-1]
    pad = 2 * LANES - N_REL
    return jnp.pad(rev, ((0, 0), (pad, 0)), mode="edge")[:, None, :]


def kernel(x, w_in, b_gate, norm_mix, norm_ffn, hgrn_lb_logits, hgrn_out_gain,
           q_gain, k_gain, rel_bias, w_proj_a, w_proj_b, w_out, w_ffn_in, w_ffn_out):
    bsz, t, d = x.shape
    depth = w_in.shape[0]
    d_a = hgrn_out_gain.shape[1]
    dh = q_gain.shape[1]
    n_heads_b = rel_bias.shape[1]
    d_b = n_heads_b * dh
    n_in = w_in.shape[2]
    gate_col0 = 4 * d_a + 3 * d_b
    assert dh == LANES and d_a == IN_TN and d_b == IN_TN and depth == 1
    assert hgrn_lb_logits.shape[0] == depth + 1
    assert n_in == gate_col0 + 2 * d and t % AT_Q == 0 and t % (HG_CHUNKS * HG_L) == 0
    assert (bsz * t) % IN_TM == 0 and (bsz * t) % FF_TM == 0 and (bsz * t) % MG_TM == 0

    m = bsz * t
    x2 = x.reshape(m, d)
    for l in range(depth):
        proj, logf = _in_proj(
            x2, norm_mix[l][None, :], w_in[l], b_gate[l][None, :], hgrn_lb_logits,
            q_gain[l][None, :], k_gain[l][None, :], scale=dh ** -0.5 * LOG2E)
        proj3 = proj.reshape(bsz, t, n_in)
        y_a, w_gate_bf = _hgrn(
            proj3, logf.reshape(bsz, t, LOGF_PIECES * d_a), hgrn_out_gain[l][None, :],
            ((w_ffn_in[l], 0, 2),), d_a=d_a)
        y_b, wa_bf, wb_bf, wo_bf, w_up_bf = _attn(
            proj3, _bias_vector(rel_bias[l]),
            (w_proj_a[l], w_proj_b[l], w_out[l], (w_ffn_in[l], 1, 2)),
            col0=4 * d_a, n_heads=n_heads_b)
        x1, h2 = _merge(y_a.reshape(m, d_a), y_b.reshape(m, d_b), proj, x2,
                        wa_bf, wb_bf, wo_bf, norm_ffn[l][None, :], gate_col0=gate_col0)
        x2 = _ffn(h2, x1, w_gate_bf, w_up_bf, w_ffn_out[l])
    return x2.reshape(bsz, t, d)
```

```python
import collections
import functools

import jax
import jax.numpy as jnp
from jax import lax
from jax.experimental import pallas as pl
from jax.experimental.pallas import tpu as pltpu

F32 = jnp.float32
BF16 = jnp.bfloat16

EPS = 1e-6
LANES = 128
BF16_SUBLANES = 16
VMEM_LIMIT_BYTES = 56 * 1024 * 1024
CHUNK = 64
N_PAST_CHUNKS = 8
REL_FUTURE = CHUNK - 1
REL_PAST = 2 * CHUNK - 1
N_REL = REL_FUTURE + REL_PAST + 1
MASK_VALUE = -1e30
BIG_EXPONENT = 1e30
LOGF_PIECES = 3
LOG2E = 1.4426950408889634

IN_TM = 1024
IN_TN = 1024
IN_SUB = 256
IN_W_RING = 3
IN_NORM_ROWS = 256
IN_FIRST_GATE_SECTION = 7
HG_L = 128
HG_CHUNKS = 2
HG_FAST_RANGE = 100.0
HG_SUB = 32
AT_Q = 4 * CHUNK
MG_TM = 512
FF_TM = 1024
FF_TF = 512

NT_DIMS = (((1,), (1,)), ((), ()))


def _sigmoid(x):
    return 0.5 * jnp.tanh(0.5 * x) + 0.5


def _cast_plan(srcs, grid):
    steps, strides = 1, []
    for g in reversed(grid):
        strides.insert(0, steps)
        steps *= g

    def row_block(*ids):
        return sum(i * s for i, s in zip(ids, strides))

    arrays, in_specs, out_specs, out_shapes = [], [], [], []
    for src in srcs:
        a, part, n_parts = src if isinstance(src, tuple) else (src, 0, 1)
        rows, cols = a.shape[0], a.shape[1] // n_parts
        assert rows % (steps * BF16_SUBLANES) == 0 and cols % LANES == 0, (a.shape, grid)
        arrays.append(a)
        in_specs.append(pl.BlockSpec((rows // steps, cols),
                                     lambda *ids, part=part: (row_block(*ids), part)))
        out_specs.append(pl.BlockSpec((rows // steps, cols),
                                      lambda *ids: (row_block(*ids), 0)))
        out_shapes.append(jax.ShapeDtypeStruct((rows, cols), BF16))
    return arrays, in_specs, out_specs, out_shapes


def _cast_blocks(src_refs, dst_refs):
    for src, dst in zip(src_refs, dst_refs):
        dst[...] = src[...].astype(BF16)


def _in_proj_kernel(x_hbm, gain_ref, w_hbm, bias_ref, lbl_ref, qg_ref, kg_ref, *rest,
                    scale):
    out_ref, logf_ref, h_ref, x_ref, wbuf_ref, x_sem, w_sems = rest
    i, j = pl.program_id(0), pl.program_id(1)
    tm = x_ref.shape[0]
    tn = wbuf_ref.shape[2]

    n_j = pl.num_programs(1)
    step = i * n_j + j

    def w_copy(s):
        slot = lax.rem(s, IN_W_RING)
        col = pl.multiple_of(lax.rem(s, n_j) * tn, tn)
        return pltpu.make_async_copy(w_hbm.at[:, pl.ds(col, tn)], wbuf_ref.at[slot],
                                     w_sems.at[slot])

    @pl.when(step == 0)
    def _prime():
        for s in range(IN_W_RING - 1):
            w_copy(s).start()

    w_copy(step).wait()

    @pl.when(step + IN_W_RING - 1 < pl.num_programs(0) * n_j)
    def _request():
        w_copy(step + IN_W_RING - 1).start()

    w_ref = wbuf_ref.at[lax.rem(step, IN_W_RING)]

    def x_copy(tile):
        return pltpu.make_async_copy(x_hbm.at[pl.ds(tile * tm, tm), :], x_ref, x_sem)

    @pl.when((j == 0) & (i == 0))
    def _first_fetch():
        x_copy(0).start()

    def norm():
        x_copy(i).wait()
        for r in range(0, tm, IN_NORM_ROWS):
            x = x_ref[r:r + IN_NORM_ROWS, :]
            ms = jnp.mean(x * x, axis=-1, keepdims=True)
            h_ref[r:r + IN_NORM_ROWS, :] = (x * lax.rsqrt(ms + EPS) * gain_ref[...]).astype(BF16)

    @pl.when((j == 1) & (i + 1 < pl.num_programs(0)))
    def _next_fetch():
        x_copy(i + 1).start()

    def section(epilogue):
        for c in range(w_ref.shape[1] // IN_SUB):
            cs = slice(c * IN_SUB, (c + 1) * IN_SUB)
            epilogue(jnp.dot(h_ref[...], w_ref[:, cs].astype(BF16),
                             preferred_element_type=F32), cs)

    def silu_epilogue(acc, cs):
        out_ref[:, cs] = (acc * _sigmoid(acc)).astype(BF16)

    @pl.when(j == 0)
    def _norm_and_silu():
        norm()
        section(silu_epilogue)

    @pl.when(j == 3)
    def _silu():
        section(silu_epilogue)

    @pl.when(j == 1)
    def _forget():
        l = lbl_ref[...]
        e = jnp.exp(l - jnp.max(l, axis=0, keepdims=True))
        lb_row = e[0:1] / jnp.sum(e, axis=0, keepdims=True)

        def epilogue(acc, cs):
            lb = lb_row[:, cs]
            sig = _sigmoid(acc)
            lf = jnp.log2(lb + (1.0 - lb) * sig)
            hi = lf.astype(BF16)
            r1 = lf - hi.astype(F32)
            mid = r1.astype(BF16)
            tn = w_ref.shape[1]
            logf_ref[:, cs] = hi
            logf_ref[:, slice(tn + cs.start, tn + cs.stop)] = mid
            logf_ref[:, slice(2 * tn + cs.start, 2 * tn + cs.stop)] = (
                r1 - mid.astype(F32)).astype(BF16)
            out_ref[:, cs] = ((1.0 - lb) * (1.0 - sig)).astype(BF16)
        section(epilogue)

    @pl.when((j == 2) | (j == 6))
    def _plain():
        def epilogue(acc, cs):
            out_ref[:, cs] = acc.astype(BF16)
        section(epilogue)

    def head_norm(gain_ref_, mult):
        def epilogue(acc, cs):
            for h in range(IN_SUB // LANES):
                t = acc[:, h * LANES:(h + 1) * LANES]
                sl = slice(cs.start + h * LANES, cs.start + (h + 1) * LANES)
                ms = jnp.mean(t * t, axis=-1, keepdims=True)
                out_ref[:, sl] = (t * lax.rsqrt(ms + EPS) * (gain_ref_[...] * mult)).astype(BF16)
        section(epilogue)

    @pl.when(j == 4)
    def _qnorm():
        head_norm(qg_ref, scale)

    @pl.when(j == 5)
    def _knorm():
        head_norm(kg_ref, 1.0)

    @pl.when(j >= IN_FIRST_GATE_SECTION)
    def _gates():
        def epilogue(acc, cs):
            out_ref[:, cs] = _sigmoid(acc + bias_ref[:, cs]).astype(BF16)
        section(epilogue)


def _in_proj(x2, gain, w, b_gate, lb_logits, qg, kg, *, scale):
    m, d = x2.shape
    n = w.shape[1]
    tm, tn = IN_TM, IN_TN
    grid = (m // tm, n // tn)
    first_gate = (n - b_gate.shape[1]) // tn
    assert first_gate == IN_FIRST_GATE_SECTION
    return pl.pallas_call(
        functools.partial(_in_proj_kernel, scale=scale),
        out_shape=(jax.ShapeDtypeStruct((m, n), BF16),
                   jax.ShapeDtypeStruct((m, LOGF_PIECES * tn), BF16)),
        grid=grid,
        in_specs=[
            pl.BlockSpec(memory_space=pl.ANY),
            pl.BlockSpec((1, d), lambda i, j: (0, 0)),
            pl.BlockSpec(memory_space=pl.ANY),
            pl.BlockSpec((1, tn), lambda i, j: (0, jnp.maximum(j - first_gate, 0))),
            pl.BlockSpec(lb_logits.shape, lambda i, j: (0, 0)),
            pl.BlockSpec((1, LANES), lambda i, j: (0, 0)),
            pl.BlockSpec((1, LANES), lambda i, j: (0, 0)),
        ],
        out_specs=(pl.BlockSpec((tm, tn), lambda i, j: (i, j)),
                   pl.BlockSpec((tm, LOGF_PIECES * tn), lambda i, j: (i, 0))),
        scratch_shapes=[pltpu.VMEM((tm, d), BF16), pltpu.VMEM((tm, d), F32),
                        pltpu.VMEM((IN_W_RING, d, tn), F32),
                        pltpu.SemaphoreType.DMA(()), pltpu.SemaphoreType.DMA((IN_W_RING,))],
        compiler_params=pltpu.CompilerParams(
            dimension_semantics=("arbitrary", "arbitrary"),
            vmem_limit_bytes=VMEM_LIMIT_BYTES),
        name="in_proj",
    )(x2, gain, w, b_gate, lb_logits, qg, kg)


def _hgrn_kernel(q_ref, k_ref, v_ref, g_ref, lf_ref, gain_ref, *rest, n_cast):
    cast_in = rest[:n_cast]
    out_ref = rest[n_cast]
    cast_out = rest[n_cast + 1:2 * n_cast + 1]
    st_ref, cb_ref, sel_ref, tri_ref = rest[-4:]
    _cast_blocks(cast_in, cast_out)
    n_g = st_ref.shape[0]
    L, SUB = HG_L, HG_SUB

    @pl.when(pl.program_id(1) == 0)
    def _init():
        st_ref[...] = jnp.zeros_like(st_ref)
        cb_ref[...] = jnp.full(cb_ref.shape, BIG_EXPONENT, F32)
        r = lax.broadcasted_iota(jnp.int32, sel_ref.shape, 0)
        c = lax.broadcasted_iota(jnp.int32, sel_ref.shape, 1)
        sel_ref[...] = jnp.where(c == SUB - 1 - r // LANES, 1.0, 0.0).astype(BF16)
        row = lax.broadcasted_iota(jnp.int32, (L, L), 0)
        col = lax.broadcasted_iota(jnp.int32, (L, L), 1)
        tri_ref[...] = jnp.where(row >= col, 1.0, 0.0).astype(BF16)

    heads = [slice(g * LANES, (g + 1) * LANES) for g in range(n_g)]
    gw = n_g * LANES
    state = [st_ref[g] for g in range(n_g)]
    chunks = []
    for ch in range(HG_CHUNKS):
        rows = slice(ch * L, (ch + 1) * L)
        state, ctx = _hgrn_front(rows, state, heads, gw, q_ref, k_ref, v_ref, lf_ref,
                                 tri_ref)
        chunks.append((rows, ch, ctx))
    for g in range(n_g):
        st_ref[g] = state[g]

    d_min = functools.reduce(jnp.minimum, [jnp.min(ctx.d) for _, _, ctx in chunks])
    fast = d_min >= -HG_FAST_RANGE

    @pl.when(fast)
    def _fast():
        _hgrn_back_fast(chunks, heads, g_ref, gain_ref, out_ref)

    @pl.when(jnp.logical_not(fast))
    def _exact():
        for rows, ch, ctx in chunks:
            _hgrn_back_exact(rows, ch, ctx, heads, g_ref, gain_ref, out_ref, cb_ref, sel_ref)


_HgrnChunk = collections.namedtuple("_HgrnChunk", "q k b d v_bf o_blocks s_far qn")


def _hgrn_front(rows, state, heads, gw, q_ref, k_ref, v_ref, lf_ref, tri_ref):
    L, SUB = HG_L, HG_SUB

    b = sum(jnp.dot(tri_ref[...], lf_ref[0, rows, p * gw:(p + 1) * gw],
                    preferred_element_type=F32) for p in range(LOGF_PIECES))
    d = b - jnp.concatenate([jnp.broadcast_to(b[r0:r0 + 1], (SUB, gw))
                             for r0 in range(0, L, SUB)], axis=0)
    q = q_ref[0, rows, :].astype(F32)
    k = k_ref[0, rows, :].astype(F32)
    v_bf = v_ref[0, rows, :]
    v = v_bf.astype(F32)

    qe = (q * jnp.exp2(b)).astype(BF16)
    o_blocks = [[None] * (L // SUB) for _ in heads]
    for g, sl in enumerate(heads):
        o = lax.dot_general(qe[:, sl], state[g].astype(BF16), NT_DIMS,
                            preferred_element_type=F32)
        for i in range(L // SUB):
            o_blocks[g][i] = o[i * SUB:(i + 1) * SUB]

    qn = (q * jnp.exp2(d)).astype(BF16)
    s_far = {}
    for i in range(1, L // SUB):
        r0 = i * SUB
        kt = (k[0:r0] * jnp.exp2(b[r0:r0 + 1] - b[0:r0])).astype(BF16)
        for g, sl in enumerate(heads):
            s_far[i, g] = lax.dot_general(qn[r0:r0 + SUB, sl], kt[:, sl], NT_DIMS,
                                          preferred_element_type=F32)

    b_last = b[L - 1:L]
    kdec = (k * jnp.exp2(b_last - b)).astype(BF16)
    eb_last = jnp.exp2(b_last)
    new_state = []
    for g, sl in enumerate(heads):
        upd = jnp.dot(jnp.transpose(v[:, sl]).astype(BF16), kdec[:, sl],
                      preferred_element_type=F32)
        new_state.append(state[g] * eb_last[:, sl] + upd)
    return new_state, _HgrnChunk(q, k, b, d, v_bf, o_blocks, s_far, qn)


def _hgrn_far(o_blocks, s_far, v_bf, heads, min_distance):
    L, SUB = HG_L, HG_SUB
    out = [list(blocks) for blocks in o_blocks]
    for i in range(1, L // SUB):
        r0 = i * SUB
        keep = (lax.broadcasted_iota(jnp.int32, (SUB, r0), 1) + min_distance
                <= lax.broadcasted_iota(jnp.int32, (SUB, r0), 0) + r0)
        for g, sl in enumerate(heads):
            s = s_far[i, g] if min_distance <= 1 else jnp.where(keep, s_far[i, g], 0.0)
            oi = jnp.dot(s.astype(BF16), v_bf[0:r0, sl], preferred_element_type=F32)
            out[g][i] = out[g][i] + oi
    return out


def _hgrn_finish(o, rows, sl, g_ref, gain_ref, out_ref):
    ms = jnp.mean(o * o, axis=-1, keepdims=True)
    y = o * lax.rsqrt(ms + EPS) * gain_ref[:, sl] * g_ref[0, rows, sl].astype(F32)
    out_ref[0, rows, sl] = y.astype(BF16)


def _hgrn_back_fast(chunks, heads, g_ref, gain_ref, out_ref):
    L, SUB = HG_L, HG_SUB
    row = lax.broadcasted_iota(jnp.int32, (L, L), 0)
    col = lax.broadcasted_iota(jnp.int32, (L, L), 1)
    same = (col <= row) & (col >= (row // SUB) * SUB)
    s_near = []
    for _, _, ctx in chunks:
        kn = (ctx.k * jnp.exp2(-ctx.d)).astype(BF16)
        s_near.append([lax.dot_general(ctx.qn[:, sl], kn[:, sl], NT_DIMS,
                                       preferred_element_type=F32) for sl in heads])
    far = [_hgrn_far(ctx.o_blocks, ctx.s_far, ctx.v_bf, heads, min_distance=1)
           for _, _, ctx in chunks]
    near = [[jnp.dot(jnp.where(same, s_near[n][g], 0.0).astype(BF16), ctx.v_bf[:, sl],
                     preferred_element_type=F32) for g, sl in enumerate(heads)]
            for n, (_, _, ctx) in enumerate(chunks)]
    for n, (rows, _, _) in enumerate(chunks):
        for g, sl in enumerate(heads):
            _hgrn_finish(jnp.concatenate(far[n][g], axis=0) + near[n][g], rows, sl,
                         g_ref, gain_ref, out_ref)


def _hgrn_back_exact(rows, ch, ctx, heads, g_ref, gain_ref, out_ref, cb_ref, sel_ref):
    L, SUB = HG_L, HG_SUB
    q, k, b, v_bf = ctx.q, ctx.k, ctx.b, ctx.v_bf
    o_blocks = _hgrn_far(ctx.o_blocks, ctx.s_far, v_bf, heads, min_distance=SUB)

    c = b - jnp.log2(k)
    for g, sl in enumerate(heads):
        cb_ref[ch, g, SUB:SUB + L, :] = c[:, sl]

    near = []
    for g, sl in enumerate(heads):
        qg, bg = q[:, sl], b[:, sl]
        e = [(qg * jnp.exp2(bg - cb_ref[ch, g, SUB - off:SUB - off + L, :])).astype(BF16)
             for off in range(SUB)]
        near.append(jnp.dot(jnp.concatenate(e, axis=1), sel_ref[...],
                            preferred_element_type=F32))

    for g, sl in enumerate(heads):
        s = pltpu.roll(near[g], LANES - (SUB - 1), axis=1, stride=1, stride_axis=0)
        o = jnp.concatenate(o_blocks[g], axis=0) + jnp.dot(
            s[:, 0:L].astype(BF16), v_bf[:, sl], preferred_element_type=F32)
        _hgrn_finish(o, rows, sl, g_ref, gain_ref, out_ref)


def _hgrn(proj3, logf3, gain, cast_srcs, *, d_a):
    bsz, t, _ = proj3.shape
    n_g = d_a // LANES
    rows = HG_CHUNKS * HG_L
    grid = (bsz, t // rows)
    cast_arrays, cast_in_specs, cast_out_specs, cast_out_shapes = _cast_plan(cast_srcs, grid)

    def sec(s):
        return pl.BlockSpec((1, rows, d_a), lambda b, c, s=s: (b, c, s))

    return pl.pallas_call(
        functools.partial(_hgrn_kernel, n_cast=len(cast_srcs)),
        out_shape=(jax.ShapeDtypeStruct((bsz, t, d_a), BF16), *cast_out_shapes),
        grid=grid,
        in_specs=[sec(0), sec(1), sec(2), sec(3),
                  pl.BlockSpec((1, rows, LOGF_PIECES * d_a), lambda b, c: (b, c, 0)),
                  pl.BlockSpec((1, d_a), lambda b, c: (0, 0)),
                  *cast_in_specs],
        out_specs=(pl.BlockSpec((1, rows, d_a), lambda b, c: (b, c, 0)), *cast_out_specs),
        scratch_shapes=[pltpu.VMEM((n_g, LANES, LANES), F32),
                        pltpu.VMEM((HG_CHUNKS, n_g, HG_L + HG_SUB, LANES), F32),
                        pltpu.VMEM((HG_SUB * LANES, LANES), BF16),
                        pltpu.VMEM((HG_L, HG_L), BF16)],
        compiler_params=pltpu.CompilerParams(
            dimension_semantics=("arbitrary", "arbitrary")),
        name="hgrn",
    )(proj3, proj3, proj3, proj3, logf3, gain, *cast_arrays)


def _attn_kernel(q_ref, k_ref, v_ref, gvec_ref, *rest, n_cast):
    cast_in = rest[:n_cast]
    out_ref = rest[n_cast]
    cast_out = rest[n_cast + 1:2 * n_cast + 1]
    band_ref, bias_ref, vx_ref = rest[-3:]
    _cast_blocks(cast_in, cast_out)
    t = q_ref.shape[1]
    nq = AT_Q // CHUNK
    nk = nq + N_PAST_CHUNKS
    n_tb = REL_PAST // CHUNK + 2
    wb = (nk + nq - 1) * CHUNK

    @pl.when(pl.program_id(1) == 0)
    def _build_bias():
        grow = gvec_ref[0] * LOG2E
        xb = jnp.broadcast_to(grow, (CHUNK, 2 * LANES))
        r = lax.broadcasted_iota(jnp.int32, (CHUNK, 2 * LANES), 0)
        for bit in range(6):
            xb = jnp.where(((r >> bit) & 1) == 1, pltpu.roll(xb, 1 << bit, axis=1), xb)
        const = grow[:, 0:1]

        n_const = nk - nq + 1 - n_tb
        lo = (nq - 1) * CHUNK
        band_ref[:, 0:lo] = jnp.full((CHUNK, lo), MASK_VALUE, F32)
        band_ref[:, lo:lo + n_const * CHUNK] = jnp.broadcast_to(const, (CHUNK, n_const * CHUNK))
        band_ref[:, lo + n_const * CHUNK:lo + (n_const + n_tb) * CHUNK] = xb[:, CHUNK:]
        band_ref[:, wb - lo:wb] = jnp.full((CHUNK, lo), MASK_VALUE, F32)
        for qi in range(nq):
            off = (nq - 1 - qi) * CHUNK
            bias_ref[qi * CHUNK:(qi + 1) * CHUNK, :] = band_ref[:, off:off + nk * CHUNK]

    vx_ref[:, 0:LANES] = v_ref[0]
    vx_ref[:, LANES:2 * LANES] = jnp.ones((t, LANES), BF16)

    def window(g):
        q0 = g * AT_Q
        ks = max(0, q0 - N_PAST_CHUNKS * CHUNK)
        return q0, ks, q0 + AT_Q - ks

    def scores(g):
        q0, ks, kw = window(g)
        return lax.dot_general(q_ref[0, q0:q0 + AT_Q, :], k_ref[0, ks:ks + kw, :], NT_DIMS,
                               preferred_element_type=F32)

    n_groups = t // AT_Q
    s_next = scores(0)
    for g in range(n_groups):
        q0, ks, kw = window(g)
        s = s_next + bias_ref[:, nk * CHUNK - kw:nk * CHUNK]
        if g + 1 < n_groups:
            s_next = scores(g + 1)
        p = jnp.exp2(s - jnp.max(s, axis=-1, keepdims=True))
        ox = jnp.dot(p.astype(BF16), vx_ref[ks:ks + kw, :], preferred_element_type=F32)
        out_ref[0, q0:q0 + AT_Q, :] = (ox[:, 0:LANES] / ox[:, LANES:2 * LANES]).astype(BF16)


def _attn(proj3, gvec, cast_srcs, *, col0, n_heads):
    bsz, t, _ = proj3.shape
    d_b = n_heads * LANES
    nq = AT_Q // CHUNK
    nk = nq + N_PAST_CHUNKS
    blk0 = col0 // LANES
    grid = (n_heads, bsz)
    cast_arrays, cast_in_specs, cast_out_specs, cast_out_shapes = _cast_plan(cast_srcs, grid)

    def sec(s):
        return pl.BlockSpec((1, t, LANES), lambda h, b, s=s: (b, 0, blk0 + s * n_heads + h))

    return pl.pallas_call(
        functools.partial(_attn_kernel, n_cast=len(cast_srcs)),
        out_shape=(jax.ShapeDtypeStruct((bsz, t, d_b), BF16), *cast_out_shapes),
        grid=grid,
        in_specs=[sec(0), sec(1), sec(2),
                  pl.BlockSpec((1, 1, 2 * LANES), lambda h, b: (h, 0, 0)),
                  *cast_in_specs],
        out_specs=(pl.BlockSpec((1, t, LANES), lambda h, b: (b, 0, h)), *cast_out_specs),
        scratch_shapes=[pltpu.VMEM((CHUNK, (nk + nq - 1) * CHUNK), F32),
                        pltpu.VMEM((AT_Q, nk * CHUNK), F32),
                        pltpu.VMEM((t, 2 * LANES), BF16)],
        compiler_params=pltpu.CompilerParams(
            dimension_semantics=("arbitrary", "arbitrary")),
        name="attn",
    )(proj3, proj3, proj3, gvec, *cast_arrays)


def _merge_kernel(ya_ref, yb_ref, ga0_ref, ga1_ref, gb0_ref, gb1_ref, x_ref,
                  wa_ref, wb_ref, wo_ref, gain_ref, x1_ref, h2_ref):
    pa = jnp.dot(ya_ref[...], wa_ref[...], preferred_element_type=F32)
    pb = jnp.dot(yb_ref[...], wb_ref[...], preferred_element_type=F32)
    ga = jnp.concatenate([ga0_ref[...], ga1_ref[...]], axis=1).astype(F32)
    gb = jnp.concatenate([gb0_ref[...], gb1_ref[...]], axis=1).astype(F32)
    merged = (ga * pa + gb * pb).astype(BF16)
    x1 = x_ref[...] + jnp.dot(merged, wo_ref[...], preferred_element_type=F32)
    x1_ref[...] = x1
    ms = jnp.mean(x1 * x1, axis=-1, keepdims=True)
    h2_ref[...] = (x1 * lax.rsqrt(ms + EPS) * gain_ref[...]).astype(BF16)


def _merge(ya, yb, proj, x2, wa, wb, wo, gain, *, gate_col0):
    m, d = x2.shape
    da, db = ya.shape[1], yb.shape[1]
    tm = MG_TM
    gblk = gate_col0 // IN_TN

    def gate(s):
        return pl.BlockSpec((tm, IN_TN), lambda i, s=s: (i, gblk + s))

    def whole(a):
        return pl.BlockSpec(a.shape, lambda i: (0, 0), pipeline_mode=pl.Buffered(1))

    return pl.pallas_call(
        _merge_kernel,
        out_shape=(jax.ShapeDtypeStruct((m, d), F32),
                   jax.ShapeDtypeStruct((m, d), BF16)),
        grid=(m // tm,),
        in_specs=[pl.BlockSpec((tm, da), lambda i: (i, 0)),
                  pl.BlockSpec((tm, db), lambda i: (i, 0)),
                  gate(0), gate(1), gate(2), gate(3),
                  pl.BlockSpec((tm, d), lambda i: (i, 0)),
                  whole(wa), whole(wb), whole(wo), whole(gain)],
        out_specs=(pl.BlockSpec((tm, d), lambda i: (i, 0)),
                   pl.BlockSpec((tm, d), lambda i: (i, 0))),
        compiler_params=pltpu.CompilerParams(
            dimension_semantics=("arbitrary",),
            vmem_limit_bytes=VMEM_LIMIT_BYTES),
        name="merge",
    )(ya, yb, proj, proj, proj, proj, x2, wa, wb, wo, gain)


def _ffn_kernel(h_ref, wg_ref, wu_ref, wd_ref, x1_hbm, out_ref, x1_ref, sem):
    i, f = pl.program_id(0), pl.program_id(1)
    tm = out_ref.shape[0]

    def residual_copy():
        return pltpu.make_async_copy(x1_hbm.at[pl.ds(i * tm, tm), :], x1_ref, sem)

    def hidden_tile():
        h = h_ref[...]
        gate = jnp.dot(h, wg_ref[...], preferred_element_type=F32)
        up = jnp.dot(h, wu_ref[...], preferred_element_type=F32)
        act = (gate * _sigmoid(gate) * up).astype(BF16)
        return jnp.dot(act, wd_ref[...].astype(BF16), preferred_element_type=F32)

    last = pl.num_programs(1) - 1

    @pl.when(f == 0)
    def _first():
        residual_copy().start()
        out_ref[...] = hidden_tile()

    @pl.when((f > 0) & (f < last))
    def _middle():
        out_ref[...] += hidden_tile()

    @pl.when(f == last)
    def _last():
        residual_copy().wait()
        out_ref[...] += hidden_tile() + x1_ref[...]


def _ffn(h2, x1, w_gate, w_up, w_down):
    m, d = h2.shape
    dff = w_down.shape[0]
    tm, tf = FF_TM, FF_TF
    nf = dff // tf
    return pl.pallas_call(
        _ffn_kernel,
        out_shape=jax.ShapeDtypeStruct((m, d), F32),
        grid=(m // tm, nf),
        in_specs=[pl.BlockSpec((tm, d), lambda i, f: (i, 0)),
                  pl.BlockSpec((d, tf), lambda i, f: (0, f)),
                  pl.BlockSpec((d, tf), lambda i, f: (0, f)),
                  pl.BlockSpec((tf, d), lambda i, f: (f, 0)),
                  pl.BlockSpec(memory_space=pl.ANY)],
        out_specs=pl.BlockSpec((tm, d), lambda i, f: (i, 0)),
        scratch_shapes=[pltpu.VMEM((tm, d), F32), pltpu.SemaphoreType.DMA(())],
        compiler_params=pltpu.CompilerParams(
            dimension_semantics=("arbitrary", "arbitrary"),
            vmem_limit_bytes=VMEM_LIMIT_BYTES),
        name="ffn",
    )(h2, w_gate, w_up, w_down, x1)


def _bias_vector(rel_bias):
    assert rel_bias.shape[1] == N_REL and N_REL + CHUNK <= 2 * LANES
    rev = rel_bias[:, ::-1]
    pad = 2 * LANES - N_REL
    return jnp.pad(rev, ((0, 0), (pad, 0)), mode="edge")[:, None, :]


def kernel(x, w_in, b_gate, norm_mix, norm_ffn, hgrn_lb_logits, hgrn_out_gain,
           q_gain, k_gain, rel_bias, w_proj_a, w_proj_b, w_out, w_ffn_in, w_ffn_out):
    bsz, t, d = x.shape
    depth = w_in.shape[0]
    d_a = hgrn_out_gain.shape[1]
    dh = q_gain.shape[1]
    n_heads_b = rel_bias.shape[1]
    d_b = n_heads_b * dh
    n_in = w_in.shape[2]
    gate_col0 = 4 * d_a + 3 * d_b
    assert dh == LANES and d_a == IN_TN and d_b == IN_TN and depth == 1
    assert hgrn_lb_logits.shape[0] == depth + 1
    assert n_in == gate_col0 + 2 * d and t % AT_Q == 0 and t % (HG_CHUNKS * HG_L) == 0
    assert (bsz * t) % IN_TM == 0 and (bsz * t) % FF_TM == 0 and (bsz * t) % MG_TM == 0

    m = bsz * t
    x2 = x.reshape(m, d)
    for l in range(depth):
        proj, logf = _in_proj(
            x2, norm_mix[l][None, :], w_in[l], b_gate[l][None, :], hgrn_lb_logits,
            q_gain[l][None, :], k_gain[l][None, :], scale=dh ** -0.5 * LOG2E)
        proj3 = proj.reshape(bsz, t, n_in)
        y_a, w_gate_bf = _hgrn(
            proj3, logf.reshape(bsz, t, LOGF_PIECES * d_a), hgrn_out_gain[l][None, :],
            ((w_ffn_in[l], 0, 2),), d_a=d_a)
        y_b, wa_bf, wb_bf, wo_bf, w_up_bf = _attn(
            proj3, _bias_vector(rel_bias[l]),
            (w_proj_a[l], w_proj_b[l], w_out[l], (w_ffn_in[l], 1, 2)),
            col0=4 * d_a, n_heads=n_heads_b)
        x1, h2 = _merge(y_a.reshape(m, d_a), y_b.reshape(m, d_b), proj, x2,
                        wa_bf, wb_bf, wo_bf, norm_ffn[l][None, :], gate_col0=gate_col0)
        x2 = _ffn(h2, x1, w_gate_bf, w_up_bf, w_ffn_out[l])
    return x2.reshape(bsz, t, d)
```

```python
import collections
import functools

import jax
import jax.numpy as jnp
from jax import lax
from jax.experimental import pallas as pl
from jax.experimental.pallas import tpu as pltpu

F32 = jnp.float32
BF16 = jnp.bfloat16

EPS = 1e-6
LANES = 128
BF16_SUBLANES = 16
VMEM_LIMIT_BYTES = 56 * 1024 * 1024
CHUNK = 64
N_PAST_CHUNKS = 8
REL_FUTURE = CHUNK - 1
REL_PAST = 2 * CHUNK - 1
N_REL = REL_FUTURE + REL_PAST + 1
MASK_VALUE = -1e30
BIG_EXPONENT = 1e30
LOGF_PIECES = 3
LOG2E = 1.4426950408889634

IN_TM = 1024
IN_TN = 1024
IN_SUB = 256
IN_W_RING = 3
IN_NORM_ROWS = 256
IN_FIRST_GATE_SECTION = 7
HG_L = 128
HG_CHUNKS = 4
HG_FAST_RANGE = 100.0
HG_SUB = 32
AT_Q = 4 * CHUNK
MG_TM = 512
FF_TM = 1024
FF_TF = 512

NT_DIMS = (((1,), (1,)), ((), ()))


def _sigmoid(x):
    return 0.5 * jnp.tanh(0.5 * x) + 0.5


def _cast_plan(srcs, grid):
    steps, strides = 1, []
    for g in reversed(grid):
        strides.insert(0, steps)
        steps *= g

    def row_block(*ids):
        return sum(i * s for i, s in zip(ids, strides))

    arrays, in_specs, out_specs, out_shapes = [], [], [], []
    for src in srcs:
        a, part, n_parts = src if isinstance(src, tuple) else (src, 0, 1)
        rows, cols = a.shape[0], a.shape[1] // n_parts
        assert rows % (steps * BF16_SUBLANES) == 0 and cols % LANES == 0, (a.shape, grid)
        arrays.append(a)
        in_specs.append(pl.BlockSpec((rows // steps, cols),
                                     lambda *ids, part=part: (row_block(*ids), part)))
        out_specs.append(pl.BlockSpec((rows // steps, cols),
                                      lambda *ids: (row_block(*ids), 0)))
        out_shapes.append(jax.ShapeDtypeStruct((rows, cols), BF16))
    return arrays, in_specs, out_specs, out_shapes


def _cast_blocks(src_refs, dst_refs):
    for src, dst in zip(src_refs, dst_refs):
        dst[...] = src[...].astype(BF16)


def _in_proj_kernel(x_hbm, gain_ref, w_hbm, bias_ref, lbl_ref, qg_ref, kg_ref, *rest,
                    scale):
    out_ref, logf_ref, h_ref, x_ref, wbuf_ref, x_sem, w_sems = rest
    i, j = pl.program_id(0), pl.program_id(1)
    tm = x_ref.shape[0]
    tn = wbuf_ref.shape[2]

    n_j = pl.num_programs(1)
    step = i * n_j + j

    def w_copy(s):
        slot = lax.rem(s, IN_W_RING)
        col = pl.multiple_of(lax.rem(s, n_j) * tn, tn)
        return pltpu.make_async_copy(w_hbm.at[:, pl.ds(col, tn)], wbuf_ref.at[slot],
                                     w_sems.at[slot])

    @pl.when(step == 0)
    def _prime():
        for s in range(IN_W_RING - 1):
            w_copy(s).start()

    w_copy(step).wait()

    @pl.when(step + IN_W_RING - 1 < pl.num_programs(0) * n_j)
    def _request():
        w_copy(step + IN_W_RING - 1).start()

    w_ref = wbuf_ref.at[lax.rem(step, IN_W_RING)]

    def x_copy(tile):
        return pltpu.make_async_copy(x_hbm.at[pl.ds(tile * tm, tm), :], x_ref, x_sem)

    @pl.when((j == 0) & (i == 0))
    def _first_fetch():
        x_copy(0).start()

    def norm():
        x_copy(i).wait()
        for r in range(0, tm, IN_NORM_ROWS):
            x = x_ref[r:r + IN_NORM_ROWS, :]
            ms = jnp.mean(x * x, axis=-1, keepdims=True)
            h_ref[r:r + IN_NORM_ROWS, :] = (x * lax.rsqrt(ms + EPS) * gain_ref[...]).astype(BF16)

    @pl.when((j == 1) & (i + 1 < pl.num_programs(0)))
    def _next_fetch():
        x_copy(i + 1).start()

    def section(epilogue):
        for c in range(w_ref.shape[1] // IN_SUB):
            cs = slice(c * IN_SUB, (c + 1) * IN_SUB)
            epilogue(jnp.dot(h_ref[...], w_ref[:, cs].astype(BF16),
                             preferred_element_type=F32), cs)

    def silu_epilogue(acc, cs):
        out_ref[:, cs] = (acc * _sigmoid(acc)).astype(BF16)

    @pl.when(j == 0)
    def _norm_and_silu():
        norm()
        section(silu_epilogue)

    @pl.when(j == 3)
    def _silu():
        section(silu_epilogue)

    @pl.when(j == 1)
    def _forget():
        l = lbl_ref[...]
        e = jnp.exp(l - jnp.max(l, axis=0, keepdims=True))
        lb_row = e[0:1] / jnp.sum(e, axis=0, keepdims=True)

        def epilogue(acc, cs):
            lb = lb_row[:, cs]
            sig = _sigmoid(acc)
            lf = jnp.log2(lb + (1.0 - lb) * sig)
            hi = lf.astype(BF16)
            r1 = lf - hi.astype(F32)
            mid = r1.astype(BF16)
            tn = w_ref.shape[1]
            logf_ref[:, cs] = hi
            logf_ref[:, slice(tn + cs.start, tn + cs.stop)] = mid
            logf_ref[:, slice(2 * tn + cs.start, 2 * tn + cs.stop)] = (
                r1 - mid.astype(F32)).astype(BF16)
            out_ref[:, cs] = ((1.0 - lb) * (1.0 - sig)).astype(BF16)
        section(epilogue)

    @pl.when((j == 2) | (j == 6))
    def _plain():
        def epilogue(acc, cs):
            out_ref[:, cs] = acc.astype(BF16)
        section(epilogue)

    def head_norm(gain_ref_, mult):
        def epilogue(acc, cs):
            for h in range(IN_SUB // LANES):
                t = acc[:, h * LANES:(h + 1) * LANES]
                sl = slice(cs.start + h * LANES, cs.start + (h + 1) * LANES)
                ms = jnp.mean(t * t, axis=-1, keepdims=True)
                out_ref[:, sl] = (t * lax.rsqrt(ms + EPS) * (gain_ref_[...] * mult)).astype(BF16)
        section(epilogue)

    @pl.when(j == 4)
    def _qnorm():
        head_norm(qg_ref, scale)

    @pl.when(j == 5)
    def _knorm():
        head_norm(kg_ref, 1.0)

    @pl.when(j >= IN_FIRST_GATE_SECTION)
    def _gates():
        def epilogue(acc, cs):
            out_ref[:, cs] = _sigmoid(acc + bias_ref[:, cs]).astype(BF16)
        section(epilogue)


def _in_proj(x2, gain, w, b_gate, lb_logits, qg, kg, *, scale):
    m, d = x2.shape
    n = w.shape[1]
    tm, tn = IN_TM, IN_TN
    grid = (m // tm, n // tn)
    first_gate = (n - b_gate.shape[1]) // tn
    assert first_gate == IN_FIRST_GATE_SECTION
    return pl.pallas_call(
        functools.partial(_in_proj_kernel, scale=scale),
        out_shape=(jax.ShapeDtypeStruct((m, n), BF16),
                   jax.ShapeDtypeStruct((m, LOGF_PIECES * tn), BF16)),
        grid=grid,
        in_specs=[
            pl.BlockSpec(memory_space=pl.ANY),
            pl.BlockSpec((1, d), lambda i, j: (0, 0)),
            pl.BlockSpec(memory_space=pl.ANY),
            pl.BlockSpec((1, tn), lambda i, j: (0, jnp.maximum(j - first_gate, 0))),
            pl.BlockSpec(lb_logits.shape, lambda i, j: (0, 0)),
            pl.BlockSpec((1, LANES), lambda i, j: (0, 0)),
            pl.BlockSpec((1, LANES), lambda i, j: (0, 0)),
        ],
        out_specs=(pl.BlockSpec((tm, tn), lambda i, j: (i, j)),
                   pl.BlockSpec((tm, LOGF_PIECES * tn), lambda i, j: (i, 0))),
        scratch_shapes=[pltpu.VMEM((tm, d), BF16), pltpu.VMEM((tm, d), F32),
                        pltpu.VMEM((IN_W_RING, d, tn), F32),
                        pltpu.SemaphoreType.DMA(()), pltpu.SemaphoreType.DMA((IN_W_RING,))],
        compiler_params=pltpu.CompilerParams(
            dimension_semantics=("arbitrary", "arbitrary"),
            vmem_limit_bytes=VMEM_LIMIT_BYTES),
        name="in_proj",
    )(x2, gain, w, b_gate, lb_logits, qg, kg)


def _hgrn_kernel(q_ref, k_ref, v_ref, g_ref, lf_ref, gain_ref, *rest, n_cast):
    cast_in = rest[:n_cast]
    out_ref = rest[n_cast]
    cast_out = rest[n_cast + 1:2 * n_cast + 1]
    st_ref, cb_ref, sel_ref, tri_ref = rest[-4:]
    _cast_blocks(cast_in, cast_out)
    n_g = st_ref.shape[0]
    L, SUB = HG_L, HG_SUB

    @pl.when(pl.program_id(1) == 0)
    def _init():
        st_ref[...] = jnp.zeros_like(st_ref)
        cb_ref[...] = jnp.full(cb_ref.shape, BIG_EXPONENT, F32)
        r = lax.broadcasted_iota(jnp.int32, sel_ref.shape, 0)
        c = lax.broadcasted_iota(jnp.int32, sel_ref.shape, 1)
        sel_ref[...] = jnp.where(c == SUB - 1 - r // LANES, 1.0, 0.0).astype(BF16)
        row = lax.broadcasted_iota(jnp.int32, (L, L), 0)
        col = lax.broadcasted_iota(jnp.int32, (L, L), 1)
        tri_ref[...] = jnp.where(row >= col, 1.0, 0.0).astype(BF16)

    heads = [slice(g * LANES, (g + 1) * LANES) for g in range(n_g)]
    gw = n_g * LANES
    state = [st_ref[g] for g in range(n_g)]
    chunks = []
    for ch in range(HG_CHUNKS):
        rows = slice(ch * L, (ch + 1) * L)
        state, ctx = _hgrn_front(rows, state, heads, gw, q_ref, k_ref, v_ref, lf_ref,
                                 tri_ref)
        chunks.append((rows, ch, ctx))
    for g in range(n_g):
        st_ref[g] = state[g]

    d_min = functools.reduce(jnp.minimum, [jnp.min(ctx.d) for _, _, ctx in chunks])
    fast = d_min >= -HG_FAST_RANGE

    @pl.when(fast)
    def _fast():
        _hgrn_back_fast(chunks, heads, g_ref, gain_ref, out_ref)

    @pl.when(jnp.logical_not(fast))
    def _exact():
        for rows, ch, ctx in chunks:
            _hgrn_back_exact(rows, ch, ctx, heads, g_ref, gain_ref, out_ref, cb_ref, sel_ref)


_HgrnChunk = collections.namedtuple("_HgrnChunk", "q k b d v_bf o_blocks s_far qn")


def _hgrn_front(rows, state, heads, gw, q_ref, k_ref, v_ref, lf_ref, tri_ref):
    L, SUB = HG_L, HG_SUB

    b = sum(jnp.dot(tri_ref[...], lf_ref[0, rows, p * gw:(p + 1) * gw],
                    preferred_element_type=F32) for p in range(LOGF_PIECES))
    d = b - jnp.concatenate([jnp.broadcast_to(b[r0:r0 + 1], (SUB, gw))
                             for r0 in range(0, L, SUB)], axis=0)
    q = q_ref[0, rows, :].astype(F32)
    k = k_ref[0, rows, :].astype(F32)
    v_bf = v_ref[0, rows, :]
    v = v_bf.astype(F32)

    qe = (q * jnp.exp2(b)).astype(BF16)
    o_blocks = [[None] * (L // SUB) for _ in heads]
    for g, sl in enumerate(heads):
        o = lax.dot_general(qe[:, sl], state[g].astype(BF16), NT_DIMS,
                            preferred_element_type=F32)
        for i in range(L // SUB):
            o_blocks[g][i] = o[i * SUB:(i + 1) * SUB]

    qn = (q * jnp.exp2(d)).astype(BF16)
    s_far = {}
    for i in range(1, L // SUB):
        r0 = i * SUB
        kt = (k[0:r0] * jnp.exp2(b[r0:r0 + 1] - b[0:r0])).astype(BF16)
        for g, sl in enumerate(heads):
            s_far[i, g] = lax.dot_general(qn[r0:r0 + SUB, sl], kt[:, sl], NT_DIMS,
                                          preferred_element_type=F32)

    b_last = b[L - 1:L]
    kdec = (k * jnp.exp2(b_last - b)).astype(BF16)
    eb_last = jnp.exp2(b_last)
    new_state = []
    for g, sl in enumerate(heads):
        upd = jnp.dot(jnp.transpose(v[:, sl]).astype(BF16), kdec[:, sl],
                      preferred_element_type=F32)
        new_state.append(state[g] * eb_last[:, sl] + upd)
    return new_state, _HgrnChunk(q, k, b, d, v_bf, o_blocks, s_far, qn)


def _hgrn_far(o_blocks, s_far, v_bf, heads, min_distance):
    L, SUB = HG_L, HG_SUB
    out = [list(blocks) for blocks in o_blocks]
    for i in range(1, L // SUB):
        r0 = i * SUB
        keep = (lax.broadcasted_iota(jnp.int32, (SUB, r0), 1) + min_distance
                <= lax.broadcasted_iota(jnp.int32, (SUB, r0), 0) + r0)
        for g, sl in enumerate(heads):
            s = s_far[i, g] if min_distance <= 1 else jnp.where(keep, s_far[i, g], 0.0)
            oi = jnp.dot(s.astype(BF16), v_bf[0:r0, sl], preferred_element_type=F32)
            out[g][i] = out[g][i] + oi
    return out


def _hgrn_finish(o, rows, sl, g_ref, gain_ref, out_ref):
    ms = jnp.mean(o * o, axis=-1, keepdims=True)
    y = o * lax.rsqrt(ms + EPS) * gain_ref[:, sl] * g_ref[0, rows, sl].astype(F32)
    out_ref[0, rows, sl] = y.astype(BF16)


def _hgrn_back_fast(chunks, heads, g_ref, gain_ref, out_ref):
    L, SUB = HG_L, HG_SUB
    row = lax.broadcasted_iota(jnp.int32, (L, L), 0)
    col = lax.broadcasted_iota(jnp.int32, (L, L), 1)
    same = (col <= row) & (col >= (row // SUB) * SUB)
    s_near = []
    for _, _, ctx in chunks:
        kn = (ctx.k * jnp.exp2(-ctx.d)).astype(BF16)
        s_near.append([lax.dot_general(ctx.qn[:, sl], kn[:, sl], NT_DIMS,
                                       preferred_element_type=F32) for sl in heads])
    far = [_hgrn_far(ctx.o_blocks, ctx.s_far, ctx.v_bf, heads, min_distance=1)
           for _, _, ctx in chunks]
    near = [[jnp.dot(jnp.where(same, s_near[n][g], 0.0).astype(BF16), ctx.v_bf[:, sl],
                     preferred_element_type=F32) for g, sl in enumerate(heads)]
            for n, (_, _, ctx) in enumerate(chunks)]
    for n, (rows, _, _) in enumerate(chunks):
        for g, sl in enumerate(heads):
            _hgrn_finish(jnp.concatenate(far[n][g], axis=0) + near[n][g], rows, sl,
                         g_ref, gain_ref, out_ref)


def _hgrn_back_exact(rows, ch, ctx, heads, g_ref, gain_ref, out_ref, cb_ref, sel_ref):
    L, SUB = HG_L, HG_SUB
    q, k, b, v_bf = ctx.q, ctx.k, ctx.b, ctx.v_bf
    o_blocks = _hgrn_far(ctx.o_blocks, ctx.s_far, v_bf, heads, min_distance=SUB)

    c = b - jnp.log2(k)
    for g, sl in enumerate(heads):
        cb_ref[ch, g, SUB:SUB + L, :] = c[:, sl]

    near = []
    for g, sl in enumerate(heads):
        qg, bg = q[:, sl], b[:, sl]
        e = [(qg * jnp.exp2(bg - cb_ref[ch, g, SUB - off:SUB - off + L, :])).astype(BF16)
             for off in range(SUB)]
        near.append(jnp.dot(jnp.concatenate(e, axis=1), sel_ref[...],
                            preferred_element_type=F32))

    for g, sl in enumerate(heads):
        s = pltpu.roll(near[g], LANES - (SUB - 1), axis=1, stride=1, stride_axis=0)
        o = jnp.concatenate(o_blocks[g], axis=0) + jnp.dot(
            s[:, 0:L].astype(BF16), v_bf[:, sl], preferred_element_type=F32)
        _hgrn_finish(o, rows, sl, g_ref, gain_ref, out_ref)


def _hgrn(proj3, logf3, gain, cast_srcs, *, d_a):
    bsz, t, _ = proj3.shape
    n_g = d_a // LANES
    rows = HG_CHUNKS * HG_L
    grid = (bsz, t // rows)
    cast_arrays, cast_in_specs, cast_out_specs, cast_out_shapes = _cast_plan(cast_srcs, grid)

    def sec(s):
        return pl.BlockSpec((1, rows, d_a), lambda b, c, s=s: (b, c, s))

    return pl.pallas_call(
        functools.partial(_hgrn_kernel, n_cast=len(cast_srcs)),
        out_shape=(jax.ShapeDtypeStruct((bsz, t, d_a), BF16), *cast_out_shapes),
        grid=grid,
        in_specs=[sec(0), sec(1), sec(2), sec(3),
                  pl.BlockSpec((1, rows, LOGF_PIECES * d_a), lambda b, c: (b, c, 0)),
                  pl.BlockSpec((1, d_a), lambda b, c: (0, 0)),
                  *cast_in_specs],
        out_specs=(pl.BlockSpec((1, rows, d_a), lambda b, c: (b, c, 0)), *cast_out_specs),
        scratch_shapes=[pltpu.VMEM((n_g, LANES, LANES), F32),
                        pltpu.VMEM((HG_CHUNKS, n_g, HG_L + HG_SUB, LANES), F32),
                        pltpu.VMEM((HG_SUB * LANES, LANES), BF16),
                        pltpu.VMEM((HG_L, HG_L), BF16)],
        compiler_params=pltpu.CompilerParams(
            dimension_semantics=("arbitrary", "arbitrary")),
        name="hgrn",
    )(proj3, proj3, proj3, proj3, logf3, gain, *cast_arrays)


def _attn_kernel(q_ref, k_ref, v_ref, gvec_ref, *rest, n_cast):
    cast_in = rest[:n_cast]
    out_ref = rest[n_cast]
    cast_out = rest[n_cast + 1:2 * n_cast + 1]
    band_ref, bias_ref, vx_ref = rest[-3:]
    _cast_blocks(cast_in, cast_out)
    t = q_ref.shape[1]
    nq = AT_Q // CHUNK
    nk = nq + N_PAST_CHUNKS
    n_tb = REL_PAST // CHUNK + 2
    wb = (nk + nq - 1) * CHUNK

    @pl.when(pl.program_id(1) == 0)
    def _build_bias():
        grow = gvec_ref[0] * LOG2E
        xb = jnp.broadcast_to(grow, (CHUNK, 2 * LANES))
        r = lax.broadcasted_iota(jnp.int32, (CHUNK, 2 * LANES), 0)
        for bit in range(6):
            xb = jnp.where(((r >> bit) & 1) == 1, pltpu.roll(xb, 1 << bit, axis=1), xb)
        const = grow[:, 0:1]

        n_const = nk - nq + 1 - n_tb
        lo = (nq - 1) * CHUNK
        band_ref[:, 0:lo] = jnp.full((CHUNK, lo), MASK_VALUE, F32)
        band_ref[:, lo:lo + n_const * CHUNK] = jnp.broadcast_to(const, (CHUNK, n_const * CHUNK))
        band_ref[:, lo + n_const * CHUNK:lo + (n_const + n_tb) * CHUNK] = xb[:, CHUNK:]
        band_ref[:, wb - lo:wb] = jnp.full((CHUNK, lo), MASK_VALUE, F32)
        for qi in range(nq):
            off = (nq - 1 - qi) * CHUNK
            bias_ref[qi * CHUNK:(qi + 1) * CHUNK, :] = band_ref[:, off:off + nk * CHUNK]

    vx_ref[:, 0:LANES] = v_ref[0]
    vx_ref[:, LANES:2 * LANES] = jnp.ones((t, LANES), BF16)

    def window(g):
        q0 = g * AT_Q
        ks = max(0, q0 - N_PAST_CHUNKS * CHUNK)
        return q0, ks, q0 + AT_Q - ks

    def scores(g):
        q0, ks, kw = window(g)
        return lax.dot_general(q_ref[0, q0:q0 + AT_Q, :], k_ref[0, ks:ks + kw, :], NT_DIMS,
                               preferred_element_type=F32)

    n_groups = t // AT_Q
    s_next = scores(0)
    for g in range(n_groups):
        q0, ks, kw = window(g)
        s = s_next + bias_ref[:, nk * CHUNK - kw:nk * CHUNK]
        if g + 1 < n_groups:
            s_next = scores(g + 1)
        p = jnp.exp2(s - jnp.max(s, axis=-1, keepdims=True))
        ox = jnp.dot(p.astype(BF16), vx_ref[ks:ks + kw, :], preferred_element_type=F32)
        out_ref[0, q0:q0 + AT_Q, :] = (ox[:, 0:LANES] / ox[:, LANES:2 * LANES]).astype(BF16)


def _attn(proj3, gvec, cast_srcs, *, col0, n_heads):
    bsz, t, _ = proj3.shape
    d_b = n_heads * LANES
    nq = AT_Q // CHUNK
    nk = nq + N_PAST_CHUNKS
    blk0 = col0 // LANES
    grid = (n_heads, bsz)
    cast_arrays, cast_in_specs, cast_out_specs, cast_out_shapes = _cast_plan(cast_srcs, grid)

    def sec(s):
        return pl.BlockSpec((1, t, LANES), lambda h, b, s=s: (b, 0, blk0 + s * n_heads + h))

    return pl.pallas_call(
        functools.partial(_attn_kernel, n_cast=len(cast_srcs)),
        out_shape=(jax.ShapeDtypeStruct((bsz, t, d_b), BF16), *cast_out_shapes),
        grid=grid,
        in_specs=[sec(0), sec(1), sec(2),
                  pl.BlockSpec((1, 1, 2 * LANES), lambda h, b: (h, 0, 0)),
                  *cast_in_specs],
        out_specs=(pl.BlockSpec((1, t, LANES), lambda h, b: (b, 0, h)), *cast_out_specs),
        scratch_shapes=[pltpu.VMEM((CHUNK, (nk + nq - 1) * CHUNK), F32),
                        pltpu.VMEM((AT_Q, nk * CHUNK), F32),
                        pltpu.VMEM((t, 2 * LANES), BF16)],
        compiler_params=pltpu.CompilerParams(
            dimension_semantics=("arbitrary", "arbitrary")),
        name="attn",
    )(proj3, proj3, proj3, gvec, *cast_arrays)


def _merge_kernel(ya_ref, yb_ref, ga0_ref, ga1_ref, gb0_ref, gb1_ref, x_ref,
                  wa_ref, wb_ref, wo_ref, gain_ref, x1_ref, h2_ref):
    pa = jnp.dot(ya_ref[...], wa_ref[...], preferred_element_type=F32)
    pb = jnp.dot(yb_ref[...], wb_ref[...], preferred_element_type=F32)
    ga = jnp.concatenate([ga0_ref[...], ga1_ref[...]], axis=1).astype(F32)
    gb = jnp.concatenate([gb0_ref[...], gb1_ref[...]], axis=1).astype(F32)
    merged = (ga * pa + gb * pb).astype(BF16)
    x1 = x_ref[...] + jnp.dot(merged, wo_ref[...], preferred_element_type=F32)
    x1_ref[...] = x1
    ms = jnp.mean(x1 * x1, axis=-1, keepdims=True)
    h2_ref[...] = (x1 * lax.rsqrt(ms + EPS) * gain_ref[...]).astype(BF16)


def _merge(ya, yb, proj, x2, wa, wb, wo, gain, *, gate_col0):
    m, d = x2.shape
    da, db = ya.shape[1], yb.shape[1]
    tm = MG_TM
    gblk = gate_col0 // IN_TN

    def gate(s):
        return pl.BlockSpec((tm, IN_TN), lambda i, s=s: (i, gblk + s))

    def whole(a):
        return pl.BlockSpec(a.shape, lambda i: (0, 0), pipeline_mode=pl.Buffered(1))

    return pl.pallas_call(
        _merge_kernel,
        out_shape=(jax.ShapeDtypeStruct((m, d), F32),
                   jax.ShapeDtypeStruct((m, d), BF16)),
        grid=(m // tm,),
        in_specs=[pl.BlockSpec((tm, da), lambda i: (i, 0)),
                  pl.BlockSpec((tm, db), lambda i: (i, 0)),
                  gate(0), gate(1), gate(2), gate(3),
                  pl.BlockSpec((tm, d), lambda i: (i, 0)),
                  whole(wa), whole(wb), whole(wo), whole(gain)],
        out_specs=(pl.BlockSpec((tm, d), lambda i: (i, 0)),
                   pl.BlockSpec((tm, d), lambda i: (i, 0))),
        compiler_params=pltpu.CompilerParams(
            dimension_semantics=("arbitrary",),
            vmem_limit_bytes=VMEM_LIMIT_BYTES),
        name="merge",
    )(ya, yb, proj, proj, proj, proj, x2, wa, wb, wo, gain)


def _ffn_kernel(h_ref, wg_ref, wu_ref, wd_ref, x1_hbm, out_ref, x1_ref, sem):
    i, f = pl.program_id(0), pl.program_id(1)
    tm = out_ref.shape[0]

    def residual_copy():
        return pltpu.make_async_copy(x1_hbm.at[pl.ds(i * tm, tm), :], x1_ref, sem)

    def hidden_tile():
        h = h_ref[...]
        gate = jnp.dot(h, wg_ref[...], preferred_element_type=F32)
        up = jnp.dot(h, wu_ref[...], preferred_element_type=F32)
        act = (gate * _sigmoid(gate) * up).astype(BF16)
        return jnp.dot(act, wd_ref[...].astype(BF16), preferred_element_type=F32)

    last = pl.num_programs(1) - 1

    @pl.when(f == 0)
    def _first():
        residual_copy().start()
        out_ref[...] = hidden_tile()

    @pl.when((f > 0) & (f < last))
    def _middle():
        out_ref[...] += hidden_tile()

    @pl.when(f == last)
    def _last():
        residual_copy().wait()
        out_ref[...] += hidden_tile() + x1_ref[...]


def _ffn(h2, x1, w_gate, w_up, w_down):
    m, d = h2.shape
    dff = w_down.shape[0]
    tm, tf = FF_TM, FF_TF
    nf = dff // tf
    return pl.pallas_call(
        _ffn_kernel,
        out_shape=jax.ShapeDtypeStruct((m, d), F32),
        grid=(m // tm, nf),
        in_specs=[pl.BlockSpec((tm, d), lambda i, f: (i, 0)),
                  pl.BlockSpec((d, tf), lambda i, f: (0, f)),
                  pl.BlockSpec((d, tf), lambda i, f: (0, f)),
                  pl.BlockSpec((tf, d), lambda i, f: (f, 0)),
                  pl.BlockSpec(memory_space=pl.ANY)],
        out_specs=pl.BlockSpec((tm, d), lambda i, f: (i, 0)),
        scratch_shapes=[pltpu.VMEM((tm, d), F32), pltpu.SemaphoreType.DMA(())],
        compiler_params=pltpu.CompilerParams(
            dimension_semantics=("arbitrary", "arbitrary"),
            vmem_limit_bytes=VMEM_LIMIT_BYTES),
        name="ffn",
    )(h2, w_gate, w_up, w_down, x1)


def _bias_vector(rel_bias):
    assert rel_bias.shape[1] == N_REL and N_REL + CHUNK <= 2 * LANES
    rev = rel_bias[:, ::-1]
    pad = 2 * LANES - N_REL
    return jnp.pad(rev, ((0, 0), (pad, 0)), mode="edge")[:, None, :]


def kernel(x, w_in, b_gate, norm_mix, norm_ffn, hgrn_lb_logits, hgrn_out_gain,
           q_gain, k_gain, rel_bias, w_proj_a, w_proj_b, w_out, w_ffn_in, w_ffn_out):
    bsz, t, d = x.shape
    depth = w_in.shape[0]
    d_a = hgrn_out_gain.shape[1]
    dh = q_gain.shape[1]
    n_heads_b = rel_bias.shape[1]
    d_b = n_heads_b * dh
    n_in = w_in.shape[2]
    gate_col0 = 4 * d_a + 3 * d_b
    assert dh == LANES and d_a == IN_TN and d_b == IN_TN and depth == 1
    assert hgrn_lb_logits.shape[0] == depth + 1
    assert n_in == gate_col0 + 2 * d and t % AT_Q == 0 and t % (HG_CHUNKS * HG_L) == 0
    assert (bsz * t) % IN_TM == 0 and (bsz * t) % FF_TM == 0 and (bsz * t) % MG_TM == 0

    m = bsz * t
    x2 = x.reshape(m, d)
    for l in range(depth):
        proj, logf = _in_proj(
            x2, norm_mix[l][None, :], w_in[l], b_gate[l][None, :], hgrn_lb_logits,
            q_gain[l][None, :], k_gain[l][None, :], scale=dh ** -0.5 * LOG2E)
        proj3 = proj.reshape(bsz, t, n_in)
        y_a, w_gate_bf = _hgrn(
            proj3, logf.reshape(bsz, t, LOGF_PIECES * d_a), hgrn_out_gain[l][None, :],
            ((w_ffn_in[l], 0, 2),), d_a=d_a)
        y_b, wa_bf, wb_bf, wo_bf, w_up_bf = _attn(
            proj3, _bias_vector(rel_bias[l]),
            (w_proj_a[l], w_proj_b[l], w_out[l], (w_ffn_in[l], 1, 2)),
            col0=4 * d_a, n_heads=n_heads_b)
        x1, h2 = _merge(y_a.reshape(m, d_a), y_b.reshape(m, d_b), proj, x2,
                        wa_bf, wb_bf, wo_bf, norm_ffn[l][None, :], gate_col0=gate_col0)
        x2 = _ffn(h2, x1, w_gate_bf, w_up_bf, w_ffn_out[l])
    return x2.reshape(bsz, t, d)
```

```python
import collections
import functools

import jax
import jax.numpy as jnp
from jax import lax
from jax.experimental import pallas as pl
from jax.experimental.pallas import tpu as pltpu

F32 = jnp.float32
BF16 = jnp.bfloat16

EPS = 1e-6
LANES = 128
BF16_SUBLANES = 16
VMEM_LIMIT_BYTES = 56 * 1024 * 1024
CHUNK = 64
N_PAST_CHUNKS = 8
REL_FUTURE = CHUNK - 1
REL_PAST = 2 * CHUNK - 1
N_REL = REL_FUTURE + REL_PAST + 1
MASK_VALUE = -1e30
BIG_EXPONENT = 1e30
LOGF_PIECES = 3
LOG2E = 1.4426950408889634

IN_TM = 1024
IN_TN = 1024
IN_SUB = 256
IN_W_RING = 3
IN_NORM_ROWS = 256
IN_FIRST_GATE_SECTION = 7
HG_L = 128
HG_CHUNKS = 2
HG_FAST_RANGE = 100.0
HG_SUB = 32
AT_Q = 4 * CHUNK
AT_HEADS = 4
MG_TM = 512
FF_TM = 1024
FF_TF = 512

NT_DIMS = (((1,), (1,)), ((), ()))


def _sigmoid(x):
    return 0.5 * jnp.tanh(0.5 * x) + 0.5


def _cast_plan(srcs, grid):
    steps, strides = 1, []
    for g in reversed(grid):
        strides.insert(0, steps)
        steps *= g

    def row_block(*ids):
        return sum(i * s for i, s in zip(ids, strides))

    arrays, in_specs, out_specs, out_shapes = [], [], [], []
    for src in srcs:
        a, part, n_parts = src if isinstance(src, tuple) else (src, 0, 1)
        rows, cols = a.shape[0], a.shape[1] // n_parts
        assert rows % (steps * BF16_SUBLANES) == 0 and cols % LANES == 0, (a.shape, grid)
        arrays.append(a)
        in_specs.append(pl.BlockSpec((rows // steps, cols),
                                     lambda *ids, part=part: (row_block(*ids), part)))
        out_specs.append(pl.BlockSpec((rows // steps, cols),
                                      lambda *ids: (row_block(*ids), 0)))
        out_shapes.append(jax.ShapeDtypeStruct((rows, cols), BF16))
    return arrays, in_specs, out_specs, out_shapes


def _cast_blocks(src_refs, dst_refs):
    for src, dst in zip(src_refs, dst_refs):
        dst[...] = src[...].astype(BF16)


def _in_proj_kernel(x_hbm, gain_ref, w_hbm, bias_ref, lbl_ref, qg_ref, kg_ref, *rest,
                    scale):
    out_ref, logf_ref, h_ref, x_ref, wbuf_ref, x_sem, w_sems = rest
    i, j = pl.program_id(0), pl.program_id(1)
    tm = x_ref.shape[0]
    tn = wbuf_ref.shape[2]

    n_j = pl.num_programs(1)
    step = i * n_j + j

    def w_copy(s):
        slot = lax.rem(s, IN_W_RING)
        col = pl.multiple_of(lax.rem(s, n_j) * tn, tn)
        return pltpu.make_async_copy(w_hbm.at[:, pl.ds(col, tn)], wbuf_ref.at[slot],
                                     w_sems.at[slot])

    @pl.when(step == 0)
    def _prime():
        for s in range(IN_W_RING - 1):
            w_copy(s).start()

    w_copy(step).wait()

    @pl.when(step + IN_W_RING - 1 < pl.num_programs(0) * n_j)
    def _request():
        w_copy(step + IN_W_RING - 1).start()

    w_ref = wbuf_ref.at[lax.rem(step, IN_W_RING)]

    def x_copy(tile):
        return pltpu.make_async_copy(x_hbm.at[pl.ds(tile * tm, tm), :], x_ref, x_sem)

    @pl.when((j == 0) & (i == 0))
    def _first_fetch():
        x_copy(0).start()

    def norm():
        x_copy(i).wait()
        for r in range(0, tm, IN_NORM_ROWS):
            x = x_ref[r:r + IN_NORM_ROWS, :]
            ms = jnp.mean(x * x, axis=-1, keepdims=True)
            h_ref[r:r + IN_NORM_ROWS, :] = (x * lax.rsqrt(ms + EPS) * gain_ref[...]).astype(BF16)

    @pl.when((j == 1) & (i + 1 < pl.num_programs(0)))
    def _next_fetch():
        x_copy(i + 1).start()

    def section(epilogue):
        for c in range(w_ref.shape[1] // IN_SUB):
            cs = slice(c * IN_SUB, (c + 1) * IN_SUB)
            epilogue(jnp.dot(h_ref[...], w_ref[:, cs].astype(BF16),
                             preferred_element_type=F32), cs)

    def silu_epilogue(acc, cs):
        out_ref[:, cs] = (acc * _sigmoid(acc)).astype(BF16)

    @pl.when(j == 0)
    def _norm_and_silu():
        norm()
        section(silu_epilogue)

    @pl.when(j == 3)
    def _silu():
        section(silu_epilogue)

    @pl.when(j == 1)
    def _forget():
        l = lbl_ref[...]
        e = jnp.exp(l - jnp.max(l, axis=0, keepdims=True))
        lb_row = e[0:1] / jnp.sum(e, axis=0, keepdims=True)

        def epilogue(acc, cs):
            lb = lb_row[:, cs]
            sig = _sigmoid(acc)
            lf = jnp.log2(lb + (1.0 - lb) * sig)
            hi = lf.astype(BF16)
            r1 = lf - hi.astype(F32)
            mid = r1.astype(BF16)
            tn = w_ref.shape[1]
            logf_ref[:, cs] = hi
            logf_ref[:, slice(tn + cs.start, tn + cs.stop)] = mid
            logf_ref[:, slice(2 * tn + cs.start, 2 * tn + cs.stop)] = (
                r1 - mid.astype(F32)).astype(BF16)
            out_ref[:, cs] = ((1.0 - lb) * (1.0 - sig)).astype(BF16)
        section(epilogue)

    @pl.when((j == 2) | (j == 6))
    def _plain():
        def epilogue(acc, cs):
            out_ref[:, cs] = acc.astype(BF16)
        section(epilogue)

    def head_norm(gain_ref_, mult):
        def epilogue(acc, cs):
            for h in range(IN_SUB // LANES):
                t = acc[:, h * LANES:(h + 1) * LANES]
                sl = slice(cs.start + h * LANES, cs.start + (h + 1) * LANES)
                ms = jnp.mean(t * t, axis=-1, keepdims=True)
                out_ref[:, sl] = (t * lax.rsqrt(ms + EPS) * (gain_ref_[...] * mult)).astype(BF16)
        section(epilogue)

    @pl.when(j == 4)
    def _qnorm():
        head_norm(qg_ref, scale)

    @pl.when(j == 5)
    def _knorm():
        head_norm(kg_ref, 1.0)

    @pl.when(j >= IN_FIRST_GATE_SECTION)
    def _gates():
        def epilogue(acc, cs):
            out_ref[:, cs] = _sigmoid(acc + bias_ref[:, cs]).astype(BF16)
        section(epilogue)


def _in_proj(x2, gain, w, b_gate, lb_logits, qg, kg, *, scale):
    m, d = x2.shape
    n = w.shape[1]
    tm, tn = IN_TM, IN_TN
    grid = (m // tm, n // tn)
    first_gate = (n - b_gate.shape[1]) // tn
    assert first_gate == IN_FIRST_GATE_SECTION
    return pl.pallas_call(
        functools.partial(_in_proj_kernel, scale=scale),
        out_shape=(jax.ShapeDtypeStruct((m, n), BF16),
                   jax.ShapeDtypeStruct((m, LOGF_PIECES * tn), BF16)),
        grid=grid,
        in_specs=[
            pl.BlockSpec(memory_space=pl.ANY),
            pl.BlockSpec((1, d), lambda i, j: (0, 0)),
            pl.BlockSpec(memory_space=pl.ANY),
            pl.BlockSpec((1, tn), lambda i, j: (0, jnp.maximum(j - first_gate, 0))),
            pl.BlockSpec(lb_logits.shape, lambda i, j: (0, 0)),
            pl.BlockSpec((1, LANES), lambda i, j: (0, 0)),
            pl.BlockSpec((1, LANES), lambda i, j: (0, 0)),
        ],
        out_specs=(pl.BlockSpec((tm, tn), lambda i, j: (i, j)),
                   pl.BlockSpec((tm, LOGF_PIECES * tn), lambda i, j: (i, 0))),
        scratch_shapes=[pltpu.VMEM((tm, d), BF16), pltpu.VMEM((tm, d), F32),
                        pltpu.VMEM((IN_W_RING, d, tn), F32),
                        pltpu.SemaphoreType.DMA(()), pltpu.SemaphoreType.DMA((IN_W_RING,))],
        compiler_params=pltpu.CompilerParams(
            dimension_semantics=("arbitrary", "arbitrary"),
            vmem_limit_bytes=VMEM_LIMIT_BYTES),
        name="in_proj",
    )(x2, gain, w, b_gate, lb_logits, qg, kg)


def _hgrn_kernel(q_ref, k_ref, v_ref, g_ref, lf_ref, gain_ref, *rest, n_cast):
    cast_in = rest[:n_cast]
    out_ref = rest[n_cast]
    cast_out = rest[n_cast + 1:2 * n_cast + 1]
    st_ref, cb_ref, sel_ref, tri_ref = rest[-4:]
    _cast_blocks(cast_in, cast_out)
    n_g = st_ref.shape[0]
    L, SUB = HG_L, HG_SUB

    @pl.when(pl.program_id(1) == 0)
    def _init():
        st_ref[...] = jnp.zeros_like(st_ref)
        cb_ref[...] = jnp.full(cb_ref.shape, BIG_EXPONENT, F32)
        r = lax.broadcasted_iota(jnp.int32, sel_ref.shape, 0)
        c = lax.broadcasted_iota(jnp.int32, sel_ref.shape, 1)
        sel_ref[...] = jnp.where(c == SUB - 1 - r // LANES, 1.0, 0.0).astype(BF16)
        row = lax.broadcasted_iota(jnp.int32, (L, L), 0)
        col = lax.broadcasted_iota(jnp.int32, (L, L), 1)
        tri_ref[...] = jnp.where(row >= col, 1.0, 0.0).astype(BF16)

    heads = [slice(g * LANES, (g + 1) * LANES) for g in range(n_g)]
    gw = n_g * LANES
    state = [st_ref[g] for g in range(n_g)]
    chunks = []
    for ch in range(HG_CHUNKS):
        rows = slice(ch * L, (ch + 1) * L)
        state, ctx = _hgrn_front(rows, state, heads, gw, q_ref, k_ref, v_ref, lf_ref,
                                 tri_ref)
        chunks.append((rows, ch, ctx))
    for g in range(n_g):
        st_ref[g] = state[g]

    d_min = functools.reduce(jnp.minimum, [jnp.min(ctx.d) for _, _, ctx in chunks])
    fast = d_min >= -HG_FAST_RANGE

    @pl.when(fast)
    def _fast():
        _hgrn_back_fast(chunks, heads, g_ref, gain_ref, out_ref)

    @pl.when(jnp.logical_not(fast))
    def _exact():
        for rows, ch, ctx in chunks:
            _hgrn_back_exact(rows, ch, ctx, heads, g_ref, gain_ref, out_ref, cb_ref, sel_ref)


_HgrnChunk = collections.namedtuple("_HgrnChunk", "q k b d v_bf o_blocks s_far qn")


def _hgrn_front(rows, state, heads, gw, q_ref, k_ref, v_ref, lf_ref, tri_ref):
    L, SUB = HG_L, HG_SUB

    b = sum(jnp.dot(tri_ref[...], lf_ref[0, rows, p * gw:(p + 1) * gw],
                    preferred_element_type=F32) for p in range(LOGF_PIECES))
    d = b - jnp.concatenate([jnp.broadcast_to(b[r0:r0 + 1], (SUB, gw))
                             for r0 in range(0, L, SUB)], axis=0)
    q = q_ref[0, rows, :].astype(F32)
    k = k_ref[0, rows, :].astype(F32)
    v_bf = v_ref[0, rows, :]
    v = v_bf.astype(F32)

    qe = (q * jnp.exp2(b)).astype(BF16)
    o_blocks = [[None] * (L // SUB) for _ in heads]
    for g, sl in enumerate(heads):
        o = lax.dot_general(qe[:, sl], state[g].astype(BF16), NT_DIMS,
                            preferred_element_type=F32)
        for i in range(L // SUB):
            o_blocks[g][i] = o[i * SUB:(i + 1) * SUB]

    qn = (q * jnp.exp2(d)).astype(BF16)
    s_far = {}
    for i in range(1, L // SUB):
        r0 = i * SUB
        kt = (k[0:r0] * jnp.exp2(b[r0:r0 + 1] - b[0:r0])).astype(BF16)
        for g, sl in enumerate(heads):
            s_far[i, g] = lax.dot_general(qn[r0:r0 + SUB, sl], kt[:, sl], NT_DIMS,
                                          preferred_element_type=F32)

    b_last = b[L - 1:L]
    kdec = (k * jnp.exp2(b_last - b)).astype(BF16)
    eb_last = jnp.exp2(b_last)
    new_state = []
    for g, sl in enumerate(heads):
        upd = jnp.dot(jnp.transpose(v[:, sl]).astype(BF16), kdec[:, sl],
                      preferred_element_type=F32)
        new_state.append(state[g] * eb_last[:, sl] + upd)
    return new_state, _HgrnChunk(q, k, b, d, v_bf, o_blocks, s_far, qn)


def _hgrn_far(o_blocks, s_far, v_bf, heads, min_distance):
    L, SUB = HG_L, HG_SUB
    out = [list(blocks) for blocks in o_blocks]
    for i in range(1, L // SUB):
        r0 = i * SUB
        keep = (lax.broadcasted_iota(jnp.int32, (SUB, r0), 1) + min_distance
                <= lax.broadcasted_iota(jnp.int32, (SUB, r0), 0) + r0)
        for g, sl in enumerate(heads):
            s = s_far[i, g] if min_distance <= 1 else jnp.where(keep, s_far[i, g], 0.0)
            oi = jnp.dot(s.astype(BF16), v_bf[0:r0, sl], preferred_element_type=F32)
            out[g][i] = out[g][i] + oi
    return out


def _hgrn_finish(o, rows, sl, g_ref, gain_ref, out_ref):
    ms = jnp.mean(o * o, axis=-1, keepdims=True)
    y = o * lax.rsqrt(ms + EPS) * gain_ref[:, sl] * g_ref[0, rows, sl].astype(F32)
    out_ref[0, rows, sl] = y.astype(BF16)


def _hgrn_back_fast(chunks, heads, g_ref, gain_ref, out_ref):
    L, SUB = HG_L, HG_SUB
    row = lax.broadcasted_iota(jnp.int32, (L, L), 0)
    col = lax.broadcasted_iota(jnp.int32, (L, L), 1)
    same = (col <= row) & (col >= (row // SUB) * SUB)
    s_near = []
    for _, _, ctx in chunks:
        kn = (ctx.k * jnp.exp2(-ctx.d)).astype(BF16)
        s_near.append([lax.dot_general(ctx.qn[:, sl], kn[:, sl], NT_DIMS,
                                       preferred_element_type=F32) for sl in heads])
    far = [_hgrn_far(ctx.o_blocks, ctx.s_far, ctx.v_bf, heads, min_distance=1)
           for _, _, ctx in chunks]
    near = [[jnp.dot(jnp.where(same, s_near[n][g], 0.0).astype(BF16), ctx.v_bf[:, sl],
                     preferred_element_type=F32) for g, sl in enumerate(heads)]
            for n, (_, _, ctx) in enumerate(chunks)]
    for n, (rows, _, _) in enumerate(chunks):
        for g, sl in enumerate(heads):
            _hgrn_finish(jnp.concatenate(far[n][g], axis=0) + near[n][g], rows, sl,
                         g_ref, gain_ref, out_ref)


def _hgrn_back_exact(rows, ch, ctx, heads, g_ref, gain_ref, out_ref, cb_ref, sel_ref):
    L, SUB = HG_L, HG_SUB
    q, k, b, v_bf = ctx.q, ctx.k, ctx.b, ctx.v_bf
    o_blocks = _hgrn_far(ctx.o_blocks, ctx.s_far, v_bf, heads, min_distance=SUB)

    c = b - jnp.log2(k)
    for g, sl in enumerate(heads):
        cb_ref[ch, g, SUB:SUB + L, :] = c[:, sl]

    near = []
    for g, sl in enumerate(heads):
        qg, bg = q[:, sl], b[:, sl]
        e = [(qg * jnp.exp2(bg - cb_ref[ch, g, SUB - off:SUB - off + L, :])).astype(BF16)
             for off in range(SUB)]
        near.append(jnp.dot(jnp.concatenate(e, axis=1), sel_ref[...],
                            preferred_element_type=F32))

    for g, sl in enumerate(heads):
        s = pltpu.roll(near[g], LANES - (SUB - 1), axis=1, stride=1, stride_axis=0)
        o = jnp.concatenate(o_blocks[g], axis=0) + jnp.dot(
            s[:, 0:L].astype(BF16), v_bf[:, sl], preferred_element_type=F32)
        _hgrn_finish(o, rows, sl, g_ref, gain_ref, out_ref)


def _hgrn(proj3, logf3, gain, cast_srcs, *, d_a):
    bsz, t, _ = proj3.shape
    n_g = d_a // LANES
    rows = HG_CHUNKS * HG_L
    grid = (bsz, t // rows)
    cast_arrays, cast_in_specs, cast_out_specs, cast_out_shapes = _cast_plan(cast_srcs, grid)

    def sec(s):
        return pl.BlockSpec((1, rows, d_a), lambda b, c, s=s: (b, c, s))

    return pl.pallas_call(
        functools.partial(_hgrn_kernel, n_cast=len(cast_srcs)),
        out_shape=(jax.ShapeDtypeStruct((bsz, t, d_a), BF16), *cast_out_shapes),
        grid=grid,
        in_specs=[sec(0), sec(1), sec(2), sec(3),
                  pl.BlockSpec((1, rows, LOGF_PIECES * d_a), lambda b, c: (b, c, 0)),
                  pl.BlockSpec((1, d_a), lambda b, c: (0, 0)),
                  *cast_in_specs],
        out_specs=(pl.BlockSpec((1, rows, d_a), lambda b, c: (b, c, 0)), *cast_out_specs),
        scratch_shapes=[pltpu.VMEM((n_g, LANES, LANES), F32),
                        pltpu.VMEM((HG_CHUNKS, n_g, HG_L + HG_SUB, LANES), F32),
                        pltpu.VMEM((HG_SUB * LANES, LANES), BF16),
                        pltpu.VMEM((HG_L, HG_L), BF16)],
        compiler_params=pltpu.CompilerParams(
            dimension_semantics=("arbitrary", "arbitrary")),
        name="hgrn",
    )(proj3, proj3, proj3, proj3, logf3, gain, *cast_arrays)


def _attn_kernel(q_ref, k_ref, v_ref, gvec_ref, *rest, n_cast):
    cast_in = rest[:n_cast]
    out_ref = rest[n_cast]
    cast_out = rest[n_cast + 1:2 * n_cast + 1]
    band_ref, bias_ref, vx_ref = rest[-3:]
    _cast_blocks(cast_in, cast_out)
    t = q_ref.shape[1]
    nq = AT_Q // CHUNK
    nk = nq + N_PAST_CHUNKS
    n_tb = REL_PAST // CHUNK + 2
    wb = (nk + nq - 1) * CHUNK

    n_h = vx_ref.shape[0]

    def build_bias(hh):
        grow = gvec_ref[hh] * LOG2E
        xb = jnp.broadcast_to(grow, (CHUNK, 2 * LANES))
        r = lax.broadcasted_iota(jnp.int32, (CHUNK, 2 * LANES), 0)
        for bit in range(6):
            xb = jnp.where(((r >> bit) & 1) == 1, pltpu.roll(xb, 1 << bit, axis=1), xb)
        const = grow[:, 0:1]

        n_const = nk - nq + 1 - n_tb
        lo = (nq - 1) * CHUNK
        band_ref[:, 0:lo] = jnp.full((CHUNK, lo), MASK_VALUE, F32)
        band_ref[:, lo:lo + n_const * CHUNK] = jnp.broadcast_to(const, (CHUNK, n_const * CHUNK))
        band_ref[:, lo + n_const * CHUNK:lo + (n_const + n_tb) * CHUNK] = xb[:, CHUNK:]
        band_ref[:, wb - lo:wb] = jnp.full((CHUNK, lo), MASK_VALUE, F32)
        for qi in range(nq):
            off = (nq - 1 - qi) * CHUNK
            bias_ref[hh, qi * CHUNK:(qi + 1) * CHUNK, :] = band_ref[:, off:off + nk * CHUNK]

    @pl.when(pl.program_id(1) == 0)
    def _build_bias():
        for hh in range(n_h):
            build_bias(hh)

    for hh in range(n_h):
        vx_ref[hh, :, 0:LANES] = v_ref[0, :, hh * LANES:(hh + 1) * LANES]
        vx_ref[hh, :, LANES:2 * LANES] = jnp.ones((t, LANES), BF16)

    def window(g):
        q0 = g * AT_Q
        ks = max(0, q0 - N_PAST_CHUNKS * CHUNK)
        return q0, ks, q0 + AT_Q - ks

    def scores(task):
        g, hh = task
        q0, ks, kw = window(g)
        sl = slice(hh * LANES, (hh + 1) * LANES)
        return lax.dot_general(q_ref[0, q0:q0 + AT_Q, sl], k_ref[0, ks:ks + kw, sl], NT_DIMS,
                               preferred_element_type=F32)

    tasks = [(g, hh) for g in range(t // AT_Q) for hh in range(n_h)]
    s_next = scores(tasks[0])
    for n, (g, hh) in enumerate(tasks):
        q0, ks, kw = window(g)
        s = s_next + bias_ref[hh, :, nk * CHUNK - kw:nk * CHUNK]
        if n + 1 < len(tasks):
            s_next = scores(tasks[n + 1])
        p = jnp.exp2(s - jnp.max(s, axis=-1, keepdims=True))
        ox = jnp.dot(p.astype(BF16), vx_ref[hh, ks:ks + kw, :], preferred_element_type=F32)
        out_ref[0, q0:q0 + AT_Q, hh * LANES:(hh + 1) * LANES] = (
            ox[:, 0:LANES] / ox[:, LANES:2 * LANES]).astype(BF16)


def _attn(proj3, gvec, cast_srcs, *, col0, n_heads):
    bsz, t, _ = proj3.shape
    d_b = n_heads * LANES
    nq = AT_Q // CHUNK
    nk = nq + N_PAST_CHUNKS
    hw = AT_HEADS * LANES
    blk0 = col0 // hw
    assert col0 % hw == 0 and d_b % hw == 0
    grid = (n_heads // AT_HEADS, bsz)
    cast_arrays, cast_in_specs, cast_out_specs, cast_out_shapes = _cast_plan(cast_srcs, grid)

    def sec(s):
        return pl.BlockSpec((1, t, hw), lambda h, b, s=s: (b, 0, blk0 + s * (d_b // hw) + h))

    return pl.pallas_call(
        functools.partial(_attn_kernel, n_cast=len(cast_srcs)),
        out_shape=(jax.ShapeDtypeStruct((bsz, t, d_b), BF16), *cast_out_shapes),
        grid=grid,
        in_specs=[sec(0), sec(1), sec(2),
                  pl.BlockSpec((AT_HEADS, 1, 2 * LANES), lambda h, b: (h, 0, 0)),
                  *cast_in_specs],
        out_specs=(pl.BlockSpec((1, t, hw), lambda h, b: (b, 0, h)), *cast_out_specs),
        scratch_shapes=[pltpu.VMEM((CHUNK, (nk + nq - 1) * CHUNK), F32),
                        pltpu.VMEM((AT_HEADS, AT_Q, nk * CHUNK), F32),
                        pltpu.VMEM((AT_HEADS, t, 2 * LANES), BF16)],
        compiler_params=pltpu.CompilerParams(
            dimension_semantics=("arbitrary", "arbitrary")),
        name="attn",
    )(proj3, proj3, proj3, gvec, *cast_arrays)


def _merge_kernel(ya_ref, yb_ref, ga0_ref, ga1_ref, gb0_ref, gb1_ref, x_ref,
                  wa_ref, wb_ref, wo_ref, gain_ref, x1_ref, h2_ref):
    pa = jnp.dot(ya_ref[...], wa_ref[...], preferred_element_type=F32)
    pb = jnp.dot(yb_ref[...], wb_ref[...], preferred_element_type=F32)
    ga = jnp.concatenate([ga0_ref[...], ga1_ref[...]], axis=1).astype(F32)
    gb = jnp.concatenate([gb0_ref[...], gb1_ref[...]], axis=1).astype(F32)
    merged = (ga * pa + gb * pb).astype(BF16)
    x1 = x_ref[...] + jnp.dot(merged, wo_ref[...], preferred_element_type=F32)
    x1_ref[...] = x1
    ms = jnp.mean(x1 * x1, axis=-1, keepdims=True)
    h2_ref[...] = (x1 * lax.rsqrt(ms + EPS) * gain_ref[...]).astype(BF16)


def _merge(ya, yb, proj, x2, wa, wb, wo, gain, *, gate_col0):
    m, d = x2.shape
    da, db = ya.shape[1], yb.shape[1]
    tm = MG_TM
    gblk = gate_col0 // IN_TN

    def gate(s):
        return pl.BlockSpec((tm, IN_TN), lambda i, s=s: (i, gblk + s))

    def whole(a):
        return pl.BlockSpec(a.shape, lambda i: (0, 0), pipeline_mode=pl.Buffered(1))

    return pl.pallas_call(
        _merge_kernel,
        out_shape=(jax.ShapeDtypeStruct((m, d), F32),
                   jax.ShapeDtypeStruct((m, d), BF16)),
        grid=(m // tm,),
        in_specs=[pl.BlockSpec((tm, da), lambda i: (i, 0)),
                  pl.BlockSpec((tm, db), lambda i: (i, 0)),
                  gate(0), gate(1), gate(2), gate(3),
                  pl.BlockSpec((tm, d), lambda i: (i, 0)),
                  whole(wa), whole(wb), whole(wo), whole(gain)],
        out_specs=(pl.BlockSpec((tm, d), lambda i: (i, 0)),
                   pl.BlockSpec((tm, d), lambda i: (i, 0))),
        compiler_params=pltpu.CompilerParams(
            dimension_semantics=("arbitrary",),
            vmem_limit_bytes=VMEM_LIMIT_BYTES),
        name="merge",
    )(ya, yb, proj, proj, proj, proj, x2, wa, wb, wo, gain)


def _ffn_kernel(h_ref, wg_ref, wu_ref, wd_ref, x1_hbm, out_ref, x1_ref, sem):
    i, f = pl.program_id(0), pl.program_id(1)
    tm = out_ref.shape[0]

    def residual_copy():
        return pltpu.make_async_copy(x1_hbm.at[pl.ds(i * tm, tm), :], x1_ref, sem)

    def hidden_tile():
        h = h_ref[...]
        gate = jnp.dot(h, wg_ref[...], preferred_element_type=F32)
        up = jnp.dot(h, wu_ref[...], preferred_element_type=F32)
        act = (gate * _sigmoid(gate) * up).astype(BF16)
        return jnp.dot(act, wd_ref[...].astype(BF16), preferred_element_type=F32)

    last = pl.num_programs(1) - 1

    @pl.when(f == 0)
    def _first():
        residual_copy().start()
        out_ref[...] = hidden_tile()

    @pl.when((f > 0) & (f < last))
    def _middle():
        out_ref[...] += hidden_tile()

    @pl.when(f == last)
    def _last():
        residual_copy().wait()
        out_ref[...] += hidden_tile() + x1_ref[...]


def _ffn(h2, x1, w_gate, w_up, w_down):
    m, d = h2.shape
    dff = w_down.shape[0]
    tm, tf = FF_TM, FF_TF
    nf = dff // tf
    return pl.pallas_call(
        _ffn_kernel,
        out_shape=jax.ShapeDtypeStruct((m, d), F32),
        grid=(m // tm, nf),
        in_specs=[pl.BlockSpec((tm, d), lambda i, f: (i, 0)),
                  pl.BlockSpec((d, tf), lambda i, f: (0, f)),
                  pl.BlockSpec((d, tf), lambda i, f: (0, f)),
                  pl.BlockSpec((tf, d), lambda i, f: (f, 0)),
                  pl.BlockSpec(memory_space=pl.ANY)],
        out_specs=pl.BlockSpec((tm, d), lambda i, f: (i, 0)),
        scratch_shapes=[pltpu.VMEM((tm, d), F32), pltpu.SemaphoreType.DMA(())],
        compiler_params=pltpu.CompilerParams(
            dimension_semantics=("arbitrary", "arbitrary"),
            vmem_limit_bytes=VMEM_LIMIT_BYTES),
        name="ffn",
    )(h2, w_gate, w_up, w_down, x1)


def _bias_vector(rel_bias):
    assert rel_bias.shape[1] == N_REL and N_REL + CHUNK <= 2 * LANES
    rev = rel_bias[:, ::-1]
    pad = 2 * LANES - N_REL
    return jnp.pad(rev, ((0, 0), (pad, 0)), mode="edge")[:, None, :]


def kernel(x, w_in, b_gate, norm_mix, norm_ffn, hgrn_lb_logits, hgrn_out_gain,
           q_gain, k_gain, rel_bias, w_proj_a, w_proj_b, w_out, w_ffn_in, w_ffn_out):
    bsz, t, d = x.shape
    depth = w_in.shape[0]
    d_a = hgrn_out_gain.shape[1]
    dh = q_gain.shape[1]
    n_heads_b = rel_bias.shape[1]
    d_b = n_heads_b * dh
    n_in = w_in.shape[2]
    gate_col0 = 4 * d_a + 3 * d_b
    assert dh == LANES and d_a == IN_TN and d_b == IN_TN and depth == 1
    assert hgrn_lb_logits.shape[0] == depth + 1
    assert n_in == gate_col0 + 2 * d and t % AT_Q == 0 and t % (HG_CHUNKS * HG_L) == 0
    assert (bsz * t) % IN_TM == 0 and (bsz * t) % FF_TM == 0 and (bsz * t) % MG_TM == 0

    m = bsz * t
    x2 = x.reshape(m, d)
    for l in range(depth):
        proj, logf = _in_proj(
            x2, norm_mix[l][None, :], w_in[l], b_gate[l][None, :], hgrn_lb_logits,
            q_gain[l][None, :], k_gain[l][None, :], scale=dh ** -0.5 * LOG2E)
        proj3 = proj.reshape(bsz, t, n_in)
        y_a, w_gate_bf = _hgrn(
            proj3, logf.reshape(bsz, t, LOGF_PIECES * d_a), hgrn_out_gain[l][None, :],
            ((w_ffn_in[l], 0, 2),), d_a=d_a)
        y_b, wa_bf, wb_bf, wo_bf, w_up_bf = _attn(
            proj3, _bias_vector(rel_bias[l]),
            (w_proj_a[l], w_proj_b[l], w_out[l], (w_ffn_in[l], 1, 2)),
            col0=4 * d_a, n_heads=n_heads_b)
        x1, h2 = _merge(y_a.reshape(m, d_a), y_b.reshape(m, d_b), proj, x2,
                        wa_bf, wb_bf, wo_bf, norm_ffn[l][None, :], gate_col0=gate_col0)
        x2 = _ffn(h2, x1, w_gate_bf, w_up_bf, w_ffn_out[l])
    return x2.reshape(bsz, t, d)
```
